```python
import math
import jax, jax.numpy as jnp
from jax import lax
import numpy as np

D_MODEL = 1024
BATCH = 8
SEQ = 4096
DEPTH = 1

PLE_DIM = 256
RMS_EPS = 1e-6

RWKV_HEADS = 8
RWKV_HEAD_DIM = 64
RWKV_WIDTH = RWKV_HEADS * RWKV_HEAD_DIM
DECAY_LORA = 64
AAA_LORA = 64
GATE_LORA = 128
GN_EPS = 64e-5
RWKV_COLS = 3 * RWKV_WIDTH + DECAY_LORA + AAA_LORA + GATE_LORA

DIFF_HEADS = 4
DIFF_HEAD_DIM = 64
DIFF_QK_WIDTH = DIFF_HEADS * 2 * DIFF_HEAD_DIM
DIFF_V_DIM = 2 * DIFF_HEAD_DIM
DIFF_V_WIDTH = DIFF_HEADS * DIFF_V_DIM
DIFF_COLS = 2 * DIFF_QK_WIDTH + DIFF_V_WIDTH
ROPE_THETA = 10000.0
Q_BLOCK = 128

GATE_COLS = 2 * D_MODEL
IN_COLS = RWKV_COLS + DIFF_COLS + GATE_COLS

N_GROUPS = 4
EXPERTS_PER_GROUP = 8
N_EXPERTS = N_GROUPS * EXPERTS_PER_GROUP
TOP_K = 2
EXPERT_FF = 512
ROW_BLOCK = 128

kernel_name = 'hybrid_rwkv7_diffattn_hmoe_block'


def rms_norm(x, gain, eps=RMS_EPS):
    xf = x.astype(jnp.float32)
    y = xf * lax.rsqrt(jnp.mean(xf * xf, axis=-1, keepdims=True) + eps)
    return (y * gain.astype(jnp.float32)).astype(x.dtype)


def rope_tables(seq, dim):
    inv = ROPE_THETA ** (-jnp.arange(0, dim, 2, dtype=jnp.float32) / dim)
    ang = jnp.arange(seq, dtype=jnp.float32)[:, None] * inv[None, :]
    ang = jnp.concatenate([ang, ang], axis=-1)
    return jnp.cos(ang), jnp.sin(ang)


def apply_rope(x, cos, sin):
    c = cos[:, None, None, :]
    s = sin[:, None, None, :]
    x1, x2 = jnp.split(x, 2, axis=-1)
    rot = jnp.concatenate([-x2, x1], axis=-1)
    return (x.astype(jnp.float32) * c + rot.astype(jnp.float32) * s).astype(x.dtype)


def wkv7_scan(r, decay, k, kk, a, v):
    b, s, h, n = r.shape

    def step(state, inp):
        r_t, w_t, k_t, kk_t, a_t, v_t = inp
        sa = jnp.einsum('bhvk,bhk->bhv', state, -kk_t)
        state = (state * w_t[:, :, None, :]
                 + sa[..., None] * (kk_t * a_t)[:, :, None, :]
                 + v_t[..., None] * k_t[:, :, None, :])
        return state, jnp.einsum('bhvk,bhk->bhv', state, r_t)

    s0 = jnp.zeros((b, h, n, n), jnp.float32)
    xs = (jnp.swapaxes(r, 0, 1), jnp.swapaxes(decay, 0, 1), jnp.swapaxes(k, 0, 1),
          jnp.swapaxes(kk, 0, 1), jnp.swapaxes(a, 0, 1), jnp.swapaxes(v, 0, 1))
    _, y = lax.scan(step, s0, xs)
    return jnp.swapaxes(y, 0, 1)


def rwkv7_branch(pr, mu, w0, w2, a0, a2, g2, k_k, k_a, r_k, ln_w, ln_b):
    b, s, _ = pr.shape
    f32 = jnp.float32
    prev = jnp.pad(pr[:, :-1], ((0, 0), (1, 0), (0, 0)))
    xs = pr + (prev - pr) * mu
    c0 = RWKV_WIDTH
    r, k, v, xw, xa, xg = jnp.split(
        xs, [c0, 2 * c0, 3 * c0, 3 * c0 + DECAY_LORA, 3 * c0 + DECAY_LORA + AAA_LORA], axis=-1)
    w_log = -jax.nn.softplus(-(w0 + jnp.tanh(xw) @ w2).astype(f32)) - 0.5
    decay = jnp.exp(-jnp.exp(w_log))
    a = jax.nn.sigmoid((a0 + xa @ a2).astype(f32))
    g = jax.nn.sigmoid(xg) @ g2

    def heads(t):
        return t.reshape(b, s, RWKV_HEADS, RWKV_HEAD_DIM).astype(f32)

    kk = heads(k * k_k)
    kk = kk / jnp.maximum(jnp.sqrt(jnp.sum(kk * kk, axis=-1, keepdims=True)), 1e-12)
    k_mod = k.astype(f32) * (1.0 + (a - 1.0) * k_a.astype(f32))
    r_h, k_h, v_h = heads(r), heads(k_mod), heads(v)
    y = wkv7_scan(r_h, heads(decay), k_h, kk, heads(a), v_h)
    mean = jnp.mean(y, axis=-1, keepdims=True)
    var = jnp.mean(jnp.square(y - mean), axis=-1, keepdims=True)
    y = ((y - mean) * lax.rsqrt(var + GN_EPS)).reshape(b, s, RWKV_WIDTH)
    y = y * ln_w.astype(f32) + ln_b.astype(f32)
    bonus = jnp.sum(r_h * k_h * r_k.astype(f32), axis=-1, keepdims=True) * v_h
    out = (y + bonus.reshape(b, s, RWKV_WIDTH)) * g.astype(f32)
    return out.astype(pr.dtype)


def diff_attention_branch(pd, q_gain, k_gain, lq1, lk1, lq2, lk2, subln_w, lam_init, cos, sin):
    b, s, _ = pd.shape
    f32 = jnp.float32
    q, k, v = jnp.split(pd, [DIFF_QK_WIDTH, 2 * DIFF_QK_WIDTH], axis=-1)
    q = apply_rope(rms_norm(q.reshape(b, s, DIFF_HEADS, 2, DIFF_HEAD_DIM), q_gain), cos, sin)
    k = apply_rope(rms_norm(k.reshape(b, s, DIFF_HEADS, 2, DIFF_HEAD_DIM), k_gain), cos, sin)
    q = q.transpose(0, 2, 3, 1, 4)
    k = k.transpose(0, 2, 3, 1, 4)
    v = v.reshape(b, s, DIFF_HEADS, DIFF_V_DIM).transpose(0, 2, 1, 3)
    lam = (jnp.exp(jnp.sum(lq1.astype(f32) * lk1.astype(f32)))
           - jnp.exp(jnp.sum(lq2.astype(f32) * lk2.astype(f32))) + lam_init)
    scale = DIFF_HEAD_DIM ** -0.5
    neg = jnp.finfo(f32).min
    blocks = []
    for i in range(s // Q_BLOCK):
        q0 = i * Q_BLOCK
        end = q0 + Q_BLOCK
        sc = jnp.einsum('bhcqd,bhckd->bhcqk', q[:, :, :, q0:end], k[:, :, :, :end]).astype(f32) * scale
        causal = jnp.arange(end)[None, :] <= (q0 + jnp.arange(Q_BLOCK))[:, None]
        prob = jax.nn.softmax(jnp.where(causal, sc, neg), axis=-1)
        attn = prob[:, :, 0] - lam * prob[:, :, 1]
        blocks.append(jnp.einsum('bhqk,bhkv->bhqv', attn.astype(v.dtype), v[:, :, :end]))
    o = jnp.concatenate(blocks, axis=2)
    o = rms_norm(o, subln_w) * (1.0 - lam_init)
    return o.transpose(0, 2, 1, 3).reshape(b, s, DIFF_V_WIDTH)


def hierarchical_moe(h, w_group, b_group, w_er, b_er, w_gate, w_up, w_down):
    b, s, d = h.shape
    t = b * s
    f32 = jnp.float32
    hf = h.reshape(t, d)
    g_prob = jax.nn.softmax((hf @ w_group).astype(f32) + b_group.astype(f32), axis=-1)
    g_top, g_idx = lax.top_k(g_prob, 1)
    e_logits = ((hf @ w_er).astype(f32) + b_er.astype(f32)).reshape(t, N_GROUPS, EXPERTS_PER_GROUP)
    sel = jnp.broadcast_to(g_idx[:, :, None], (t, 1, EXPERTS_PER_GROUP))
    e_logits = jnp.take_along_axis(e_logits, sel, axis=1)[:, 0]
    e_top, e_idx = lax.top_k(e_logits, TOP_K)
    weights = g_top * jax.nn.softmax(e_top, axis=-1)
    expert_id = g_idx * EXPERTS_PER_GROUP + e_idx

    n_assign = t * TOP_K
    eid = expert_id.reshape(n_assign).astype(jnp.int32)
    tid = jnp.repeat(jnp.arange(t, dtype=jnp.int32), TOP_K)
    wt = weights.reshape(n_assign)
    order = jnp.argsort(eid)
    se = eid[order]
    counts = jnp.bincount(eid, length=N_EXPERTS).astype(jnp.int32)
    starts = jnp.cumsum(counts) - counts
    pcounts = ((counts + ROW_BLOCK - 1) // ROW_BLOCK) * ROW_BLOCK
    pends = jnp.cumsum(pcounts)
    pstarts = pends - pcounts
    dest = pstarts[se] + (jnp.arange(n_assign, dtype=jnp.int32) - starts[se])
    rows = n_assign + N_EXPERTS * ROW_BLOCK
    n_blocks = rows // ROW_BLOCK
    buf_tok = jnp.full((rows,), t, jnp.int32).at[dest].set(tid[order])
    buf_w = jnp.zeros((rows,), wt.dtype).at[dest].set(wt[order])
    blk_e = jnp.minimum(jnp.searchsorted(pends, jnp.arange(n_blocks, dtype=jnp.int32) * ROW_BLOCK,
                                         side='right'), N_EXPERTS - 1)
    hpad = jnp.concatenate([hf, jnp.zeros((1, d), hf.dtype)], axis=0)
    xs = hpad[buf_tok].reshape(n_blocks, ROW_BLOCK, d)

    def expert_block(args):
        xb, e = args
        return (jax.nn.silu(xb @ w_gate[e]) * (xb @ w_up[e])) @ w_down[e]

    yb = lax.map(expert_block, (xs, blk_e)).reshape(rows, d)
    yb = yb * buf_w[:, None].astype(yb.dtype)
    out = jax.ops.segment_sum(yb, buf_tok, num_segments=t + 1)[:t]
    return out.reshape(b, s, d)


def setup_inputs(seed: int = 0) -> dict:
    key = jax.random.key(seed)
    ks = iter(jax.random.split(key, 40))
    L = DEPTH
    f32 = jnp.float32

    def nrm(shape, scale):
        return scale * jax.random.normal(next(ks), shape, f32)

    def gain(shape, base=1.0):
        return base + 0.1 * jax.random.normal(next(ks), shape, f32)

    return {
        'x': nrm((BATCH, SEQ, D_MODEL), 1.0),
        'p': nrm((L, BATCH, SEQ, PLE_DIM), 1.0),
        'norm_mix': gain((L, D_MODEL)),
        'w_in': nrm((L, D_MODEL, IN_COLS), D_MODEL ** -0.5),
        'rwkv_mu': jax.random.uniform(next(ks), (L, RWKV_COLS), f32),
        'rwkv_w0': jax.random.uniform(next(ks), (L, RWKV_WIDTH), f32, -6.0, 0.0),
        'rwkv_w2': nrm((L, DECAY_LORA, RWKV_WIDTH), DECAY_LORA ** -0.5),
        'rwkv_a0': nrm((L, RWKV_WIDTH), 0.5),
        'rwkv_a2': nrm((L, AAA_LORA, RWKV_WIDTH), AAA_LORA ** -0.5),
        'rwkv_g2': nrm((L, GATE_LORA, RWKV_WIDTH), GATE_LORA ** -0.5),
        'rwkv_k_k': gain((L, RWKV_WIDTH), 0.85),
        'rwkv_k_a': gain((L, RWKV_WIDTH)),
        'rwkv_r_k': gain((L, RWKV_HEADS, RWKV_HEAD_DIM), 0.5),
        'rwkv_ln_w': gain((L, RWKV_WIDTH)),
        'rwkv_ln_b': nrm((L, RWKV_WIDTH), 0.02),
        'q_norm': gain((L, DIFF_HEAD_DIM)),
        'k_norm': gain((L, DIFF_HEAD_DIM)),
        'lambda_q1': nrm((L, DIFF_HEAD_DIM), 0.1),
        'lambda_k1': nrm((L, DIFF_HEAD_DIM), 0.1),
        'lambda_q2': nrm((L, DIFF_HEAD_DIM), 0.1),
        'lambda_k2': nrm((L, DIFF_HEAD_DIM), 0.1),
        'subln_w': gain((L, DIFF_V_DIM)),
        'w_branch_rwkv': nrm((L, RWKV_WIDTH, D_MODEL), RWKV_WIDTH ** -0.5),
        'w_branch_diff': nrm((L, DIFF_V_WIDTH, D_MODEL), DIFF_V_WIDTH ** -0.5),
        'w_out': nrm((L, D_MODEL, D_MODEL), D_MODEL ** -0.5),
        'norm_ffn': gain((L, D_MODEL)),
        'w_group': nrm((L, D_MODEL, N_GROUPS), D_MODEL ** -0.5),
        'b_group': nrm((L, N_GROUPS), 0.01),
        'w_expert_router': nrm((L, D_MODEL, N_EXPERTS), D_MODEL ** -0.5),
        'b_expert_router': nrm((L, N_EXPERTS), 0.01),
        'w_gate': nrm((L, N_EXPERTS, D_MODEL, EXPERT_FF), D_MODEL ** -0.5),
        'w_up': nrm((L, N_EXPERTS, D_MODEL, EXPERT_FF), D_MODEL ** -0.5),
        'w_down': nrm((L, N_EXPERTS, EXPERT_FF, D_MODEL), EXPERT_FF ** -0.5),
        'norm_ple': gain((L, D_MODEL)),
        'w_ple_gate': nrm((L, D_MODEL, D_MODEL), D_MODEL ** -0.5),
        'w_ple_proj': nrm((L, PLE_DIM, D_MODEL), PLE_DIM ** -0.5),
    }


def reference(x, p, norm_mix, w_in, rwkv_mu, rwkv_w0, rwkv_w2, rwkv_a0, rwkv_a2, rwkv_g2,
              rwkv_k_k, rwkv_k_a, rwkv_r_k, rwkv_ln_w, rwkv_ln_b, q_norm, k_norm,
              lambda_q1, lambda_k1, lambda_q2, lambda_k2, subln_w, w_branch_rwkv,
              w_branch_diff, w_out, norm_ffn, w_group, b_group, w_expert_router,
              b_expert_router, w_gate, w_up, w_down, norm_ple, w_ple_gate, w_ple_proj):
    seq = x.shape[1]
    cos, sin = rope_tables(seq, DIFF_HEAD_DIM)
    for l in range(DEPTH):
        lam_init = 0.8 - 0.6 * math.exp(-0.3 * l)
        h = rms_norm(x, norm_mix[l])
        proj = h @ w_in[l]
        p_rwkv = proj[..., :RWKV_COLS]
        p_diff = proj[..., RWKV_COLS:RWKV_COLS + DIFF_COLS]
        g_rwkv, g_diff = jnp.split(proj[..., RWKV_COLS + DIFF_COLS:], 2, axis=-1)
        o_rwkv = rwkv7_branch(p_rwkv, rwkv_mu[l], rwkv_w0[l], rwkv_w2[l], rwkv_a0[l], rwkv_a2[l],
                              rwkv_g2[l], rwkv_k_k[l], rwkv_k_a[l], rwkv_r_k[l], rwkv_ln_w[l],
                              rwkv_ln_b[l])
        o_diff = diff_attention_branch(p_diff, q_norm[l], k_norm[l], lambda_q1[l], lambda_k1[l],
                                       lambda_q2[l], lambda_k2[l], subln_w[l], lam_init, cos, sin)
        mixed = (jax.nn.sigmoid(g_rwkv) * (o_rwkv @ w_branch_rwkv[l])
                 + jax.nn.sigmoid(g_diff) * (o_diff @ w_branch_diff[l]))
        x = x + mixed @ w_out[l]
        x = x + hierarchical_moe(rms_norm(x, norm_ffn[l]), w_group[l], b_group[l],
                                 w_expert_router[l], b_expert_router[l], w_gate[l], w_up[l],
                                 w_down[l])
        ple_gate = jax.nn.sigmoid(rms_norm(x, norm_ple[l]) @ w_ple_gate[l])
        x = x + ple_gate * (p[l] @ w_ple_proj[l])
    return x
```

```python
import functools
import math

import jax
import jax.numpy as jnp
from jax import lax
from jax.experimental import pallas as pl
from jax.experimental.pallas import tpu as pltpu

F32 = jnp.float32
BF16 = jnp.bfloat16

D_MODEL = 1024
PLE_DIM = 256
RMS_EPS = 1e-6

RWKV_HEADS = 8
HEAD_DIM = 64
RWKV_WIDTH = RWKV_HEADS * HEAD_DIM
DECAY_LORA = 64
AAA_LORA = 64
GATE_LORA = 128
GN_EPS = 64e-5
RWKV_COLS = 3 * RWKV_WIDTH + DECAY_LORA + AAA_LORA + GATE_LORA

DIFF_HEADS = 4
DIFF_QK_WIDTH = DIFF_HEADS * 2 * HEAD_DIM
DIFF_V_DIM = 2 * HEAD_DIM
DIFF_V_WIDTH = DIFF_HEADS * DIFF_V_DIM
DIFF_COLS = 2 * DIFF_QK_WIDTH + DIFF_V_WIDTH
ROPE_THETA = 10000.0
GATE_COLS = 2 * D_MODEL

N_GROUPS = 4
EXPERTS_PER_GROUP = 8
N_EXPERTS = N_GROUPS * EXPERTS_PER_GROUP
TOP_K = 2
EXPERT_FF = 512
ROW_BLOCK = 128

LANES = 128
CHUNK = 64
VMEM_LIMIT = 56 * 1024 * 1024


def _dot(a, b):
    return jnp.dot(a, b, preferred_element_type=F32)


def _dot_nt(a, b):
    return lax.dot_general(a, b, (((1,), (1,)), ((), ())), preferred_element_type=F32)


def _dot_hi(a, b):
    return jnp.dot(a, b, preferred_element_type=F32, precision=lax.Precision.HIGHEST)


def _group_ones(width, group):
    i = jnp.arange(width) // group
    return (i[:, None] == i[None, :]).astype(BF16)


def _rotate_half(t):
    width = t.shape[-1]
    lane = lax.broadcasted_iota(jnp.int32, t.shape, 1)
    fwd = pltpu.roll(t, width - HEAD_DIM // 2, 1)
    bwd = pltpu.roll(t, HEAD_DIM // 2, 1)
    return jnp.where(lane % HEAD_DIM < HEAD_DIM // 2, -fwd, bwd)


def _inproj_body(x_ref, g_ref, w_ref, cos_ref, sin_ref, qg_ref, kg_ref, ones_ref,
                 prw_ref, q_ref, k_ref, v_ref, gate_ref):
    x = x_ref[...]
    ms = jnp.mean(x * x, axis=-1, keepdims=True)
    h = (x * lax.rsqrt(ms + RMS_EPS) * g_ref[...]).astype(BF16)
    prw_ref[...] = _dot(h, w_ref[:, :RWKV_COLS])
    pd = _dot(h, w_ref[:, RWKV_COLS:RWKV_COLS + DIFF_COLS])
    cos = cos_ref[...]
    sin = sin_ref[...]

    def qk_prep(t, gain, scale):
        ssq = _dot((t * t).astype(BF16), ones_ref[...])
        t = t * lax.rsqrt(ssq * (1.0 / HEAD_DIM) + RMS_EPS) * gain
        return ((t * cos + _rotate_half(t) * sin) * scale).astype(BF16)

    q_ref[...] = qk_prep(pd[:, :DIFF_QK_WIDTH], qg_ref[...], HEAD_DIM ** -0.5)
    k_ref[...] = qk_prep(pd[:, DIFF_QK_WIDTH:2 * DIFF_QK_WIDTH], kg_ref[...], 1.0)
    v_ref[...] = pd[:, 2 * DIFF_QK_WIDTH:].astype(BF16)
    gate_ref[...] = jax.nn.sigmoid(_dot(h, w_ref[:, RWKV_COLS + DIFF_COLS:])).astype(BF16)


def _inproj(x2, norm_g, w_in, cos, sin, q_gain, k_gain, seq, tm):
    t = x2.shape[0]
    n_seq_tiles = seq // tm
    row = lambda i: (i, 0)
    fixed = lambda i: (0, 0)
    rope = lambda i: (i % n_seq_tiles, 0)
    return pl.pallas_call(
        _inproj_body,
        grid=(t // tm,),
        in_specs=[
            pl.BlockSpec((tm, D_MODEL), row),
            pl.BlockSpec((1, D_MODEL), fixed),
            pl.BlockSpec((D_MODEL, RWKV_COLS + DIFF_COLS + GATE_COLS), fixed),
            pl.BlockSpec((tm, DIFF_QK_WIDTH), rope),
            pl.BlockSpec((tm, DIFF_QK_WIDTH), rope),
            pl.BlockSpec((1, DIFF_QK_WIDTH), fixed),
            pl.BlockSpec((1, DIFF_QK_WIDTH), fixed),
            pl.BlockSpec((DIFF_QK_WIDTH, DIFF_QK_WIDTH), fixed),
        ],
        out_specs=[
            pl.BlockSpec((tm, RWKV_COLS), row),
            pl.BlockSpec((tm, DIFF_QK_WIDTH), row),
            pl.BlockSpec((tm, DIFF_QK_WIDTH), row),
            pl.BlockSpec((tm, DIFF_V_WIDTH), row),
            pl.BlockSpec((tm, GATE_COLS), row),
        ],
        out_shape=[
            jax.ShapeDtypeStruct((t, RWKV_COLS), F32),
            jax.ShapeDtypeStruct((t, DIFF_QK_WIDTH), BF16),
            jax.ShapeDtypeStruct((t, DIFF_QK_WIDTH), BF16),
            jax.ShapeDtypeStruct((t, DIFF_V_WIDTH), BF16),
            jax.ShapeDtypeStruct((t, GATE_COLS), BF16),
        ],
        compiler_params=pltpu.CompilerParams(
            dimension_semantics=("parallel",), vmem_limit_bytes=VMEM_LIMIT),
        name="inproj",
    )(x2, norm_g, w_in, cos, sin, q_gain, k_gain, _group_ones(DIFF_QK_WIDTH, HEAD_DIM))


def _attn_body(q_ref, k_ref, v_ref, lq1_ref, lk1_ref, lq2_ref, lk2_ref, sw_ref, o_ref,
               qs_sc, m_sc, l_sc, acc_sc, *, tq, lam_init):
    qi = pl.program_id(2)
    q = q_ref[...]
    lane = lax.broadcasted_iota(jnp.int32, q.shape, 1)
    qs_sc[:tq, :] = jnp.where(lane < HEAD_DIM, q, jnp.zeros_like(q))
    qs_sc[tq:, :] = jnp.where(lane >= HEAD_DIM, q, jnp.zeros_like(q))
    m_sc[...] = jnp.full(m_sc.shape, -jnp.inf, F32)
    l_sc[...] = jnp.zeros(l_sc.shape, F32)
    acc_sc[...] = jnp.zeros(acc_sc.shape, F32)

    def step(ki, masked):
        start = pl.multiple_of(ki * tq, tq)
        kb = k_ref[pl.ds(start, tq), :]
        vb = v_ref[pl.ds(start, tq), :]
        s = _dot_nt(qs_sc[...], kb)
        if masked:
            row = lax.broadcasted_iota(jnp.int32, s.shape, 0) % tq
            col = lax.broadcasted_iota(jnp.int32, s.shape, 1)
            s = jnp.where(col <= row, s, -jnp.inf)
        m_prev = m_sc[...]
        m_new = jnp.maximum(m_prev, jnp.max(s, axis=-1, keepdims=True))
        alpha = jnp.exp(m_prev - m_new)
        p = jnp.exp(s - m_new[:, :1])
        l_sc[...] = alpha * l_sc[...] + jnp.sum(p, axis=-1, keepdims=True)
        acc_sc[...] = alpha * acc_sc[...] + _dot(p.astype(BF16), vb)
        m_sc[...] = m_new

    def body(ki, carry):
        step(ki, False)
        return carry

    lax.fori_loop(0, qi, body, 0)
    step(qi, True)

    lam = (jnp.exp(jnp.sum(lq1_ref[...] * lk1_ref[...], axis=-1, keepdims=True))
           - jnp.exp(jnp.sum(lq2_ref[...] * lk2_ref[...], axis=-1, keepdims=True)) + lam_init)
    o = acc_sc[...] / l_sc[...]
    o = o[:tq] - lam * o[tq:]
    ms = jnp.mean(o * o, axis=-1, keepdims=True)
    o_ref[...] = o * lax.rsqrt(ms + RMS_EPS) * sw_ref[...] * (1.0 - lam_init)


def _diff_attention(q, k, v, lq1, lk1, lq2, lk2, subln_w, batch, seq, lam_init, tq):
    t = q.shape[0]
    nq = seq // tq
    vec = pl.BlockSpec((1, HEAD_DIM), lambda b, h, i: (0, 0))
    return pl.pallas_call(
        functools.partial(_attn_body, tq=tq, lam_init=lam_init),
        grid=(batch, DIFF_HEADS, nq),
        in_specs=[
            pl.BlockSpec((tq, LANES), lambda b, h, i: (b * nq + i, h)),
            pl.BlockSpec((seq, LANES), lambda b, h, i: (b, h)),
            pl.BlockSpec((seq, LANES), lambda b, h, i: (b, h)),
            vec, vec, vec, vec,
            pl.BlockSpec((1, DIFF_V_DIM), lambda b, h, i: (0, 0)),
        ],
        out_specs=pl.BlockSpec((tq, LANES), lambda b, h, i: (b * nq + i, h)),
        out_shape=jax.ShapeDtypeStruct((t, DIFF_V_WIDTH), F32),
        scratch_shapes=[
            pltpu.VMEM((2 * tq, LANES), BF16),
            pltpu.VMEM((2 * tq, LANES), F32),
            pltpu.VMEM((2 * tq, LANES), F32),
            pltpu.VMEM((2 * tq, LANES), F32),
        ],
        compiler_params=pltpu.CompilerParams(
            dimension_semantics=("parallel", "parallel", "arbitrary"),
            vmem_limit_bytes=VMEM_LIMIT),
        name="diff_attn",
    )(q, k, v, lq1, lk1, lq2, lk2, subln_w)


def _dot_tn(a, b):
    return lax.dot_general(a, b, (((0,), (0,)), ((), ())), preferred_element_type=F32)


def _wkv_pair_chunk(rt, kt, bt, kkt, kh, bh, v, pc, hp):
    c = CHUNK
    lane = lax.broadcasted_iota(jnp.int32, (c, LANES), 1)
    m0 = lane < HEAD_DIM
    row = lax.broadcasted_iota(jnp.int32, (LANES, LANES), 0)
    col = lax.broadcasted_iota(jnp.int32, (LANES, LANES), 1)
    same = (row // c) == (col // c)
    strict = same & ((col % c) < (row % c))
    incl = same & ((col % c) <= (row % c))
    eye = row == col

    def stack(t):
        return jnp.concatenate([jnp.where(m0, t, 0.0), jnp.where(m0, 0.0, t)], axis=0)

    def dup(t):
        return jnp.concatenate([t, t], axis=0)

    def fold(t):
        return t[:c] + t[c:]

    lhs = jnp.concatenate([kkt, rt], axis=0).astype(BF16)
    rhs = jnp.concatenate([stack(bt), stack(kt)], axis=0).astype(BF16)
    x = _dot_nt(lhs, rhs)
    l_mat = jnp.where(strict, dup(x[:c, :LANES]), 0.0)
    m_ak = jnp.where(strict, dup(x[:c, LANES:]), 0.0)
    a_rb = jnp.where(incl, dup(x[c:, :LANES]), 0.0)
    a_rk = jnp.where(incl, dup(x[c:, LANES:]), 0.0)

    t_inv = jnp.where(eye, 1.0, 0.0) - l_mat
    l_pow = l_mat
    for _ in range(5):
        l_pow = _dot_hi(l_pow, l_pow)
        t_inv = t_inv + _dot_hi(t_inv, l_pow)

    vs = stack(v)
    vs16 = vs.astype(BF16)
    mv = _dot(m_ak.astype(BF16), vs16)
    z = _dot_hi(t_inv, jnp.concatenate([stack(kkt), mv], axis=1))
    q2 = _dot(a_rb.astype(BF16), (-z).astype(BF16))
    qs = stack(rt) + q2[:, :LANES]
    y0s = q2[:, LANES:] + _dot(a_rk.astype(BF16), vs16)
    q = fold(qs)
    y0 = fold(y0s)
    wu = jnp.concatenate([fold(-z[:, :LANES]), fold(-z[:, LANES:])], axis=1)
    phid = _dot_tn(bh.astype(BF16), wu.astype(BF16))
    phi = jnp.where(eye, jnp.broadcast_to(pc, (LANES, LANES)), 0.0) + jnp.where(same, phid[:, :LANES], 0.0)
    delta = jnp.where(same, phid[:, LANES:] + _dot_tn(kh.astype(BF16), v.astype(BF16)), 0.0)
    hp16 = hp.astype(BF16)
    y = _dot(q.astype(BF16), hp16) + y0
    hp_new = _dot(phi.astype(BF16), hp16) + delta
    return y, hp_new


def _rwkv_body(pr_ref, halo_ref, mu_ref, w0_ref, w2_ref, a0_ref, a2_ref, g2_ref, kk_ref, ka_ref,
               rk_ref, lnw_ref, lnb_ref, ones_ref, tri_ref, o_ref,
               hp_sc, r_sc, km_sc, kn_sc, b_sc, v_sc, lw_sc, cum_sc, y_sc, *, rows):
    i = pl.program_id(1)
    n_pairs = RWKV_WIDTH // LANES
    n_chunks = rows // CHUNK

    @pl.when(i == 0)
    def _():
        hp_sc[...] = jnp.zeros(hp_sc.shape, F32)

    pr = pr_ref[0]
    first = jnp.where(i == 0, 0.0, halo_ref[0, 7:8, :])
    prev = jnp.concatenate([first, pr[:-1]], axis=0)
    xs = pr + (prev - pr) * mu_ref[...]
    w = RWKV_WIDTH
    r = xs[:, :w]
    k = xs[:, w:2 * w]
    v = xs[:, 2 * w:3 * w]
    x_lora = xs[:, 3 * w:3 * w + DECAY_LORA + AAA_LORA]
    xg = xs[:, 3 * w + DECAY_LORA + AAA_LORA:]
    wl = w0_ref[...] + _dot(jnp.tanh(x_lora).astype(BF16), w2_ref[...])
    sp = jnp.maximum(-wl, 0.0) + jnp.log(1.0 + jnp.exp(-jnp.abs(wl)))
    lw = -jnp.exp(-sp - 0.5)
    a = jax.nn.sigmoid(a0_ref[...] + _dot(x_lora.astype(BF16), a2_ref[...]))
    g = _dot(jax.nn.sigmoid(xg).astype(BF16), g2_ref[...])
    ones = ones_ref[...]
    kx = k * kk_ref[...]
    ssq = _dot((kx * kx).astype(BF16), ones)
    kn = kx / jnp.maximum(jnp.sqrt(ssq), 1e-12)
    km = k * (1.0 + (a - 1.0) * ka_ref[...])
    bonus = _dot((r * km * rk_ref[...]).astype(BF16), ones) * v

    hi = lw.astype(BF16)
    r1 = lw - hi.astype(F32)
    mid = r1.astype(BF16)
    lo = (r1 - mid.astype(F32)).astype(BF16)
    tri = tri_ref[...]
    cum = _dot(tri, hi) + _dot(tri, mid) + _dot(tri, lo)

    r_sc[...] = r
    km_sc[...] = km
    kn_sc[...] = kn
    b_sc[...] = kn * a
    v_sc[...] = v
    lw_sc[...] = lw
    cum_sc[...] = cum

    def chunk(ci, carry):
        rs = pl.ds(pl.multiple_of(ci * CHUNK, CHUNK), CHUNK)
        cum_c = cum_sc[rs, :]
        cl = cum_c[CHUNK - 1:CHUNK, :]
        p_inc = jnp.exp(cum_c)
        p_inv = jnp.exp(-cum_c)
        p_exc = jnp.exp(cum_c - lw_sc[rs, :])
        p_end = jnp.exp(cl - cum_c)
        pc = jnp.exp(cl)
        rt = r_sc[rs, :] * p_inc
        km_c = km_sc[rs, :]
        b_c = b_sc[rs, :]
        kt = km_c * p_inv
        bt = b_c * p_inv
        kkt = kn_sc[rs, :] * p_exc
        kh = km_c * p_end
        bh = b_c * p_end
        v_c = v_sc[rs, :]
        for p in range(n_pairs):
            sl = slice(p * LANES, (p + 1) * LANES)
            y, hp_new = _wkv_pair_chunk(rt[:, sl], kt[:, sl], bt[:, sl], kkt[:, sl], kh[:, sl],
                                        bh[:, sl], v_c[:, sl], pc[:, sl], hp_sc[p])
            hp_sc[p] = hp_new
            y_sc[rs, sl] = y
        return carry

    lax.fori_loop(0, n_chunks, chunk, 0)

    y = y_sc[...]
    mean = _dot(y.astype(BF16), ones) * (1.0 / HEAD_DIM)
    yc = y - mean
    var = _dot((yc * yc).astype(BF16), ones) * (1.0 / HEAD_DIM)
    yn = yc * lax.rsqrt(var + GN_EPS) * lnw_ref[...] + lnb_ref[...]
    o_ref[0] = ((yn + bonus) * g).astype(o_ref.dtype)


def _rwkv(pr3, mu, w0, w2p, a0, a2p, g2, k_k, k_a, r_k, ln_w, ln_b, rows):
    batch, seq, _ = pr3.shape
    nb = seq // rows
    fixed = lambda b, i: (0, 0)
    vec = pl.BlockSpec((1, RWKV_WIDTH), fixed)
    lora = DECAY_LORA + AAA_LORA
    n_pairs = RWKV_WIDTH // LANES
    ci = jnp.arange(rows) // CHUNK
    ti = jnp.arange(rows)
    tri = ((ci[:, None] == ci[None, :]) & (ti[None, :] <= ti[:, None])).astype(BF16)
    big = pltpu.VMEM((rows, RWKV_WIDTH), F32)
    return pl.pallas_call(
        functools.partial(_rwkv_body, rows=rows),
        grid=(batch, nb),
        in_specs=[
            pl.BlockSpec((1, rows, RWKV_COLS), lambda b, i: (b, i, 0)),
            pl.BlockSpec((1, 8, RWKV_COLS), lambda b, i: (b, jnp.maximum(i * (rows // 8) - 1, 0), 0)),
            pl.BlockSpec((1, RWKV_COLS), fixed),
            vec,
            pl.BlockSpec((lora, RWKV_WIDTH), fixed),
            vec,
            pl.BlockSpec((lora, RWKV_WIDTH), fixed),
            pl.BlockSpec((GATE_LORA, RWKV_WIDTH), fixed),
            vec, vec, vec, vec, vec,
            pl.BlockSpec((RWKV_WIDTH, RWKV_WIDTH), fixed),
            pl.BlockSpec((rows, rows), fixed),
        ],
        out_specs=pl.BlockSpec((1, rows, RWKV_WIDTH), lambda b, i: (b, i, 0)),
        out_shape=jax.ShapeDtypeStruct((batch, seq, RWKV_WIDTH), F32),
        scratch_shapes=[pltpu.VMEM((n_pairs, LANES, LANES), F32)] + [big] * 8,
        compiler_params=pltpu.CompilerParams(
            dimension_semantics=("parallel", "arbitrary"), vmem_limit_bytes=VMEM_LIMIT),
        name="rwkv7",
    )(pr3, pr3, mu, w0, w2p, a0, a2p, g2, k_k, k_a, r_k, ln_w, ln_b,
      _group_ones(RWKV_WIDTH, HEAD_DIM), tri)


ROUTE_COLS = N_GROUPS + N_EXPERTS


def _merge_body(x_ref, gate_ref, orw_ref, odf_ref, wbr_ref, wbd_ref, wout_ref, nf_ref, wr_ref,
                br_ref, x1_ref, h2_ref, route_ref):
    gates = gate_ref[...].astype(F32)
    mixed = (gates[:, :D_MODEL] * _dot(orw_ref[...].astype(BF16), wbr_ref[...])
             + gates[:, D_MODEL:] * _dot(odf_ref[...].astype(BF16), wbd_ref[...]))
    x1 = x_ref[...] + _dot(mixed.astype(BF16), wout_ref[...])
    x1_ref[...] = x1
    ms = jnp.mean(x1 * x1, axis=-1, keepdims=True)
    h2 = x1 * lax.rsqrt(ms + RMS_EPS) * nf_ref[...]
    h2_ref[...] = h2

    hi = h2.astype(BF16)
    lo = (h2 - hi.astype(F32)).astype(BF16)
    acc = _dot(hi, wr_ref[...])
    logits = acc[:, :LANES] + acc[:, LANES:] + _dot(lo, wr_ref[:, :LANES]) + br_ref[...]
    lane = lax.broadcasted_iota(jnp.int32, logits.shape, 1).astype(F32)
    neg = -jnp.inf

    def top(vals):
        m = jnp.max(vals, axis=-1, keepdims=True)
        return m, jnp.min(jnp.where(vals == m, lane, float(LANES)), axis=-1, keepdims=True)

    gl = jnp.where(lane < N_GROUPS, logits, neg)
    gm, g_idx = top(gl)
    g_top = 1.0 / jnp.sum(jnp.exp(gl - gm), axis=-1, keepdims=True)
    first = N_GROUPS + EXPERTS_PER_GROUP * g_idx
    el = jnp.where((lane >= first) & (lane < first + EXPERTS_PER_GROUP), logits, neg)
    t1, i1 = top(el)
    t2, i2 = top(jnp.where(lane == i1, neg, el))
    e2 = jnp.exp(t2 - t1)
    w1 = g_top / (1.0 + e2)
    w2 = g_top * e2 / (1.0 + e2)
    route_ref[...] = jnp.where(lane == 0, i1 - N_GROUPS, jnp.where(lane == 1, i2 - N_GROUPS,
                               jnp.where(lane == 2, w1, jnp.where(lane == 3, w2, 0.0))))


def _merge(x2, gates, o_rwkv, o_diff, w_br, w_bd, w_out, norm_ffn, w_route, b_route, tm):
    t = x2.shape[0]
    row = lambda i: (i, 0)
    fixed = lambda i: (0, 0)
    return pl.pallas_call(
        _merge_body,
        grid=(t // tm,),
        in_specs=[
            pl.BlockSpec((tm, D_MODEL), row),
            pl.BlockSpec((tm, GATE_COLS), row),
            pl.BlockSpec((tm, RWKV_WIDTH), row),
            pl.BlockSpec((tm, DIFF_V_WIDTH), row),
            pl.BlockSpec((RWKV_WIDTH, D_MODEL), fixed),
            pl.BlockSpec((DIFF_V_WIDTH, D_MODEL), fixed),
            pl.BlockSpec((D_MODEL, D_MODEL), fixed),
            pl.BlockSpec((1, D_MODEL), fixed),
            pl.BlockSpec((D_MODEL, 2 * LANES), fixed),
            pl.BlockSpec((1, LANES), fixed),
        ],
        out_specs=[
            pl.BlockSpec((tm, D_MODEL), row),
            pl.BlockSpec((tm, D_MODEL), row),
            pl.BlockSpec((tm, LANES), row),
        ],
        out_shape=[
            jax.ShapeDtypeStruct((t, D_MODEL), F32),
            jax.ShapeDtypeStruct((t, D_MODEL), F32),
            jax.ShapeDtypeStruct((t, LANES), F32),
        ],
        compiler_params=pltpu.CompilerParams(
            dimension_semantics=("parallel",), vmem_limit_bytes=VMEM_LIMIT),
        name="merge_router",
    )(x2, gates, o_rwkv, o_diff, w_br, w_bd, w_out, norm_ffn, w_route, b_route)


def _row_gather(idx_ref, base, n_rows, src_hbm, dst, sem, max_row, wait):
    def body(r, carry):
        src_row = jnp.minimum(idx_ref[base + r], max_row)
        cp = pltpu.make_async_copy(src_hbm.at[pl.ds(src_row, 1)], dst.at[pl.ds(r, 1)], sem)
        if wait:
            cp.wait()
        else:
            cp.start()
        return carry

    lax.fori_loop(0, n_rows, body, 0, unroll=8)


def _expert_body(blk_e_ref, tok_ref, h2_hbm, wg_ref, wu_ref, wd_ref, yb_ref, xbuf, sem, *, n_tokens):
    del blk_e_ref
    i = pl.program_id(0)
    n = pl.num_programs(0)
    slot = i % 2

    def gather(block, s, wait):
        _row_gather(tok_ref, block * ROW_BLOCK, ROW_BLOCK, h2_hbm, xbuf.at[s], sem.at[s],
                    n_tokens - 1, wait)

    @pl.when(i == 0)
    def _():
        gather(0, 0, False)

    @pl.when(i + 1 < n)
    def _():
        gather(i + 1, 1 - slot, False)

    gather(i, slot, True)
    x = xbuf[slot].astype(BF16)
    gt = _dot(x, wg_ref[0])
    up = _dot(x, wu_ref[0])
    mid = (gt * jax.nn.sigmoid(gt) * up).astype(BF16)
    yb_ref[...] = _dot(mid, wd_ref[0])


def _experts(blk_e, buf_tok, h2, w_gate, w_up, w_down):
    n_blocks = blk_e.shape[0]
    n_tokens = h2.shape[0]
    wspec = lambda shape: pl.BlockSpec((1,) + shape, lambda i, be, tok: (be[i], 0, 0))
    grid_spec = pltpu.PrefetchScalarGridSpec(
        num_scalar_prefetch=2,
        grid=(n_blocks,),
        in_specs=[
            pl.BlockSpec(memory_space=pl.ANY),
            wspec((D_MODEL, EXPERT_FF)),
            wspec((D_MODEL, EXPERT_FF)),
            wspec((EXPERT_FF, D_MODEL)),
        ],
        out_specs=pl.BlockSpec((ROW_BLOCK, D_MODEL), lambda i, be, tok: (i, 0)),
        scratch_shapes=[pltpu.VMEM((2, ROW_BLOCK, D_MODEL), F32), pltpu.SemaphoreType.DMA((2,))],
    )
    return pl.pallas_call(
        functools.partial(_expert_body, n_tokens=n_tokens),
        grid_spec=grid_spec,
        out_shape=jax.ShapeDtypeStruct((n_blocks * ROW_BLOCK, D_MODEL), F32),
        compiler_params=pltpu.CompilerParams(
            dimension_semantics=("arbitrary",), vmem_limit_bytes=VMEM_LIMIT),
        name="experts",
    )(blk_e, buf_tok, h2, w_gate, w_up, w_down)


def _final_body(pos_ref, x1_ref, route_ref, p_ref, yb_hbm, np_ref, wpg_ref, wpp_ref, o_ref,
                ybuf, sem, *, tm, n_rows):
    i = pl.program_id(0)
    n = pl.num_programs(0)
    slot = i % 2

    def gather(tile, s, wait):
        _row_gather(pos_ref, tile * 2 * tm, 2 * tm, yb_hbm, ybuf.at[s], sem.at[s], n_rows - 1, wait)

    @pl.when(i == 0)
    def _():
        gather(0, 0, False)

    @pl.when(i + 1 < n)
    def _():
        gather(i + 1, 1 - slot, False)

    gather(i, slot, True)
    route = route_ref[...]
    x2 = x1_ref[...] + route[:, 2:3] * ybuf[slot, :tm, :] + route[:, 3:4] * ybuf[slot, tm:, :]
    ms = jnp.mean(x2 * x2, axis=-1, keepdims=True)
    hn = (x2 * lax.rsqrt(ms + RMS_EPS) * np_ref[...]).astype(BF16)
    gate = jax.nn.sigmoid(_dot(hn, wpg_ref[...]))
    o_ref[...] = x2 + gate * _dot(p_ref[...].astype(BF16), wpp_ref[...])


def _final(pos, x1, route, p2, yb, norm_ple, w_pg, w_pp, tm):
    t = x1.shape[0]
    row = lambda i, pos: (i, 0)
    fixed = lambda i, pos: (0, 0)
    grid_spec = pltpu.PrefetchScalarGridSpec(
        num_scalar_prefetch=1,
        grid=(t // tm,),
        in_specs=[
            pl.BlockSpec((tm, D_MODEL), row),
            pl.BlockSpec((tm, LANES), row),
            pl.BlockSpec((tm, PLE_DIM), row),
            pl.BlockSpec(memory_space=pl.ANY),
            pl.BlockSpec((1, D_MODEL), fixed),
            pl.BlockSpec((D_MODEL, D_MODEL), fixed),
            pl.BlockSpec((PLE_DIM, D_MODEL), fixed),
        ],
        out_specs=pl.BlockSpec((tm, D_MODEL), row),
        scratch_shapes=[pltpu.VMEM((2, 2 * tm, D_MODEL), F32), pltpu.SemaphoreType.DMA((2,))],
    )
    return pl.pallas_call(
        functools.partial(_final_body, tm=tm, n_rows=yb.shape[0]),
        grid_spec=grid_spec,
        out_shape=jax.ShapeDtypeStruct((t, D_MODEL), F32),
        compiler_params=pltpu.CompilerParams(
            dimension_semantics=("arbitrary",), vmem_limit_bytes=VMEM_LIMIT),
        name="combine_ple",
    )(pos, x1, route, p2, yb, norm_ple, w_pg, w_pp)


def _dispatch_plan(route, n_tokens):
    eid = route[:, :TOP_K].astype(jnp.int32).reshape(-1)
    n_assign = n_tokens * TOP_K
    tid = jnp.arange(n_assign, dtype=jnp.int32) // TOP_K
    order = jnp.argsort(eid)
    se = eid[order]
    counts = jnp.bincount(eid, length=N_EXPERTS).astype(jnp.int32)
    starts = jnp.cumsum(counts) - counts
    pcounts = ((counts + ROW_BLOCK - 1) // ROW_BLOCK) * ROW_BLOCK
    pends = jnp.cumsum(pcounts)
    pstarts = pends - pcounts
    dest = pstarts[se] + (jnp.arange(n_assign, dtype=jnp.int32) - starts[se])
    rows = n_assign + N_EXPERTS * ROW_BLOCK
    n_blocks = rows // ROW_BLOCK
    buf_tok = jnp.full((rows,), n_tokens, jnp.int32).at[dest].set(tid[order])
    blk_e = jnp.minimum(jnp.searchsorted(pends, jnp.arange(n_blocks, dtype=jnp.int32) * ROW_BLOCK,
                                         side='right'), N_EXPERTS - 1).astype(jnp.int32)
    pos = jnp.zeros((n_assign,), jnp.int32).at[order].set(dest)
    return buf_tok, blk_e, pos.reshape(n_tokens, TOP_K)


def _rope_tables(seq):
    inv = ROPE_THETA ** (-jnp.arange(0, HEAD_DIM, 2, dtype=F32) / HEAD_DIM)
    ang = jnp.arange(seq, dtype=F32)[:, None] * inv[None, :]
    ang = jnp.concatenate([ang, ang], axis=-1)
    reps = DIFF_QK_WIDTH // HEAD_DIM
    return jnp.tile(jnp.cos(ang), (1, reps)), jnp.tile(jnp.sin(ang), (1, reps))


def _layer(x2, p2, batch, seq, lam_init, prm):
    t = x2.shape[0]
    tm = 256
    cos, sin = _rope_tables(seq)
    reps = DIFF_QK_WIDTH // HEAD_DIM
    prw, q, k, v, gates = _inproj(
        x2, prm['norm_mix'][None], prm['w_in'].astype(BF16), cos, sin,
        jnp.tile(prm['q_norm'], reps)[None], jnp.tile(prm['k_norm'], reps)[None], seq, tm)

    zpad = jnp.zeros((DECAY_LORA, RWKV_WIDTH), F32)
    w2p = jnp.concatenate([prm['rwkv_w2'], zpad], axis=0).astype(BF16)
    a2p = jnp.concatenate([zpad, prm['rwkv_a2']], axis=0).astype(BF16)
    o_rwkv = _rwkv(
        prw.reshape(batch, seq, RWKV_COLS), prm['rwkv_mu'][None], prm['rwkv_w0'][None], w2p,
        prm['rwkv_a0'][None], a2p, prm['rwkv_g2'].astype(BF16), prm['rwkv_k_k'][None],
        prm['rwkv_k_a'][None], prm['rwkv_r_k'].reshape(1, RWKV_WIDTH), prm['rwkv_ln_w'][None],
        prm['rwkv_ln_b'][None], min(512, seq)).reshape(t, RWKV_WIDTH)

    o_diff = _diff_attention(
        q, k, v, prm['lambda_q1'][None], prm['lambda_k1'][None], prm['lambda_q2'][None],
        prm['lambda_k2'][None], prm['subln_w'][None], batch, seq, lam_init, min(256, seq))

    w_r = jnp.concatenate([prm['w_group'], prm['w_expert_router']], axis=1)
    w_r = jnp.pad(w_r, ((0, 0), (0, LANES - ROUTE_COLS)))
    w_r_hi = w_r.astype(BF16)
    w_r_lo = (w_r - w_r_hi.astype(F32)).astype(BF16)
    b_r = jnp.pad(jnp.concatenate([prm['b_group'], prm['b_expert_router']]), (0, LANES - ROUTE_COLS))[None]
    x1, h2, route = _merge(
        x2, gates, o_rwkv, o_diff, prm['w_branch_rwkv'].astype(BF16), prm['w_branch_diff'].astype(BF16),
        prm['w_out'].astype(BF16), prm['norm_ffn'][None], jnp.concatenate([w_r_hi, w_r_lo], axis=1), b_r, tm)

    buf_tok, blk_e, pos = _dispatch_plan(route, t)
    yb = _experts(blk_e, buf_tok, h2, prm['w_gate'].astype(BF16), prm['w_up'].astype(BF16),
                  prm['w_down'].astype(BF16))
    pos_tiles = pos.reshape(t // tm, tm, TOP_K).transpose(0, 2, 1).reshape(-1)
    return _final(pos_tiles, x1, route, p2, yb, prm['norm_ple'][None], prm['w_ple_gate'].astype(BF16),
                  prm['w_ple_proj'].astype(BF16), tm)


_PARAM_NAMES = (
    'norm_mix', 'w_in', 'rwkv_mu', 'rwkv_w0', 'rwkv_w2', 'rwkv_a0', 'rwkv_a2', 'rwkv_g2', 'rwkv_k_k',
    'rwkv_k_a', 'rwkv_r_k', 'rwkv_ln_w', 'rwkv_ln_b', 'q_norm', 'k_norm', 'lambda_q1', 'lambda_k1',
    'lambda_q2', 'lambda_k2', 'subln_w', 'w_branch_rwkv', 'w_branch_diff', 'w_out', 'norm_ffn',
    'w_group', 'b_group', 'w_expert_router', 'b_expert_router', 'w_gate', 'w_up', 'w_down', 'norm_ple',
    'w_ple_gate', 'w_ple_proj')


def kernel(x, p, norm_mix, w_in, rwkv_mu, rwkv_w0, rwkv_w2, rwkv_a0, rwkv_a2, rwkv_g2, rwkv_k_k,
           rwkv_k_a, rwkv_r_k, rwkv_ln_w, rwkv_ln_b, q_norm, k_norm, lambda_q1, lambda_k1, lambda_q2,
           lambda_k2, subln_w, w_branch_rwkv, w_branch_diff, w_out, norm_ffn, w_group, b_group,
           w_expert_router, b_expert_router, w_gate, w_up, w_down, norm_ple, w_ple_gate, w_ple_proj):
    stacked = dict(zip(_PARAM_NAMES, (
        norm_mix, w_in, rwkv_mu, rwkv_w0, rwkv_w2, rwkv_a0, rwkv_a2, rwkv_g2, rwkv_k_k, rwkv_k_a,
        rwkv_r_k, rwkv_ln_w, rwkv_ln_b, q_norm, k_norm, lambda_q1, lambda_k1, lambda_q2, lambda_k2,
        subln_w, w_branch_rwkv, w_branch_diff, w_out, norm_ffn, w_group, b_group, w_expert_router,
        b_expert_router, w_gate, w_up, w_down, norm_ple, w_ple_gate, w_ple_proj)))
    batch, seq, _ = x.shape
    depth = p.shape[0]
    x2 = x.reshape(batch * seq, D_MODEL)
    for layer in range(depth):
        lam_init = 0.8 - 0.6 * math.exp(-0.3 * layer)
        prm = {name: val[layer] for name, val in stacked.items()}
        x2 = _layer(x2, p[layer].reshape(batch * seq, PLE_DIM), batch, seq, lam_init, prm)
    return x2.reshape(batch, seq, D_MODEL)
```

```python
import functools
import math

import jax
import jax.numpy as jnp
from jax import lax
from jax.experimental import pallas as pl
from jax.experimental.pallas import tpu as pltpu

F32 = jnp.float32
BF16 = jnp.bfloat16

D_MODEL = 1024
PLE_DIM = 256
RMS_EPS = 1e-6

RWKV_HEADS = 8
HEAD_DIM = 64
RWKV_WIDTH = RWKV_HEADS * HEAD_DIM
DECAY_LORA = 64
AAA_LORA = 64
GATE_LORA = 128
GN_EPS = 64e-5
RWKV_COLS = 3 * RWKV_WIDTH + DECAY_LORA + AAA_LORA + GATE_LORA

DIFF_HEADS = 4
DIFF_QK_WIDTH = DIFF_HEADS * 2 * HEAD_DIM
DIFF_V_DIM = 2 * HEAD_DIM
DIFF_V_WIDTH = DIFF_HEADS * DIFF_V_DIM
DIFF_COLS = 2 * DIFF_QK_WIDTH + DIFF_V_WIDTH
ROPE_THETA = 10000.0
GATE_COLS = 2 * D_MODEL

N_GROUPS = 4
EXPERTS_PER_GROUP = 8
N_EXPERTS = N_GROUPS * EXPERTS_PER_GROUP
TOP_K = 2
EXPERT_FF = 512
ROW_BLOCK = 128

LANES = 128
CHUNK = 64
VMEM_LIMIT = 56 * 1024 * 1024


def _dot(a, b):
    return jnp.dot(a, b, preferred_element_type=F32)


def _dot_nt(a, b):
    return lax.dot_general(a, b, (((1,), (1,)), ((), ())), preferred_element_type=F32)


def _dot_hi(a, b):
    return jnp.dot(a.astype(BF16), b.astype(BF16), preferred_element_type=F32)


def _group_ones(width, group):
    i = jnp.arange(width) // group
    return (i[:, None] == i[None, :]).astype(BF16)


def _rotate_half(t):
    width = t.shape[-1]
    lane = lax.broadcasted_iota(jnp.int32, t.shape, 1)
    fwd = pltpu.roll(t, width - HEAD_DIM // 2, 1)
    bwd = pltpu.roll(t, HEAD_DIM // 2, 1)
    return jnp.where(lane % HEAD_DIM < HEAD_DIM // 2, -fwd, bwd)


def _inproj_body(x_ref, g_ref, w_ref, cos_ref, sin_ref, qg_ref, kg_ref, ones_ref,
                 prw_ref, q_ref, k_ref, v_ref, gate_ref):
    x = x_ref[...]
    ms = jnp.mean(x * x, axis=-1, keepdims=True)
    h = (x * lax.rsqrt(ms + RMS_EPS) * g_ref[...]).astype(BF16)
    prw_ref[...] = _dot(h, w_ref[:, :RWKV_COLS])
    pd = _dot(h, w_ref[:, RWKV_COLS:RWKV_COLS + DIFF_COLS])
    cos = cos_ref[...]
    sin = sin_ref[...]

    def qk_prep(t, gain, scale):
        ssq = _dot((t * t).astype(BF16), ones_ref[...])
        t = t * lax.rsqrt(ssq * (1.0 / HEAD_DIM) + RMS_EPS) * gain
        return ((t * cos + _rotate_half(t) * sin) * scale).astype(BF16)

    q_ref[...] = qk_prep(pd[:, :DIFF_QK_WIDTH], qg_ref[...], HEAD_DIM ** -0.5)
    k_ref[...] = qk_prep(pd[:, DIFF_QK_WIDTH:2 * DIFF_QK_WIDTH], kg_ref[...], 1.0)
    v_ref[...] = pd[:, 2 * DIFF_QK_WIDTH:].astype(BF16)
    gate_ref[...] = jax.nn.sigmoid(_dot(h, w_ref[:, RWKV_COLS + DIFF_COLS:])).astype(BF16)


def _inproj(x2, norm_g, w_in, cos, sin, q_gain, k_gain, seq, tm):
    t = x2.shape[0]
    n_seq_tiles = seq // tm
    row = lambda i: (i, 0)
    fixed = lambda i: (0, 0)
    rope = lambda i: (i % n_seq_tiles, 0)
    return pl.pallas_call(
        _inproj_body,
        grid=(t // tm,),
        in_specs=[
            pl.BlockSpec((tm, D_MODEL), row),
            pl.BlockSpec((1, D_MODEL), fixed),
            pl.BlockSpec((D_MODEL, RWKV_COLS + DIFF_COLS + GATE_COLS), fixed),
            pl.BlockSpec((tm, DIFF_QK_WIDTH), rope),
            pl.BlockSpec((tm, DIFF_QK_WIDTH), rope),
            pl.BlockSpec((1, DIFF_QK_WIDTH), fixed),
            pl.BlockSpec((1, DIFF_QK_WIDTH), fixed),
            pl.BlockSpec((DIFF_QK_WIDTH, DIFF_QK_WIDTH), fixed),
        ],
        out_specs=[
            pl.BlockSpec((tm, RWKV_COLS), row),
            pl.BlockSpec((tm, DIFF_QK_WIDTH), row),
            pl.BlockSpec((tm, DIFF_QK_WIDTH), row),
            pl.BlockSpec((tm, DIFF_V_WIDTH), row),
            pl.BlockSpec((tm, GATE_COLS), row),
        ],
        out_shape=[
            jax.ShapeDtypeStruct((t, RWKV_COLS), F32),
            jax.ShapeDtypeStruct((t, DIFF_QK_WIDTH), BF16),
            jax.ShapeDtypeStruct((t, DIFF_QK_WIDTH), BF16),
            jax.ShapeDtypeStruct((t, DIFF_V_WIDTH), BF16),
            jax.ShapeDtypeStruct((t, GATE_COLS), BF16),
        ],
        compiler_params=pltpu.CompilerParams(
            dimension_semantics=("parallel",), vmem_limit_bytes=VMEM_LIMIT),
        name="inproj",
    )(x2, norm_g, w_in, cos, sin, q_gain, k_gain, _group_ones(DIFF_QK_WIDTH, HEAD_DIM))


def _attn_body(q_ref, k_ref, v_ref, lq1_ref, lk1_ref, lq2_ref, lk2_ref, sw_ref, o_ref,
               qs_sc, m_sc, l_sc, acc_sc, s0_sc, s1_sc, *, tq, wide, lam_init):
    qi = pl.program_id(2)
    q = q_ref[...]
    lane = lax.broadcasted_iota(jnp.int32, q.shape, 1)
    qs_sc[:tq, :] = jnp.where(lane < HEAD_DIM, q, jnp.zeros_like(q))
    qs_sc[tq:, :] = jnp.where(lane >= HEAD_DIM, q, jnp.zeros_like(q))
    m_sc[...] = jnp.full(m_sc.shape, -jnp.inf, F32)
    l_sc[...] = jnp.zeros(l_sc.shape, F32)
    acc_sc[...] = jnp.zeros(acc_sc.shape, F32)

    width = wide * tq

    def scores(j, s_sc):
        kb = k_ref[pl.ds(pl.multiple_of(j * width, width), width), :]
        s_sc[...] = _dot_nt(qs_sc[...], kb)

    def update(j, s_sc, masked):
        s = s_sc[...]
        if masked:
            row = lax.broadcasted_iota(jnp.int32, s.shape, 0) % tq + qi * tq
            col = lax.broadcasted_iota(jnp.int32, s.shape, 1) + j * width
            s = jnp.where(col <= row, s, -jnp.inf)
        vb = v_ref[pl.ds(pl.multiple_of(j * width, width), width), :]
        m_prev = m_sc[...]
        m_new = jnp.maximum(m_prev, jnp.max(s, axis=-1, keepdims=True))
        alpha = jnp.exp(m_prev - m_new)
        p = jnp.exp(s - m_new[:, :1])
        l_sc[...] = alpha * l_sc[...] + jnp.sum(p, axis=-1, keepdims=True)
        acc_sc[...] = alpha * acc_sc[...] + _dot(p.astype(BF16), vb)
        m_sc[...] = m_new

    last = qi // wide
    pairs = last // 2
    scores(0, s0_sc)

    def body(jj, carry):
        scores(2 * jj + 1, s1_sc)
        update(2 * jj, s0_sc, False)
        scores(2 * jj + 2, s0_sc)
        update(2 * jj + 1, s1_sc, False)
        return carry

    lax.fori_loop(0, pairs, body, 0)

    @pl.when(last % 2 == 1)
    def _():
        scores(last, s1_sc)
        update(last - 1, s0_sc, False)
        update(last, s1_sc, True)

    @pl.when(last % 2 == 0)
    def _():
        update(last, s0_sc, True)

    lam = (jnp.exp(jnp.sum(lq1_ref[...] * lk1_ref[...], axis=-1, keepdims=True))
           - jnp.exp(jnp.sum(lq2_ref[...] * lk2_ref[...], axis=-1, keepdims=True)) + lam_init)
    o = acc_sc[...] / l_sc[...]
    o = o[:tq] - lam * o[tq:]
    ms = jnp.mean(o * o, axis=-1, keepdims=True)
    o_ref[...] = o * lax.rsqrt(ms + RMS_EPS) * sw_ref[...] * (1.0 - lam_init)


def _diff_attention(q, k, v, lq1, lk1, lq2, lk2, subln_w, batch, seq, lam_init, tq):
    t = q.shape[0]
    nq = seq // tq
    vec = pl.BlockSpec((1, HEAD_DIM), lambda b, h, i: (0, 0))
    wide = max(1, min(4, nq // 2))
    assert nq % wide == 0
    return pl.pallas_call(
        functools.partial(_attn_body, tq=tq, wide=wide, lam_init=lam_init),
        grid=(batch, DIFF_HEADS, nq),
        in_specs=[
            pl.BlockSpec((tq, LANES), lambda b, h, i: (b * nq + i, h)),
            pl.BlockSpec((seq, LANES), lambda b, h, i: (b, h)),
            pl.BlockSpec((seq, LANES), lambda b, h, i: (b, h)),
            vec, vec, vec, vec,
            pl.BlockSpec((1, DIFF_V_DIM), lambda b, h, i: (0, 0)),
        ],
        out_specs=pl.BlockSpec((tq, LANES), lambda b, h, i: (b * nq + i, h)),
        out_shape=jax.ShapeDtypeStruct((t, DIFF_V_WIDTH), F32),
        scratch_shapes=[
            pltpu.VMEM((2 * tq, LANES), BF16),
            pltpu.VMEM((2 * tq, LANES), F32),
            pltpu.VMEM((2 * tq, LANES), F32),
            pltpu.VMEM((2 * tq, LANES), F32),
            pltpu.VMEM((2 * tq, wide * tq), F32),
            pltpu.VMEM((2 * tq, wide * tq), F32),
        ],
        compiler_params=pltpu.CompilerParams(
            dimension_semantics=("parallel", "parallel", "arbitrary"),
            vmem_limit_bytes=VMEM_LIMIT),
        name="diff_attn",
    )(q, k, v, lq1, lk1, lq2, lk2, subln_w)


def _dot_tn(a, b):
    return lax.dot_general(a, b, (((0,), (0,)), ((), ())), preferred_element_type=F32)


def _wkv_prep(problems):
    c = CHUNK
    lane = lax.broadcasted_iota(jnp.int32, (c, LANES), 1)
    m0 = lane < HEAD_DIM
    row = lax.broadcasted_iota(jnp.int32, (LANES, LANES), 0)
    col = lax.broadcasted_iota(jnp.int32, (LANES, LANES), 1)
    same = (row // c) == (col // c)
    strict = same & ((col % c) < (row % c))
    incl = same & ((col % c) <= (row % c))
    eye = row == col

    def stack(t):
        zero = jnp.zeros_like(t)
        return jnp.concatenate([jnp.where(m0, t, zero), jnp.where(m0, zero, t)], axis=0)

    def dup(t):
        return jnp.concatenate([t, t], axis=0)

    def fold(t):
        return t[:c] + t[c:]

    n = len(problems)
    xs = [_dot_nt(jnp.concatenate([q['kkt'], q['rt']], axis=0),
                  jnp.concatenate([stack(q['bt']), stack(q['kt'])], axis=0)) for q in problems]
    l_pow = [jnp.where(strict, dup(x[:c, :LANES]), 0.0) for x in xs]
    m_ak = [jnp.where(strict, dup(x[:c, LANES:]), 0.0).astype(BF16) for x in xs]
    a_rbk = [jnp.concatenate([jnp.where(incl, dup(x[c:, :LANES]), 0.0),
                              jnp.where(incl, dup(x[c:, LANES:]), 0.0)], axis=1).astype(BF16) for x in xs]
    t_inv = [jnp.where(eye, 1.0, 0.0) - l for l in l_pow]
    for _ in range(5):
        l16 = [l.astype(BF16) for l in l_pow]
        l_pow = [_dot(l, l) for l in l16]
        upd = [_dot(t.astype(BF16), l.astype(BF16)) for t, l in zip(t_inv, l_pow)]
        t_inv = [t + u for t, u in zip(t_inv, upd)]

    vs = [stack(q['v']) for q in problems]
    mv = [_dot(m, v) for m, v in zip(m_ak, vs)]
    z = [_dot(t.astype(BF16), jnp.concatenate([stack(q['kkt']), m.astype(BF16)], axis=1))
         for t, q, m in zip(t_inv, problems, mv)]
    zero_sq = jnp.zeros((LANES, LANES), BF16)
    qy = [_dot(a, jnp.concatenate([(-zz).astype(BF16), jnp.concatenate([zero_sq, v], axis=1)], axis=0))
          for a, zz, v in zip(a_rbk, z, vs)]
    zero_c = jnp.zeros((c, LANES), BF16)
    pd = [_dot_tn(jnp.concatenate([q['bh'], q['kh']], axis=0),
                  jnp.concatenate([jnp.concatenate([fold(-zz[:, :LANES]), fold(-zz[:, LANES:])], axis=1).astype(BF16),
                                   jnp.concatenate([zero_c, q['v']], axis=1)], axis=0))
          for q, zz in zip(problems, z)]
    out = []
    for i in range(n):
        q = problems[i]
        qs = stack(q['rt']).astype(F32) + qy[i][:, :LANES]
        phi = (jnp.where(eye, jnp.broadcast_to(q['pc'], (LANES, LANES)), 0.0)
               + jnp.where(same, pd[i][:, :LANES], 0.0))
        delta = jnp.where(same, pd[i][:, LANES:], 0.0)
        qphi = jnp.concatenate([fold(qs), phi], axis=0).astype(BF16)
        out.append((qphi, fold(qy[i][:, LANES:]), delta))
    return out


def _rwkv_body(pr_ref, halo_ref, mu_ref, w0_ref, w2_ref, a0_ref, a2_ref, g2_ref, kk_ref, ka_ref,
               rk_ref, lnw_ref, lnb_ref, ones_ref, tri_ref, o_ref,
               hp_sc, rt_sc, kt_sc, bt_sc, kkt_sc, kh_sc, bh_sc, v_sc, pc_sc, qphi_sc, delta_sc, y_sc,
               *, rows):
    i = pl.program_id(1)
    n_pairs = RWKV_WIDTH // LANES
    n_chunks = rows // CHUNK

    @pl.when(i == 0)
    def _():
        hp_sc[...] = jnp.zeros(hp_sc.shape, F32)

    pr = pr_ref[0]
    first = jnp.where(i == 0, 0.0, halo_ref[0, 7:8, :])
    prev = jnp.concatenate([first, pr[:-1]], axis=0)
    xs = pr + (prev - pr) * mu_ref[...]
    w = RWKV_WIDTH
    r = xs[:, :w]
    k = xs[:, w:2 * w]
    v = xs[:, 2 * w:3 * w]
    x_lora = xs[:, 3 * w:3 * w + DECAY_LORA + AAA_LORA]
    xg = xs[:, 3 * w + DECAY_LORA + AAA_LORA:]
    wl = w0_ref[...] + _dot(jnp.tanh(x_lora).astype(BF16), w2_ref[...])
    sp = jnp.maximum(-wl, 0.0) + jnp.log(1.0 + jnp.exp(-jnp.abs(wl)))
    lw = -jnp.exp(-sp - 0.5)
    a = jax.nn.sigmoid(a0_ref[...] + _dot(x_lora.astype(BF16), a2_ref[...]))
    g = _dot(jax.nn.sigmoid(xg).astype(BF16), g2_ref[...])
    ones = ones_ref[...]
    kx = k * kk_ref[...]
    ssq = _dot((kx * kx).astype(BF16), ones)
    kn = kx / jnp.maximum(jnp.sqrt(ssq), 1e-12)
    km = k * (1.0 + (a - 1.0) * ka_ref[...])
    bonus = _dot((r * km * rk_ref[...]).astype(BF16), ones) * v

    hi = lw.astype(BF16)
    r1 = lw - hi.astype(F32)
    mid = r1.astype(BF16)
    lo = (r1 - mid.astype(F32)).astype(BF16)
    tri = tri_ref[...]
    cum = _dot(tri, hi) + _dot(tri, mid) + _dot(tri, lo)

    cum3 = cum.reshape(n_chunks, CHUNK, w)
    end3 = cum3[:, CHUNK - 1:CHUNK, :]
    p_inv = jnp.exp(-cum)
    p_end = jnp.exp(end3 - cum3).reshape(rows, w)
    b = kn * a
    rt_sc[...] = (r * jnp.exp(cum)).astype(BF16)
    kt_sc[...] = (km * p_inv).astype(BF16)
    bt_sc[...] = (b * p_inv).astype(BF16)
    kkt_sc[...] = (kn * jnp.exp(cum - lw)).astype(BF16)
    kh_sc[...] = (km * p_end).astype(BF16)
    bh_sc[...] = (b * p_end).astype(BF16)
    v_sc[...] = v.astype(BF16)
    pc_sc[...] = jnp.broadcast_to(jnp.exp(end3), pc_sc.shape)

    group = 2 if n_chunks % 2 == 0 else 1

    def prep_chunks(gi, carry):
        where = []
        for g in range(group):
            ci = gi * group + g
            rs = pl.ds(pl.multiple_of(ci * CHUNK, CHUNK), CHUNK)
            for p in range(n_pairs):
                where.append((ci, rs, p, slice(p * LANES, (p + 1) * LANES)))
        problems = [dict(rt=rt_sc[rs, sl], kt=kt_sc[rs, sl], bt=bt_sc[rs, sl], kkt=kkt_sc[rs, sl],
                         kh=kh_sc[rs, sl], bh=bh_sc[rs, sl], v=v_sc[rs, sl], pc=pc_sc[ci, 0:1, sl])
                    for ci, rs, p, sl in where]
        for (ci, rs, p, sl), (qphi, y0, delta) in zip(where, _wkv_prep(problems)):
            qphi_sc[ci, p] = qphi
            delta_sc[ci, p] = delta
            y_sc[rs, sl] = y0
        return carry

    lax.fori_loop(0, n_chunks // group, prep_chunks, 0)

    def scan_chunk(ci, carry):
        rs = pl.ds(pl.multiple_of(ci * CHUNK, CHUNK), CHUNK)
        for p in range(n_pairs):
            sl = slice(p * LANES, (p + 1) * LANES)
            res = _dot(qphi_sc[ci, p], hp_sc[p].astype(BF16))
            y_sc[rs, sl] += res[:CHUNK]
            hp_sc[p] = res[CHUNK:] + delta_sc[ci, p]
        return carry

    lax.fori_loop(0, n_chunks, scan_chunk, 0, unroll=True)

    y = y_sc[...]
    mean = _dot(y.astype(BF16), ones) * (1.0 / HEAD_DIM)
    yc = y - mean
    var = _dot((yc * yc).astype(BF16), ones) * (1.0 / HEAD_DIM)
    yn = yc * lax.rsqrt(var + GN_EPS) * lnw_ref[...] + lnb_ref[...]
    o_ref[0] = ((yn + bonus) * g).astype(o_ref.dtype)


def _rwkv(pr3, mu, w0, w2p, a0, a2p, g2, k_k, k_a, r_k, ln_w, ln_b, rows):
    batch, seq, _ = pr3.shape
    nb = seq // rows
    fixed = lambda b, i: (0, 0)
    vec = pl.BlockSpec((1, RWKV_WIDTH), fixed)
    lora = DECAY_LORA + AAA_LORA
    n_pairs = RWKV_WIDTH // LANES
    ci = jnp.arange(rows) // CHUNK
    ti = jnp.arange(rows)
    tri = ((ci[:, None] == ci[None, :]) & (ti[None, :] <= ti[:, None])).astype(BF16)
    big = pltpu.VMEM((rows, RWKV_WIDTH), F32)
    n_chunks = rows // CHUNK
    return pl.pallas_call(
        functools.partial(_rwkv_body, rows=rows),
        grid=(batch, nb),
        in_specs=[
            pl.BlockSpec((1, rows, RWKV_COLS), lambda b, i: (b, i, 0)),
            pl.BlockSpec((1, 8, RWKV_COLS), lambda b, i: (b, jnp.maximum(i * (rows // 8) - 1, 0), 0)),
            pl.BlockSpec((1, RWKV_COLS), fixed),
            vec,
            pl.BlockSpec((lora, RWKV_WIDTH), fixed),
            vec,
            pl.BlockSpec((lora, RWKV_WIDTH), fixed),
            pl.BlockSpec((GATE_LORA, RWKV_WIDTH), fixed),
            vec, vec, vec, vec, vec,
            pl.BlockSpec((RWKV_WIDTH, RWKV_WIDTH), fixed),
            pl.BlockSpec((rows, rows), fixed),
        ],
        out_specs=pl.BlockSpec((1, rows, RWKV_WIDTH), lambda b, i: (b, i, 0)),
        out_shape=jax.ShapeDtypeStruct((batch, seq, RWKV_WIDTH), F32),
        scratch_shapes=(
            [pltpu.VMEM((n_pairs, LANES, LANES), F32)]
            + [pltpu.VMEM((rows, RWKV_WIDTH), BF16)] * 7
            + [pltpu.VMEM((n_chunks, 8, RWKV_WIDTH), F32),
               pltpu.VMEM((n_chunks, n_pairs, CHUNK + LANES, LANES), BF16),
               pltpu.VMEM((n_chunks, n_pairs, LANES, LANES), F32),
               big]),
        compiler_params=pltpu.CompilerParams(
            dimension_semantics=("parallel", "arbitrary"), vmem_limit_bytes=VMEM_LIMIT),
        name="rwkv7",
    )(pr3, pr3, mu, w0, w2p, a0, a2p, g2, k_k, k_a, r_k, ln_w, ln_b,
      _group_ones(RWKV_WIDTH, HEAD_DIM), tri)


ROUTE_COLS = N_GROUPS + N_EXPERTS


def _merge_body(x_ref, gate_ref, orw_ref, odf_ref, wbr_ref, wbd_ref, wout_ref, nf_ref, wr_ref,
                br_ref, x1_ref, h2_ref, route_ref):
    gates = gate_ref[...].astype(F32)
    mixed = (gates[:, :D_MODEL] * _dot(orw_ref[...].astype(BF16), wbr_ref[...])
             + gates[:, D_MODEL:] * _dot(odf_ref[...].astype(BF16), wbd_ref[...]))
    x1 = x_ref[...] + _dot(mixed.astype(BF16), wout_ref[...])
    x1_ref[...] = x1
    ms = jnp.mean(x1 * x1, axis=-1, keepdims=True)
    h2 = x1 * lax.rsqrt(ms + RMS_EPS) * nf_ref[...]
    h2_ref[...] = h2

    hi = h2.astype(BF16)
    lo = (h2 - hi.astype(F32)).astype(BF16)
    acc = _dot(hi, wr_ref[...])
    logits = acc[:, :LANES] + acc[:, LANES:] + _dot(lo, wr_ref[:, :LANES]) + br_ref[...]
    lane = lax.broadcasted_iota(jnp.int32, logits.shape, 1).astype(F32)
    neg = -jnp.inf

    def top(vals):
        m = jnp.max(vals, axis=-1, keepdims=True)
        return m, jnp.min(jnp.where(vals == m, lane, float(LANES)), axis=-1, keepdims=True)

    gl = jnp.where(lane < N_GROUPS, logits, neg)
    gm, g_idx = top(gl)
    g_top = 1.0 / jnp.sum(jnp.exp(gl - gm), axis=-1, keepdims=True)
    first = N_GROUPS + EXPERTS_PER_GROUP * g_idx
    el = jnp.where((lane >= first) & (lane < first + EXPERTS_PER_GROUP), logits, neg)
    t1, i1 = top(el)
    t2, i2 = top(jnp.where(lane == i1, neg, el))
    e2 = jnp.exp(t2 - t1)
    w1 = g_top / (1.0 + e2)
    w2 = g_top * e2 / (1.0 + e2)
    route_ref[...] = jnp.where(lane == 0, i1 - N_GROUPS, jnp.where(lane == 1, i2 - N_GROUPS,
                               jnp.where(lane == 2, w1, jnp.where(lane == 3, w2, 0.0))))


def _merge(x2, gates, o_rwkv, o_diff, w_br, w_bd, w_out, norm_ffn, w_route, b_route, tm):
    t = x2.shape[0]
    row = lambda i: (i, 0)
    fixed = lambda i: (0, 0)
    return pl.pallas_call(
        _merge_body,
        grid=(t // tm,),
        in_specs=[
            pl.BlockSpec((tm, D_MODEL), row),
            pl.BlockSpec((tm, GATE_COLS), row),
            pl.BlockSpec((tm, RWKV_WIDTH), row),
            pl.BlockSpec((tm, DIFF_V_WIDTH), row),
            pl.BlockSpec((RWKV_WIDTH, D_MODEL), fixed),
            pl.BlockSpec((DIFF_V_WIDTH, D_MODEL), fixed),
            pl.BlockSpec((D_MODEL, D_MODEL), fixed),
            pl.BlockSpec((1, D_MODEL), fixed),
            pl.BlockSpec((D_MODEL, 2 * LANES), fixed),
            pl.BlockSpec((1, LANES), fixed),
        ],
        out_specs=[
            pl.BlockSpec((tm, D_MODEL), row),
            pl.BlockSpec((tm, D_MODEL), row),
            pl.BlockSpec((tm, LANES), row),
        ],
        out_shape=[
            jax.ShapeDtypeStruct((t, D_MODEL), F32),
            jax.ShapeDtypeStruct((t, D_MODEL), F32),
            jax.ShapeDtypeStruct((t, LANES), F32),
        ],
        compiler_params=pltpu.CompilerParams(
            dimension_semantics=("parallel",), vmem_limit_bytes=VMEM_LIMIT),
        name="merge_router",
    )(x2, gates, o_rwkv, o_diff, w_br, w_bd, w_out, norm_ffn, w_route, b_route)


def _wait_rows(src, dst, sem, n_rows):
    for _ in range(n_rows):
        pltpu.make_async_copy(src.at[pl.ds(0, 1)], dst.at[pl.ds(0, 1)], sem).wait()


def _expert_body(blk_e_ref, rows_ref, h2_hbm, wg_ref, wu_ref, wd_ref, y2_hbm,
                 xbuf, obuf, gsem, ssem, *, n_assign, n_blocks):
    del blk_e_ref
    i = pl.program_id(0)
    slot = i % 2
    last_token = n_assign // TOP_K - 1

    def start_gather(block, s):
        for r in range(ROW_BLOCK):
            a = rows_ref[block * ROW_BLOCK + r]
            tok = jnp.minimum(lax.shift_right_logical(a, 1), last_token)
            pltpu.make_async_copy(h2_hbm.at[pl.ds(tok, 1)], xbuf.at[s, pl.ds(r, 1)], gsem.at[s]).start()

    def start_scatter(block, s):
        for r in range(ROW_BLOCK):
            a = rows_ref[block * ROW_BLOCK + r]
            dst = y2_hbm.at[a & 1, pl.ds(lax.shift_right_logical(a, 1), 1)]
            pltpu.make_async_copy(obuf.at[s, pl.ds(r, 1)], dst, ssem.at[s]).start()

    def wait_gather(s):
        _wait_rows(h2_hbm, xbuf.at[s], gsem.at[s], ROW_BLOCK)

    def wait_scatter(s):
        _wait_rows(obuf.at[s], y2_hbm.at[0], ssem.at[s], ROW_BLOCK)

    @pl.when(i == 0)
    def _():
        start_gather(0, 0)
        obuf[...] = jnp.zeros(obuf.shape, F32)
        for j in range(TOP_K):
            spare = pltpu.make_async_copy(
                obuf.at[j], y2_hbm.at[j, pl.ds(n_assign // TOP_K, ROW_BLOCK)], ssem.at[j])
            spare.start()
            spare.wait()

    @pl.when(i >= 2)
    def _():
        wait_scatter(slot)

    wait_gather(slot)
    start_gather(i + 1, 1 - slot)
    x = xbuf[slot].astype(BF16)
    gt = _dot(x, wg_ref[0])
    up = _dot(x, wu_ref[0])
    mid = (gt * jax.nn.sigmoid(gt) * up).astype(BF16)
    obuf[slot] = _dot(mid, wd_ref[0])
    start_scatter(i, slot)

    @pl.when(i == n_blocks - 1)
    def _():
        wait_gather(1 - slot)
        if n_blocks >= 2:
            wait_scatter(1 - slot)
        wait_scatter(slot)


def _experts(blk_e, rows, h2, w_gate, w_up, w_down):
    n_blocks = blk_e.shape[0]
    n_assign = h2.shape[0] * TOP_K
    wspec = lambda shape: pl.BlockSpec((1,) + shape, lambda i, be, *_: (be[i], 0, 0))
    grid_spec = pltpu.PrefetchScalarGridSpec(
        num_scalar_prefetch=2,
        grid=(n_blocks,),
        in_specs=[
            pl.BlockSpec(memory_space=pl.ANY),
            wspec((D_MODEL, EXPERT_FF)),
            wspec((D_MODEL, EXPERT_FF)),
            wspec((EXPERT_FF, D_MODEL)),
        ],
        out_specs=pl.BlockSpec(memory_space=pl.ANY),
        scratch_shapes=[pltpu.VMEM((2, ROW_BLOCK, D_MODEL), F32), pltpu.VMEM((2, ROW_BLOCK, D_MODEL), F32),
                        pltpu.SemaphoreType.DMA((2,)), pltpu.SemaphoreType.DMA((2,))],
    )
    return pl.pallas_call(
        functools.partial(_expert_body, n_assign=n_assign, n_blocks=n_blocks),
        grid_spec=grid_spec,
        out_shape=jax.ShapeDtypeStruct((TOP_K, n_assign // TOP_K + ROW_BLOCK, D_MODEL), F32),
        compiler_params=pltpu.CompilerParams(
            dimension_semantics=("arbitrary",), vmem_limit_bytes=VMEM_LIMIT),
        name="experts",
    )(blk_e, rows, h2, w_gate, w_up, w_down)


def _final_body(x1_ref, route_ref, p_ref, y2_ref, np_ref, wpg_ref, wpp_ref, o_ref):
    route = route_ref[...]
    x2 = x1_ref[...] + route[:, 2:3] * y2_ref[0] + route[:, 3:4] * y2_ref[1]
    ms = jnp.mean(x2 * x2, axis=-1, keepdims=True)
    hn = (x2 * lax.rsqrt(ms + RMS_EPS) * np_ref[...]).astype(BF16)
    gate = jax.nn.sigmoid(_dot(hn, wpg_ref[...]))
    o_ref[...] = x2 + gate * _dot(p_ref[...].astype(BF16), wpp_ref[...])


def _final(x1, route, p2, y2, norm_ple, w_pg, w_pp, tm):
    t = x1.shape[0]
    row = lambda i: (i, 0)
    fixed = lambda i: (0, 0)
    return pl.pallas_call(
        _final_body,
        grid=(t // tm,),
        in_specs=[
            pl.BlockSpec((tm, D_MODEL), row),
            pl.BlockSpec((tm, LANES), row),
            pl.BlockSpec((tm, PLE_DIM), row),
            pl.BlockSpec((TOP_K, tm, D_MODEL), lambda i: (0, i, 0)),
            pl.BlockSpec((1, D_MODEL), fixed),
            pl.BlockSpec((D_MODEL, D_MODEL), fixed),
            pl.BlockSpec((PLE_DIM, D_MODEL), fixed),
        ],
        out_specs=pl.BlockSpec((tm, D_MODEL), row),
        out_shape=jax.ShapeDtypeStruct((t, D_MODEL), F32),
        compiler_params=pltpu.CompilerParams(
            dimension_semantics=("parallel",), vmem_limit_bytes=VMEM_LIMIT),
        name="combine_ple",
    )(x1, route, p2, y2, norm_ple, w_pg, w_pp)


def _dispatch_plan(route, n_tokens):
    eid = route[:, :TOP_K].astype(jnp.int32).reshape(-1)
    n_assign = n_tokens * TOP_K
    order = jnp.argsort(eid).astype(jnp.int32)
    experts = jnp.arange(N_EXPERTS, dtype=jnp.int32)
    counts = jnp.sum((eid[:, None] == experts[None, :]).astype(jnp.int32), axis=0)
    starts = jnp.cumsum(counts) - counts
    pcounts = ((counts + ROW_BLOCK - 1) // ROW_BLOCK) * ROW_BLOCK
    pends = jnp.cumsum(pcounts)
    pstarts = pends - pcounts
    n_blocks = (n_assign + N_EXPERTS * ROW_BLOCK) // ROW_BLOCK
    row0 = jnp.arange(n_blocks, dtype=jnp.int32) * ROW_BLOCK
    blk_e = jnp.minimum(jnp.sum((pends[None, :] <= row0[:, None]).astype(jnp.int32), axis=1), N_EXPERTS - 1)
    offset = row0 - pstarts[blk_e]
    valid = counts[blk_e] - offset
    first = jnp.clip(starts[blk_e] + offset, 0, n_assign - 1)
    r = jnp.arange(ROW_BLOCK, dtype=jnp.int32)[None, :]
    idx = jnp.minimum(first[:, None] + r, n_assign - 1)
    spare = n_assign + (jnp.arange(n_blocks, dtype=jnp.int32)[:, None] % 2) * ROW_BLOCK + r
    rows = jnp.where(r < valid[:, None], order[idx], spare).reshape(-1)
    rows = jnp.concatenate([rows, n_assign + (n_blocks % 2) * ROW_BLOCK + r[0]])
    return blk_e, rows


def _rope_tables(seq):
    inv = ROPE_THETA ** (-jnp.arange(0, HEAD_DIM, 2, dtype=F32) / HEAD_DIM)
    ang = jnp.arange(seq, dtype=F32)[:, None] * inv[None, :]
    ang = jnp.concatenate([ang, ang], axis=-1)
    reps = DIFF_QK_WIDTH // HEAD_DIM
    return jnp.tile(jnp.cos(ang), (1, reps)), jnp.tile(jnp.sin(ang), (1, reps))


def _layer(x2, p2, batch, seq, lam_init, prm):
    t = x2.shape[0]
    tm = 256
    cos, sin = _rope_tables(seq)
    reps = DIFF_QK_WIDTH // HEAD_DIM
    prw, q, k, v, gates = _inproj(
        x2, prm['norm_mix'][None], prm['w_in'].astype(BF16), cos, sin,
        jnp.tile(prm['q_norm'], reps)[None], jnp.tile(prm['k_norm'], reps)[None], seq, tm)

    zpad = jnp.zeros((DECAY_LORA, RWKV_WIDTH), F32)
    w2p = jnp.concatenate([prm['rwkv_w2'], zpad], axis=0).astype(BF16)
    a2p = jnp.concatenate([zpad, prm['rwkv_a2']], axis=0).astype(BF16)
    o_rwkv = _rwkv(
        prw.reshape(batch, seq, RWKV_COLS), prm['rwkv_mu'][None], prm['rwkv_w0'][None], w2p,
        prm['rwkv_a0'][None], a2p, prm['rwkv_g2'].astype(BF16), prm['rwkv_k_k'][None],
        prm['rwkv_k_a'][None], prm['rwkv_r_k'].reshape(1, RWKV_WIDTH), prm['rwkv_ln_w'][None],
        prm['rwkv_ln_b'][None], min(512, seq)).reshape(t, RWKV_WIDTH)

    o_diff = _diff_attention(
        q, k, v, prm['lambda_q1'][None], prm['lambda_k1'][None], prm['lambda_q2'][None],
        prm['lambda_k2'][None], prm['subln_w'][None], batch, seq, lam_init, min(256, seq))

    w_r = jnp.concatenate([prm['w_group'], prm['w_expert_router']], axis=1)
    w_r = jnp.pad(w_r, ((0, 0), (0, LANES - ROUTE_COLS)))
    w_r_hi = w_r.astype(BF16)
    w_r_lo = (w_r - w_r_hi.astype(F32)).astype(BF16)
    b_r = jnp.pad(jnp.concatenate([prm['b_group'], prm['b_expert_router']]), (0, LANES - ROUTE_COLS))[None]
    x1, h2, route = _merge(
        x2, gates, o_rwkv, o_diff, prm['w_branch_rwkv'].astype(BF16), prm['w_branch_diff'].astype(BF16),
        prm['w_out'].astype(BF16), prm['norm_ffn'][None], jnp.concatenate([w_r_hi, w_r_lo], axis=1), b_r, tm)

    blk_e, rows = _dispatch_plan(route, t)
    y2 = _experts(blk_e, rows, h2, prm['w_gate'].astype(BF16), prm['w_up'].astype(BF16),
                  prm['w_down'].astype(BF16))
    return _final(x1, route, p2, y2, prm['norm_ple'][None], prm['w_ple_gate'].astype(BF16),
                  prm['w_ple_proj'].astype(BF16), tm)


_PARAM_NAMES = (
    'norm_mix', 'w_in', 'rwkv_mu', 'rwkv_w0', 'rwkv_w2', 'rwkv_a0', 'rwkv_a2', 'rwkv_g2', 'rwkv_k_k',
    'rwkv_k_a', 'rwkv_r_k', 'rwkv_ln_w', 'rwkv_ln_b', 'q_norm', 'k_norm', 'lambda_q1', 'lambda_k1',
    'lambda_q2', 'lambda_k2', 'subln_w', 'w_branch_rwkv', 'w_branch_diff', 'w_out', 'norm_ffn',
    'w_group', 'b_group', 'w_expert_router', 'b_expert_router', 'w_gate', 'w_up', 'w_down', 'norm_ple',
    'w_ple_gate', 'w_ple_proj')


def kernel(x, p, norm_mix, w_in, rwkv_mu, rwkv_w0, rwkv_w2, rwkv_a0, rwkv_a2, rwkv_g2, rwkv_k_k,
           rwkv_k_a, rwkv_r_k, rwkv_ln_w, rwkv_ln_b, q_norm, k_norm, lambda_q1, lambda_k1, lambda_q2,
           lambda_k2, subln_w, w_branch_rwkv, w_branch_diff, w_out, norm_ffn, w_group, b_group,
           w_expert_router, b_expert_router, w_gate, w_up, w_down, norm_ple, w_ple_gate, w_ple_proj):
    stacked = dict(zip(_PARAM_NAMES, (
        norm_mix, w_in, rwkv_mu, rwkv_w0, rwkv_w2, rwkv_a0, rwkv_a2, rwkv_g2, rwkv_k_k, rwkv_k_a,
        rwkv_r_k, rwkv_ln_w, rwkv_ln_b, q_norm, k_norm, lambda_q1, lambda_k1, lambda_q2, lambda_k2,
        subln_w, w_branch_rwkv, w_branch_diff, w_out, norm_ffn, w_group, b_group, w_expert_router,
        b_expert_router, w_gate, w_up, w_down, norm_ple, w_ple_gate, w_ple_proj)))
    batch, seq, _ = x.shape
    depth = p.shape[0]
    x2 = x.reshape(batch * seq, D_MODEL)
    for layer in range(depth):
        lam_init = 0.8 - 0.6 * math.exp(-0.3 * layer)
        prm = {name: val[layer] for name, val in stacked.items()}
        x2 = _layer(x2, p[layer].reshape(batch * seq, PLE_DIM), batch, seq, lam_init, prm)
    return x2.reshape(batch, seq, D_MODEL)
```

```python
import functools
import math

import jax
import jax.numpy as jnp
from jax import lax
from jax.experimental import pallas as pl
from jax.experimental.pallas import tpu as pltpu

F32 = jnp.float32
BF16 = jnp.bfloat16

D_MODEL = 1024
PLE_DIM = 256
RMS_EPS = 1e-6

RWKV_HEADS = 8
HEAD_DIM = 64
RWKV_WIDTH = RWKV_HEADS * HEAD_DIM
DECAY_LORA = 64
AAA_LORA = 64
GATE_LORA = 128
GN_EPS = 64e-5
RWKV_COLS = 3 * RWKV_WIDTH + DECAY_LORA + AAA_LORA + GATE_LORA

DIFF_HEADS = 4
DIFF_QK_WIDTH = DIFF_HEADS * 2 * HEAD_DIM
DIFF_V_DIM = 2 * HEAD_DIM
DIFF_V_WIDTH = DIFF_HEADS * DIFF_V_DIM
DIFF_COLS = 2 * DIFF_QK_WIDTH + DIFF_V_WIDTH
ROPE_THETA = 10000.0
GATE_COLS = 2 * D_MODEL

N_GROUPS = 4
EXPERTS_PER_GROUP = 8
N_EXPERTS = N_GROUPS * EXPERTS_PER_GROUP
TOP_K = 2
EXPERT_FF = 512
ROW_BLOCK = 128

LANES = 128
CHUNK = 64
VMEM_LIMIT = 56 * 1024 * 1024


def _dot(a, b):
    return jnp.dot(a, b, preferred_element_type=F32)


def _dot_nt(a, b):
    return lax.dot_general(a, b, (((1,), (1,)), ((), ())), preferred_element_type=F32)


def _dot_hi(a, b):
    return jnp.dot(a.astype(BF16), b.astype(BF16), preferred_element_type=F32)


SUBLANES = 8
TILE_ROWS = D_MODEL // LANES
assert TILE_ROWS == SUBLANES


def _store_token_tiles(ref, base, val, pitch=TILE_ROWS):
    n = val.shape[0]
    for j in range(TILE_ROWS):
        ref[pl.ds(base + j, n, stride=pitch), :] = val[:, j * LANES:(j + 1) * LANES]


def _load_token_tiles(ref, base, n, pitch=TILE_ROWS):
    return jnp.concatenate([ref[pl.ds(base + j, n, stride=pitch), :] for j in range(TILE_ROWS)], axis=1)


def _tile(ref, index):
    return ref.at[pl.ds(pl.multiple_of(index * TILE_ROWS, TILE_ROWS), TILE_ROWS)]


def _group_ones(width, group):
    i = jnp.arange(width) // group
    return (i[:, None] == i[None, :]).astype(BF16)


def _rotate_half(t):
    width = t.shape[-1]
    lane = lax.broadcasted_iota(jnp.int32, t.shape, 1)
    fwd = pltpu.roll(t, width - HEAD_DIM // 2, 1)
    bwd = pltpu.roll(t, HEAD_DIM // 2, 1)
    return jnp.where(lane % HEAD_DIM < HEAD_DIM // 2, -fwd, bwd)


def _inproj_body(x_ref, g_ref, w_ref, cos_ref, sin_ref, qg_ref, kg_ref, ones_ref,
                 prw_ref, q_ref, k_ref, v_ref, gate_ref):
    x = x_ref[...]
    ms = jnp.mean(x * x, axis=-1, keepdims=True)
    h = (x * lax.rsqrt(ms + RMS_EPS) * g_ref[...]).astype(BF16)
    prw_ref[...] = _dot(h, w_ref[:, :RWKV_COLS])
    pd = _dot(h, w_ref[:, RWKV_COLS:RWKV_COLS + DIFF_COLS])
    cos = cos_ref[...]
    sin = sin_ref[...]

    def qk_prep(t, gain, scale):
        ssq = _dot((t * t).astype(BF16), ones_ref[...])
        t = t * lax.rsqrt(ssq * (1.0 / HEAD_DIM) + RMS_EPS) * gain
        return ((t * cos + _rotate_half(t) * sin) * scale).astype(BF16)

    q_ref[...] = qk_prep(pd[:, :DIFF_QK_WIDTH], qg_ref[...], HEAD_DIM ** -0.5)
    k_ref[...] = qk_prep(pd[:, DIFF_QK_WIDTH:2 * DIFF_QK_WIDTH], kg_ref[...], 1.0)
    v_ref[...] = pd[:, 2 * DIFF_QK_WIDTH:].astype(BF16)
    gate_ref[...] = jax.nn.sigmoid(_dot(h, w_ref[:, RWKV_COLS + DIFF_COLS:])).astype(BF16)


def _inproj(x2, norm_g, w_in, cos, sin, q_gain, k_gain, seq, tm):
    t = x2.shape[0]
    n_seq_tiles = seq // tm
    row = lambda i: (i, 0)
    fixed = lambda i: (0, 0)
    rope = lambda i: (i % n_seq_tiles, 0)
    return pl.pallas_call(
        _inproj_body,
        grid=(t // tm,),
        in_specs=[
            pl.BlockSpec((tm, D_MODEL), row),
            pl.BlockSpec((1, D_MODEL), fixed),
            pl.BlockSpec((D_MODEL, RWKV_COLS + DIFF_COLS + GATE_COLS), fixed),
            pl.BlockSpec((tm, DIFF_QK_WIDTH), rope),
            pl.BlockSpec((tm, DIFF_QK_WIDTH), rope),
            pl.BlockSpec((1, DIFF_QK_WIDTH), fixed),
            pl.BlockSpec((1, DIFF_QK_WIDTH), fixed),
            pl.BlockSpec((DIFF_QK_WIDTH, DIFF_QK_WIDTH), fixed),
        ],
        out_specs=[
            pl.BlockSpec((tm, RWKV_COLS), row),
            pl.BlockSpec((tm, DIFF_QK_WIDTH), row),
            pl.BlockSpec((tm, DIFF_QK_WIDTH), row),
            pl.BlockSpec((tm, DIFF_V_WIDTH), row),
            pl.BlockSpec((tm, GATE_COLS), row),
        ],
        out_shape=[
            jax.ShapeDtypeStruct((t, RWKV_COLS), F32),
            jax.ShapeDtypeStruct((t, DIFF_QK_WIDTH), BF16),
            jax.ShapeDtypeStruct((t, DIFF_QK_WIDTH), BF16),
            jax.ShapeDtypeStruct((t, DIFF_V_WIDTH), BF16),
            jax.ShapeDtypeStruct((t, GATE_COLS), BF16),
        ],
        compiler_params=pltpu.CompilerParams(
            dimension_semantics=("parallel",), vmem_limit_bytes=VMEM_LIMIT),
        name="inproj",
    )(x2, norm_g, w_in, cos, sin, q_gain, k_gain, _group_ones(DIFF_QK_WIDTH, HEAD_DIM))


def _attn_body(q_ref, k_ref, v_ref, lq1_ref, lk1_ref, lq2_ref, lk2_ref, sw_ref, o_ref,
               qs_sc, m_sc, l_sc, acc_sc, s0_sc, s1_sc, *, tq, wide, lam_init):
    qi = pl.program_id(2)
    q = q_ref[...]
    lane = lax.broadcasted_iota(jnp.int32, q.shape, 1)
    qs_sc[:tq, :] = jnp.where(lane < HEAD_DIM, q, jnp.zeros_like(q))
    qs_sc[tq:, :] = jnp.where(lane >= HEAD_DIM, q, jnp.zeros_like(q))
    m_sc[...] = jnp.full(m_sc.shape, -jnp.inf, F32)
    l_sc[...] = jnp.zeros(l_sc.shape, F32)
    acc_sc[...] = jnp.zeros(acc_sc.shape, F32)

    width = wide * tq

    def scores(j, s_sc):
        kb = k_ref[pl.ds(pl.multiple_of(j * width, width), width), :]
        s_sc[...] = _dot_nt(qs_sc[...], kb)

    def update(j, s_sc, masked):
        s = s_sc[...]
        if masked:
            row = lax.broadcasted_iota(jnp.int32, s.shape, 0) % tq + qi * tq
            col = lax.broadcasted_iota(jnp.int32, s.shape, 1) + j * width
            s = jnp.where(col <= row, s, -jnp.inf)
        vb = v_ref[pl.ds(pl.multiple_of(j * width, width), width), :]
        m_prev = m_sc[...]
        m_new = jnp.maximum(m_prev, jnp.max(s, axis=-1, keepdims=True))
        alpha = jnp.exp(m_prev - m_new)
        p = jnp.exp(s - m_new[:, :1])
        l_sc[...] = alpha * l_sc[...] + jnp.sum(p, axis=-1, keepdims=True)
        acc_sc[...] = alpha * acc_sc[...] + _dot(p.astype(BF16), vb)
        m_sc[...] = m_new

    last = qi // wide
    pairs = last // 2
    scores(0, s0_sc)

    def body(jj, carry):
        scores(2 * jj + 1, s1_sc)
        update(2 * jj, s0_sc, False)
        scores(2 * jj + 2, s0_sc)
        update(2 * jj + 1, s1_sc, False)
        return carry

    lax.fori_loop(0, pairs, body, 0)

    @pl.when(last % 2 == 1)
    def _():
        scores(last, s1_sc)
        update(last - 1, s0_sc, False)
        update(last, s1_sc, True)

    @pl.when(last % 2 == 0)
    def _():
        update(last, s0_sc, True)

    lam = (jnp.exp(jnp.sum(lq1_ref[...] * lk1_ref[...], axis=-1, keepdims=True))
           - jnp.exp(jnp.sum(lq2_ref[...] * lk2_ref[...], axis=-1, keepdims=True)) + lam_init)
    o = acc_sc[...] / l_sc[...]
    o = o[:tq] - lam * o[tq:]
    ms = jnp.mean(o * o, axis=-1, keepdims=True)
    o_ref[...] = o * lax.rsqrt(ms + RMS_EPS) * sw_ref[...] * (1.0 - lam_init)


def _diff_attention(q, k, v, lq1, lk1, lq2, lk2, subln_w, batch, seq, lam_init, tq):
    t = q.shape[0]
    nq = seq // tq
    vec = pl.BlockSpec((1, HEAD_DIM), lambda b, h, i: (0, 0))
    wide = max(1, min(4, nq // 2))
    assert nq % wide == 0
    return pl.pallas_call(
        functools.partial(_attn_body, tq=tq, wide=wide, lam_init=lam_init),
        grid=(batch, DIFF_HEADS, nq),
        in_specs=[
            pl.BlockSpec((tq, LANES), lambda b, h, i: (b * nq + i, h)),
            pl.BlockSpec((seq, LANES), lambda b, h, i: (b, h)),
            pl.BlockSpec((seq, LANES), lambda b, h, i: (b, h)),
            vec, vec, vec, vec,
            pl.BlockSpec((1, DIFF_V_DIM), lambda b, h, i: (0, 0)),
        ],
        out_specs=pl.BlockSpec((tq, LANES), lambda b, h, i: (b * nq + i, h)),
        out_shape=jax.ShapeDtypeStruct((t, DIFF_V_WIDTH), F32),
        scratch_shapes=[
            pltpu.VMEM((2 * tq, LANES), BF16),
            pltpu.VMEM((2 * tq, LANES), F32),
            pltpu.VMEM((2 * tq, LANES), F32),
            pltpu.VMEM((2 * tq, LANES), F32),
            pltpu.VMEM((2 * tq, wide * tq), F32),
            pltpu.VMEM((2 * tq, wide * tq), F32),
        ],
        compiler_params=pltpu.CompilerParams(
            dimension_semantics=("parallel", "parallel", "arbitrary"),
            vmem_limit_bytes=VMEM_LIMIT),
        name="diff_attn",
    )(q, k, v, lq1, lk1, lq2, lk2, subln_w)


def _dot_tn(a, b):
    return lax.dot_general(a, b, (((0,), (0,)), ((), ())), preferred_element_type=F32)


def _wkv_prep(problems):
    c = CHUNK
    lane = lax.broadcasted_iota(jnp.int32, (c, LANES), 1)
    m0 = lane < HEAD_DIM
    row = lax.broadcasted_iota(jnp.int32, (LANES, LANES), 0)
    col = lax.broadcasted_iota(jnp.int32, (LANES, LANES), 1)
    same = (row // c) == (col // c)
    strict = same & ((col % c) < (row % c))
    incl = same & ((col % c) <= (row % c))
    eye = row == col

    def stack(t):
        zero = jnp.zeros_like(t)
        return jnp.concatenate([jnp.where(m0, t, zero), jnp.where(m0, zero, t)], axis=0)

    def dup(t):
        return jnp.concatenate([t, t], axis=0)

    def fold(t):
        return t[:c] + t[c:]

    n = len(problems)
    xs = [_dot_nt(jnp.concatenate([q['kkt'], q['rt']], axis=0),
                  jnp.concatenate([stack(q['bt']), stack(q['kt'])], axis=0)) for q in problems]
    l_pow = [jnp.where(strict, dup(x[:c, :LANES]), 0.0) for x in xs]
    m_ak = [jnp.where(strict, dup(x[:c, LANES:]), 0.0).astype(BF16) for x in xs]
    a_rbk = [jnp.concatenate([jnp.where(incl, dup(x[c:, :LANES]), 0.0),
                              jnp.where(incl, dup(x[c:, LANES:]), 0.0)], axis=1).astype(BF16) for x in xs]
    t_inv = [jnp.where(eye, 1.0, 0.0) - l for l in l_pow]
    for _ in range(5):
        l16 = [l.astype(BF16) for l in l_pow]
        l_pow = [_dot(l, l) for l in l16]
        upd = [_dot(t.astype(BF16), l.astype(BF16)) for t, l in zip(t_inv, l_pow)]
        t_inv = [t + u for t, u in zip(t_inv, upd)]

    vs = [stack(q['v']) for q in problems]
    mv = [_dot(m, v) for m, v in zip(m_ak, vs)]
    z = [_dot(t.astype(BF16), jnp.concatenate([stack(q['kkt']), m.astype(BF16)], axis=1))
         for t, q, m in zip(t_inv, problems, mv)]
    zero_sq = jnp.zeros((LANES, LANES), BF16)
    qy = [_dot(a, jnp.concatenate([(-zz).astype(BF16), jnp.concatenate([zero_sq, v], axis=1)], axis=0))
          for a, zz, v in zip(a_rbk, z, vs)]
    zero_c = jnp.zeros((c, LANES), BF16)
    pd = [_dot_tn(jnp.concatenate([q['bh'], q['kh']], axis=0),
                  jnp.concatenate([jnp.concatenate([fold(-zz[:, :LANES]), fold(-zz[:, LANES:])], axis=1).astype(BF16),
                                   jnp.concatenate([zero_c, q['v']], axis=1)], axis=0))
          for q, zz in zip(problems, z)]
    out = []
    for i in range(n):
        q = problems[i]
        qs = stack(q['rt']).astype(F32) + qy[i][:, :LANES]
        phi = (jnp.where(eye, jnp.broadcast_to(q['pc'], (LANES, LANES)), 0.0)
               + jnp.where(same, pd[i][:, :LANES], 0.0))
        delta = jnp.where(same, pd[i][:, LANES:], 0.0)
        qphi = jnp.concatenate([fold(qs), phi], axis=0).astype(BF16)
        out.append((qphi, fold(qy[i][:, LANES:]), delta))
    return out


def _rwkv_body(pr_ref, halo_ref, mu_ref, w0_ref, w2_ref, a0_ref, a2_ref, g2_ref, kk_ref, ka_ref,
               rk_ref, lnw_ref, lnb_ref, ones_ref, tri_ref, o_ref,
               hp_sc, rt_sc, kt_sc, bt_sc, kkt_sc, kh_sc, bh_sc, v_sc, pc_sc, qphi_sc, delta_sc, y_sc,
               *, rows):
    i = pl.program_id(1)
    n_pairs = RWKV_WIDTH // LANES
    n_chunks = rows // CHUNK

    @pl.when(i == 0)
    def _():
        hp_sc[...] = jnp.zeros(hp_sc.shape, F32)

    pr = pr_ref[0]
    first = jnp.where(i == 0, 0.0, halo_ref[0, 7:8, :])
    prev = jnp.concatenate([first, pr[:-1]], axis=0)
    xs = pr + (prev - pr) * mu_ref[...]
    w = RWKV_WIDTH
    r = xs[:, :w]
    k = xs[:, w:2 * w]
    v = xs[:, 2 * w:3 * w]
    x_lora = xs[:, 3 * w:3 * w + DECAY_LORA + AAA_LORA]
    xg = xs[:, 3 * w + DECAY_LORA + AAA_LORA:]
    wl = w0_ref[...] + _dot(jnp.tanh(x_lora).astype(BF16), w2_ref[...])
    sp = jnp.maximum(-wl, 0.0) + jnp.log(1.0 + jnp.exp(-jnp.abs(wl)))
    lw = -jnp.exp(-sp - 0.5)
    a = jax.nn.sigmoid(a0_ref[...] + _dot(x_lora.astype(BF16), a2_ref[...]))
    g = _dot(jax.nn.sigmoid(xg).astype(BF16), g2_ref[...])
    ones = ones_ref[...]
    kx = k * kk_ref[...]
    ssq = _dot((kx * kx).astype(BF16), ones)
    kn = kx / jnp.maximum(jnp.sqrt(ssq), 1e-12)
    km = k * (1.0 + (a - 1.0) * ka_ref[...])
    bonus = _dot((r * km * rk_ref[...]).astype(BF16), ones) * v

    hi = lw.astype(BF16)
    r1 = lw - hi.astype(F32)
    mid = r1.astype(BF16)
    lo = (r1 - mid.astype(F32)).astype(BF16)
    tri = tri_ref[...]
    cum = _dot(tri, hi) + _dot(tri, mid) + _dot(tri, lo)

    cum3 = cum.reshape(n_chunks, CHUNK, w)
    end3 = cum3[:, CHUNK - 1:CHUNK, :]
    p_inv = jnp.exp(-cum)
    p_end = jnp.exp(end3 - cum3).reshape(rows, w)
    b = kn * a
    rt_sc[...] = (r * jnp.exp(cum)).astype(BF16)
    kt_sc[...] = (km * p_inv).astype(BF16)
    bt_sc[...] = (b * p_inv).astype(BF16)
    kkt_sc[...] = (kn * jnp.exp(cum - lw)).astype(BF16)
    kh_sc[...] = (km * p_end).astype(BF16)
    bh_sc[...] = (b * p_end).astype(BF16)
    v_sc[...] = v.astype(BF16)
    pc_sc[...] = jnp.broadcast_to(jnp.exp(end3), pc_sc.shape)

    group = 2 if n_chunks % 2 == 0 else 1

    def prep_chunks(gi, carry):
        where = []
        for g in range(group):
            ci = gi * group + g
            rs = pl.ds(pl.multiple_of(ci * CHUNK, CHUNK), CHUNK)
            for p in range(n_pairs):
                where.append((ci, rs, p, slice(p * LANES, (p + 1) * LANES)))
        problems = [dict(rt=rt_sc[rs, sl], kt=kt_sc[rs, sl], bt=bt_sc[rs, sl], kkt=kkt_sc[rs, sl],
                         kh=kh_sc[rs, sl], bh=bh_sc[rs, sl], v=v_sc[rs, sl], pc=pc_sc[ci, 0:1, sl])
                    for ci, rs, p, sl in where]
        for (ci, rs, p, sl), (qphi, y0, delta) in zip(where, _wkv_prep(problems)):
            qphi_sc[ci, p] = qphi
            delta_sc[ci, p] = delta
            y_sc[rs, sl] = y0
        return carry

    lax.fori_loop(0, n_chunks // group, prep_chunks, 0)

    def scan_chunk(ci, carry):
        rs = pl.ds(pl.multiple_of(ci * CHUNK, CHUNK), CHUNK)
        for p in range(n_pairs):
            sl = slice(p * LANES, (p + 1) * LANES)
            res = _dot(qphi_sc[ci, p], hp_sc[p].astype(BF16))
            y_sc[rs, sl] += res[:CHUNK]
            hp_sc[p] = res[CHUNK:] + delta_sc[ci, p]
        return carry

    lax.fori_loop(0, n_chunks, scan_chunk, 0, unroll=True)

    y = y_sc[...]
    mean = _dot(y.astype(BF16), ones) * (1.0 / HEAD_DIM)
    yc = y - mean
    var = _dot((yc * yc).astype(BF16), ones) * (1.0 / HEAD_DIM)
    yn = yc * lax.rsqrt(var + GN_EPS) * lnw_ref[...] + lnb_ref[...]
    o_ref[0] = ((yn + bonus) * g).astype(o_ref.dtype)


def _rwkv(pr3, mu, w0, w2p, a0, a2p, g2, k_k, k_a, r_k, ln_w, ln_b, rows):
    batch, seq, _ = pr3.shape
    nb = seq // rows
    fixed = lambda b, i: (0, 0)
    vec = pl.BlockSpec((1, RWKV_WIDTH), fixed)
    lora = DECAY_LORA + AAA_LORA
    n_pairs = RWKV_WIDTH // LANES
    ci = jnp.arange(rows) // CHUNK
    ti = jnp.arange(rows)
    tri = ((ci[:, None] == ci[None, :]) & (ti[None, :] <= ti[:, None])).astype(BF16)
    big = pltpu.VMEM((rows, RWKV_WIDTH), F32)
    n_chunks = rows // CHUNK
    return pl.pallas_call(
        functools.partial(_rwkv_body, rows=rows),
        grid=(batch, nb),
        in_specs=[
            pl.BlockSpec((1, rows, RWKV_COLS), lambda b, i: (b, i, 0)),
            pl.BlockSpec((1, 8, RWKV_COLS), lambda b, i: (b, jnp.maximum(i * (rows // 8) - 1, 0), 0)),
            pl.BlockSpec((1, RWKV_COLS), fixed),
            vec,
            pl.BlockSpec((lora, RWKV_WIDTH), fixed),
            vec,
            pl.BlockSpec((lora, RWKV_WIDTH), fixed),
            pl.BlockSpec((GATE_LORA, RWKV_WIDTH), fixed),
            vec, vec, vec, vec, vec,
            pl.BlockSpec((RWKV_WIDTH, RWKV_WIDTH), fixed),
            pl.BlockSpec((rows, rows), fixed),
        ],
        out_specs=pl.BlockSpec((1, rows, RWKV_WIDTH), lambda b, i: (b, i, 0)),
        out_shape=jax.ShapeDtypeStruct((batch, seq, RWKV_WIDTH), F32),
        scratch_shapes=(
            [pltpu.VMEM((n_pairs, LANES, LANES), F32)]
            + [pltpu.VMEM((rows, RWKV_WIDTH), BF16)] * 7
            + [pltpu.VMEM((n_chunks, 8, RWKV_WIDTH), F32),
               pltpu.VMEM((n_chunks, n_pairs, CHUNK + LANES, LANES), BF16),
               pltpu.VMEM((n_chunks, n_pairs, LANES, LANES), F32),
               big]),
        compiler_params=pltpu.CompilerParams(
            dimension_semantics=("parallel", "arbitrary"), vmem_limit_bytes=VMEM_LIMIT),
        name="rwkv7",
    )(pr3, pr3, mu, w0, w2p, a0, a2p, g2, k_k, k_a, r_k, ln_w, ln_b,
      _group_ones(RWKV_WIDTH, HEAD_DIM), tri)


ROUTE_COLS = N_GROUPS + N_EXPERTS


def _merge_body(x_ref, gate_ref, orw_ref, odf_ref, wbr_ref, wbd_ref, wout_ref, nf_ref, wr_ref,
                br_ref, x1_ref, h2_ref, route_ref):
    gates = gate_ref[...].astype(F32)
    mixed = (gates[:, :D_MODEL] * _dot(orw_ref[...].astype(BF16), wbr_ref[...])
             + gates[:, D_MODEL:] * _dot(odf_ref[...].astype(BF16), wbd_ref[...]))
    x1 = x_ref[...] + _dot(mixed.astype(BF16), wout_ref[...])
    x1_ref[...] = x1
    ms = jnp.mean(x1 * x1, axis=-1, keepdims=True)
    h2 = x1 * lax.rsqrt(ms + RMS_EPS) * nf_ref[...]
    _store_token_tiles(h2_ref, 0, h2)

    hi = h2.astype(BF16)
    lo = (h2 - hi.astype(F32)).astype(BF16)
    acc = _dot(hi, wr_ref[...])
    logits = acc[:, :LANES] + acc[:, LANES:] + _dot(lo, wr_ref[:, :LANES]) + br_ref[...]
    lane = lax.broadcasted_iota(jnp.int32, logits.shape, 1).astype(F32)
    neg = -jnp.inf

    def top(vals):
        m = jnp.max(vals, axis=-1, keepdims=True)
        return m, jnp.min(jnp.where(vals == m, lane, float(LANES)), axis=-1, keepdims=True)

    gl = jnp.where(lane < N_GROUPS, logits, neg)
    gm, g_idx = top(gl)
    g_top = 1.0 / jnp.sum(jnp.exp(gl - gm), axis=-1, keepdims=True)
    first = N_GROUPS + EXPERTS_PER_GROUP * g_idx
    el = jnp.where((lane >= first) & (lane < first + EXPERTS_PER_GROUP), logits, neg)
    t1, i1 = top(el)
    t2, i2 = top(jnp.where(lane == i1, neg, el))
    e2 = jnp.exp(t2 - t1)
    w1 = g_top / (1.0 + e2)
    w2 = g_top * e2 / (1.0 + e2)
    route_ref[...] = jnp.where(lane == 0, i1 - N_GROUPS, jnp.where(lane == 1, i2 - N_GROUPS,
                               jnp.where(lane == 2, w1, jnp.where(lane == 3, w2, 0.0))))


def _merge(x2, gates, o_rwkv, o_diff, w_br, w_bd, w_out, norm_ffn, w_route, b_route, tm):
    t = x2.shape[0]
    row = lambda i: (i, 0)
    fixed = lambda i: (0, 0)
    return pl.pallas_call(
        _merge_body,
        grid=(t // tm,),
        in_specs=[
            pl.BlockSpec((tm, D_MODEL), row),
            pl.BlockSpec((tm, GATE_COLS), row),
            pl.BlockSpec((tm, RWKV_WIDTH), row),
            pl.BlockSpec((tm, DIFF_V_WIDTH), row),
            pl.BlockSpec((RWKV_WIDTH, D_MODEL), fixed),
            pl.BlockSpec((DIFF_V_WIDTH, D_MODEL), fixed),
            pl.BlockSpec((D_MODEL, D_MODEL), fixed),
            pl.BlockSpec((1, D_MODEL), fixed),
            pl.BlockSpec((D_MODEL, 2 * LANES), fixed),
            pl.BlockSpec((1, LANES), fixed),
        ],
        out_specs=[
            pl.BlockSpec((tm, D_MODEL), row),
            pl.BlockSpec((tm * TILE_ROWS, LANES), row),
            pl.BlockSpec((tm, LANES), row),
        ],
        out_shape=[
            jax.ShapeDtypeStruct((t, D_MODEL), F32),
            jax.ShapeDtypeStruct((t * TILE_ROWS, LANES), F32),
            jax.ShapeDtypeStruct((t, LANES), F32),
        ],
        compiler_params=pltpu.CompilerParams(
            dimension_semantics=("parallel",), vmem_limit_bytes=VMEM_LIMIT),
        name="merge_router",
    )(x2, gates, o_rwkv, o_diff, w_br, w_bd, w_out, norm_ffn, w_route, b_route)


def _expert_body(blk_a_ref, blk_b_ref, rows_ref, h2_hbm, wga_ref, wua_ref, wda_ref, wgb_ref, wub_ref,
                 wdb_ref, yb_ref, xbuf, gsem, *, n_steps):
    del blk_a_ref, blk_b_ref
    g = pl.program_id(0)

    def start_gather(block, s):
        for r in range(ROW_BLOCK):
            tok = rows_ref[block * ROW_BLOCK + r]
            pltpu.make_async_copy(_tile(h2_hbm, tok), _tile(xbuf, s * ROW_BLOCK + r), gsem.at[s]).start(r % 2)

    def wait_gather(s):
        for _ in range(ROW_BLOCK):
            pltpu.make_async_copy(_tile(h2_hbm, 0), _tile(xbuf, 0), gsem.at[s]).wait()

    @pl.when(g == 0)
    def _():
        start_gather(0, 0)
        start_gather(1, 1)

    def run_block(s, wg_ref, wu_ref, wd_ref):
        wait_gather(s)
        x = _load_token_tiles(xbuf, s * ROW_BLOCK * TILE_ROWS, ROW_BLOCK).astype(BF16)
        start_gather(2 * g + s + 2, s)
        gt = _dot(x, wg_ref[0])
        up = _dot(x, wu_ref[0])
        mid = (gt * jax.nn.sigmoid(gt) * up).astype(BF16)
        _store_token_tiles(yb_ref, s * ROW_BLOCK * TILE_ROWS, _dot(mid, wd_ref[0]))

    run_block(0, wga_ref, wua_ref, wda_ref)
    run_block(1, wgb_ref, wub_ref, wdb_ref)

    @pl.when(g == n_steps - 1)
    def _():
        for s in range(2):
            wait_gather(s)


def _experts(blk_e, rows, h2, w_gate, w_up, w_down):
    n_blocks = blk_e.shape[0]
    assert n_blocks % 2 == 0
    n_steps = n_blocks // 2
    wspec_a = lambda shape: pl.BlockSpec((1,) + shape, lambda g, ba, bb, rows: (ba[g], 0, 0))
    wspec_b = lambda shape: pl.BlockSpec((1,) + shape, lambda g, ba, bb, rows: (bb[g], 0, 0))
    grid_spec = pltpu.PrefetchScalarGridSpec(
        num_scalar_prefetch=3,
        grid=(n_steps,),
        in_specs=[
            pl.BlockSpec(memory_space=pl.ANY),
            wspec_a((D_MODEL, EXPERT_FF)), wspec_a((D_MODEL, EXPERT_FF)), wspec_a((EXPERT_FF, D_MODEL)),
            wspec_b((D_MODEL, EXPERT_FF)), wspec_b((D_MODEL, EXPERT_FF)), wspec_b((EXPERT_FF, D_MODEL)),
        ],
        out_specs=pl.BlockSpec((2 * ROW_BLOCK * TILE_ROWS, LANES), lambda g, ba, bb, rows: (g, 0)),
        scratch_shapes=[pltpu.VMEM((2 * ROW_BLOCK * TILE_ROWS, LANES), F32), pltpu.SemaphoreType.DMA((2,))],
    )
    return pl.pallas_call(
        functools.partial(_expert_body, n_steps=n_steps),
        grid_spec=grid_spec,
        out_shape=jax.ShapeDtypeStruct((n_blocks * ROW_BLOCK * TILE_ROWS, LANES), F32),
        compiler_params=pltpu.CompilerParams(
            dimension_semantics=("arbitrary",), vmem_limit_bytes=VMEM_LIMIT),
        name="experts",
    )(blk_e[0::2], blk_e[1::2], rows, h2, w_gate, w_up, w_down, w_gate, w_up, w_down)


def _final_body(pos_ref, x1_ref, route_ref, p_ref, yb_hbm, np_ref, wpg_ref, wpp_ref, o_ref,
                ybuf, sem, *, tm, n_steps):
    i = pl.program_id(0)
    per_tile = TOP_K * tm

    def start_gather(tile_idx, s):
        for r in range(per_tile):
            row = pos_ref[tile_idx * per_tile + r]
            pltpu.make_async_copy(_tile(yb_hbm, row), _tile(ybuf, s * per_tile + r), sem.at[s]).start(r % 2)

    def wait_gather(s):
        for _ in range(per_tile):
            pltpu.make_async_copy(_tile(yb_hbm, 0), _tile(ybuf, 0), sem.at[s]).wait()

    @pl.when(i == 0)
    def _():
        start_gather(0, 0)
        start_gather(1, 1)

    for s in range(2):
        rows = pl.ds(s * tm, tm)
        wait_gather(s)
        y_first = _load_token_tiles(ybuf, s * per_tile * TILE_ROWS, tm)
        y_second = _load_token_tiles(ybuf, (s * per_tile + tm) * TILE_ROWS, tm)
        start_gather(2 * i + s + 2, s)
        route = route_ref[rows, :]
        x2 = x1_ref[rows, :] + route[:, 2:3] * y_first + route[:, 3:4] * y_second
        ms = jnp.mean(x2 * x2, axis=-1, keepdims=True)
        hn = (x2 * lax.rsqrt(ms + RMS_EPS) * np_ref[...]).astype(BF16)
        gate = jax.nn.sigmoid(_dot(hn, wpg_ref[...]))
        o_ref[rows, :] = x2 + gate * _dot(p_ref[rows, :].astype(BF16), wpp_ref[...])

    @pl.when(i == n_steps - 1)
    def _():
        for s in range(2):
            wait_gather(s)


def _final(pos, x1, route, p2, yb, norm_ple, w_pg, w_pp, tm):
    t = x1.shape[0]
    assert t % (2 * tm) == 0
    n_steps = t // (2 * tm)
    row = lambda i, pos: (i, 0)
    fixed = lambda i, pos: (0, 0)
    grid_spec = pltpu.PrefetchScalarGridSpec(
        num_scalar_prefetch=1,
        grid=(n_steps,),
        in_specs=[
            pl.BlockSpec((2 * tm, D_MODEL), row),
            pl.BlockSpec((2 * tm, LANES), row),
            pl.BlockSpec((2 * tm, PLE_DIM), row),
            pl.BlockSpec(memory_space=pl.ANY),
            pl.BlockSpec((1, D_MODEL), fixed),
            pl.BlockSpec((D_MODEL, D_MODEL), fixed),
            pl.BlockSpec((PLE_DIM, D_MODEL), fixed),
        ],
        out_specs=pl.BlockSpec((2 * tm, D_MODEL), row),
        scratch_shapes=[pltpu.VMEM((2 * TOP_K * tm * TILE_ROWS, LANES), F32), pltpu.SemaphoreType.DMA((2,))],
    )
    return pl.pallas_call(
        functools.partial(_final_body, tm=tm, n_steps=n_steps),
        grid_spec=grid_spec,
        out_shape=jax.ShapeDtypeStruct((t, D_MODEL), F32),
        compiler_params=pltpu.CompilerParams(
            dimension_semantics=("arbitrary",), vmem_limit_bytes=VMEM_LIMIT),
        name="combine_ple",
    )(pos, x1, route, p2, yb, norm_ple, w_pg, w_pp)


def _dispatch_plan(route, n_tokens, tm):
    eid = route[:, :TOP_K].astype(jnp.int32).reshape(-1)
    n_assign = n_tokens * TOP_K
    order = jnp.argsort(eid).astype(jnp.int32)
    rank = jnp.argsort(order).astype(jnp.int32)
    experts = jnp.arange(N_EXPERTS, dtype=jnp.int32)
    onehot = eid[:, None] == experts[None, :]
    counts = jnp.sum(onehot.astype(jnp.int32), axis=0)
    starts = jnp.cumsum(counts) - counts
    pcounts = ((counts + ROW_BLOCK - 1) // ROW_BLOCK) * ROW_BLOCK
    pends = jnp.cumsum(pcounts)
    pstarts = pends - pcounts
    n_blocks = (n_assign + N_EXPERTS * ROW_BLOCK) // ROW_BLOCK
    row0 = jnp.arange(n_blocks, dtype=jnp.int32) * ROW_BLOCK
    blk_e = jnp.minimum(jnp.sum((pends[None, :] <= row0[:, None]).astype(jnp.int32), axis=1), N_EXPERTS - 1)
    offset = row0 - pstarts[blk_e]
    valid = counts[blk_e] - offset
    first = jnp.clip(starts[blk_e] + offset, 0, n_assign - 1)
    r = jnp.arange(ROW_BLOCK, dtype=jnp.int32)[None, :]
    idx = jnp.minimum(first[:, None] + r, n_assign - 1)
    rows = jnp.where(r < valid[:, None], order[idx] // TOP_K, 0).reshape(-1)
    rows = jnp.concatenate([rows, jnp.zeros((2 * ROW_BLOCK,), jnp.int32)])
    shift = jnp.sum(jnp.where(onehot, (pstarts - starts)[None, :], 0), axis=1)
    pos = (rank + shift).reshape(n_tokens // tm, tm, TOP_K).transpose(0, 2, 1).reshape(-1)
    pos = jnp.concatenate([pos, jnp.zeros((2 * TOP_K * tm,), jnp.int32)])
    return blk_e, rows, pos


def _rope_tables(seq):
    inv = ROPE_THETA ** (-jnp.arange(0, HEAD_DIM, 2, dtype=F32) / HEAD_DIM)
    ang = jnp.arange(seq, dtype=F32)[:, None] * inv[None, :]
    ang = jnp.concatenate([ang, ang], axis=-1)
    reps = DIFF_QK_WIDTH // HEAD_DIM
    return jnp.tile(jnp.cos(ang), (1, reps)), jnp.tile(jnp.sin(ang), (1, reps))


def _layer(x2, p2, batch, seq, lam_init, prm):
    t = x2.shape[0]
    tm = 256
    cos, sin = _rope_tables(seq)
    reps = DIFF_QK_WIDTH // HEAD_DIM
    prw, q, k, v, gates = _inproj(
        x2, prm['norm_mix'][None], prm['w_in'].astype(BF16), cos, sin,
        jnp.tile(prm['q_norm'], reps)[None], jnp.tile(prm['k_norm'], reps)[None], seq, tm)

    zpad = jnp.zeros((DECAY_LORA, RWKV_WIDTH), F32)
    w2p = jnp.concatenate([prm['rwkv_w2'], zpad], axis=0).astype(BF16)
    a2p = jnp.concatenate([zpad, prm['rwkv_a2']], axis=0).astype(BF16)
    o_rwkv = _rwkv(
        prw.reshape(batch, seq, RWKV_COLS), prm['rwkv_mu'][None], prm['rwkv_w0'][None], w2p,
        prm['rwkv_a0'][None], a2p, prm['rwkv_g2'].astype(BF16), prm['rwkv_k_k'][None],
        prm['rwkv_k_a'][None], prm['rwkv_r_k'].reshape(1, RWKV_WIDTH), prm['rwkv_ln_w'][None],
        prm['rwkv_ln_b'][None], min(512, seq)).reshape(t, RWKV_WIDTH)

    o_diff = _diff_attention(
        q, k, v, prm['lambda_q1'][None], prm['lambda_k1'][None], prm['lambda_q2'][None],
        prm['lambda_k2'][None], prm['subln_w'][None], batch, seq, lam_init, min(256, seq))

    w_r = jnp.concatenate([prm['w_group'], prm['w_expert_router']], axis=1)
    w_r = jnp.pad(w_r, ((0, 0), (0, LANES - ROUTE_COLS)))
    w_r_hi = w_r.astype(BF16)
    w_r_lo = (w_r - w_r_hi.astype(F32)).astype(BF16)
    b_r = jnp.pad(jnp.concatenate([prm['b_group'], prm['b_expert_router']]), (0, LANES - ROUTE_COLS))[None]
    x1, h2, route = _merge(
        x2, gates, o_rwkv, o_diff, prm['w_branch_rwkv'].astype(BF16), prm['w_branch_diff'].astype(BF16),
        prm['w_out'].astype(BF16), prm['norm_ffn'][None], jnp.concatenate([w_r_hi, w_r_lo], axis=1), b_r, tm)

    blk_e, rows, pos = _dispatch_plan(route, t, tm)
    yb = _experts(blk_e, rows, h2, prm['w_gate'].astype(BF16), prm['w_up'].astype(BF16),
                  prm['w_down'].astype(BF16))
    return _final(pos, x1, route, p2, yb, prm['norm_ple'][None], prm['w_ple_gate'].astype(BF16),
                  prm['w_ple_proj'].astype(BF16), tm)


_PARAM_NAMES = (
    'norm_mix', 'w_in', 'rwkv_mu', 'rwkv_w0', 'rwkv_w2', 'rwkv_a0', 'rwkv_a2', 'rwkv_g2', 'rwkv_k_k',
    'rwkv_k_a', 'rwkv_r_k', 'rwkv_ln_w', 'rwkv_ln_b', 'q_norm', 'k_norm', 'lambda_q1', 'lambda_k1',
    'lambda_q2', 'lambda_k2', 'subln_w', 'w_branch_rwkv', 'w_branch_diff', 'w_out', 'norm_ffn',
    'w_group', 'b_group', 'w_expert_router', 'b_expert_router', 'w_gate', 'w_up', 'w_down', 'norm_ple',
    'w_ple_gate', 'w_ple_proj')


def kernel(x, p, norm_mix, w_in, rwkv_mu, rwkv_w0, rwkv_w2, rwkv_a0, rwkv_a2, rwkv_g2, rwkv_k_k,
           rwkv_k_a, rwkv_r_k, rwkv_ln_w, rwkv_ln_b, q_norm, k_norm, lambda_q1, lambda_k1, lambda_q2,
           lambda_k2, subln_w, w_branch_rwkv, w_branch_diff, w_out, norm_ffn, w_group, b_group,
           w_expert_router, b_expert_router, w_gate, w_up, w_down, norm_ple, w_ple_gate, w_ple_proj):
    stacked = dict(zip(_PARAM_NAMES, (
        norm_mix, w_in, rwkv_mu, rwkv_w0, rwkv_w2, rwkv_a0, rwkv_a2, rwkv_g2, rwkv_k_k, rwkv_k_a,
        rwkv_r_k, rwkv_ln_w, rwkv_ln_b, q_norm, k_norm, lambda_q1, lambda_k1, lambda_q2, lambda_k2,
        subln_w, w_branch_rwkv, w_branch_diff, w_out, norm_ffn, w_group, b_group, w_expert_router,
        b_expert_router, w_gate, w_up, w_down, norm_ple, w_ple_gate, w_ple_proj)))
    batch, seq, _ = x.shape
    depth = p.shape[0]
    x2 = x.reshape(batch * seq, D_MODEL)
    for layer in range(depth):
        lam_init = 0.8 - 0.6 * math.exp(-0.3 * layer)
        prm = {name: val[layer] for name, val in stacked.items()}
        x2 = _layer(x2, p[layer].reshape(batch * seq, PLE_DIM), batch, seq, lam_init, prm)
    return x2.reshape(batch, seq, D_MODEL)
```

```python
import functools
import math

import jax
import jax.numpy as jnp
from jax import lax
from jax.experimental import pallas as pl
from jax.experimental.pallas import tpu as pltpu

F32 = jnp.float32
BF16 = jnp.bfloat16

D_MODEL = 1024
PLE_DIM = 256
RMS_EPS = 1e-6

RWKV_HEADS = 8
HEAD_DIM = 64
RWKV_WIDTH = RWKV_HEADS * HEAD_DIM
DECAY_LORA = 64
AAA_LORA = 64
GATE_LORA = 128
GN_EPS = 64e-5
RWKV_COLS = 3 * RWKV_WIDTH + DECAY_LORA + AAA_LORA + GATE_LORA

DIFF_HEADS = 4
DIFF_QK_WIDTH = DIFF_HEADS * 2 * HEAD_DIM
DIFF_V_DIM = 2 * HEAD_DIM
DIFF_V_WIDTH = DIFF_HEADS * DIFF_V_DIM
DIFF_COLS = 2 * DIFF_QK_WIDTH + DIFF_V_WIDTH
ROPE_THETA = 10000.0
GATE_COLS = 2 * D_MODEL

N_GROUPS = 4
EXPERTS_PER_GROUP = 8
N_EXPERTS = N_GROUPS * EXPERTS_PER_GROUP
TOP_K = 2
EXPERT_FF = 512
ROW_BLOCK = 128

LANES = 128
CHUNK = 64
VMEM_LIMIT = 56 * 1024 * 1024


def _dot(a, b):
    return jnp.dot(a, b, preferred_element_type=F32)


def _dot_nt(a, b):
    return lax.dot_general(a, b, (((1,), (1,)), ((), ())), preferred_element_type=F32)


def _dot_hi(a, b):
    return jnp.dot(a.astype(BF16), b.astype(BF16), preferred_element_type=F32)


SUBLANES = 8
TILE_ROWS = D_MODEL // LANES
assert TILE_ROWS == SUBLANES


def _store_token_tiles(ref, base, val, pitch=TILE_ROWS):
    n = val.shape[0]
    for j in range(TILE_ROWS):
        ref[pl.ds(base + j, n, stride=pitch), :] = val[:, j * LANES:(j + 1) * LANES]


def _load_token_tiles(ref, base, n, pitch=TILE_ROWS):
    return jnp.concatenate([ref[pl.ds(base + j, n, stride=pitch), :] for j in range(TILE_ROWS)], axis=1)


def _tile(ref, index):
    return ref.at[pl.ds(pl.multiple_of(index * TILE_ROWS, TILE_ROWS), TILE_ROWS)]


def _group_ones(width, group):
    i = jnp.arange(width) // group
    return (i[:, None] == i[None, :]).astype(BF16)


def _rotate_half(t):
    width = t.shape[-1]
    lane = lax.broadcasted_iota(jnp.int32, t.shape, 1)
    fwd = pltpu.roll(t, width - HEAD_DIM // 2, 1)
    bwd = pltpu.roll(t, HEAD_DIM // 2, 1)
    return jnp.where(lane % HEAD_DIM < HEAD_DIM // 2, -fwd, bwd)


def _inproj_body(x_ref, g_ref, w_ref, cos_ref, sin_ref, qg_ref, kg_ref, ones_ref,
                 prw_ref, q_ref, k_ref, v_ref, gate_ref):
    x = x_ref[...]
    ms = jnp.mean(x * x, axis=-1, keepdims=True)
    h = (x * lax.rsqrt(ms + RMS_EPS) * g_ref[...]).astype(BF16)
    prw_ref[...] = _dot(h, w_ref[:, :RWKV_COLS])
    pd = _dot(h, w_ref[:, RWKV_COLS:RWKV_COLS + DIFF_COLS])
    cos = cos_ref[...]
    sin = sin_ref[...]

    def qk_prep(t, gain, scale):
        ssq = _dot((t * t).astype(BF16), ones_ref[...])
        t = t * lax.rsqrt(ssq * (1.0 / HEAD_DIM) + RMS_EPS) * gain
        return ((t * cos + _rotate_half(t) * sin) * scale).astype(BF16)

    q_ref[...] = qk_prep(pd[:, :DIFF_QK_WIDTH], qg_ref[...], HEAD_DIM ** -0.5)
    k_ref[...] = qk_prep(pd[:, DIFF_QK_WIDTH:2 * DIFF_QK_WIDTH], kg_ref[...], 1.0)
    v_ref[...] = pd[:, 2 * DIFF_QK_WIDTH:].astype(BF16)
    gate_ref[...] = jax.nn.sigmoid(_dot(h, w_ref[:, RWKV_COLS + DIFF_COLS:])).astype(BF16)


def _inproj(x2, norm_g, w_in, cos, sin, q_gain, k_gain, seq, tm):
    t = x2.shape[0]
    n_seq_tiles = seq // tm
    row = lambda i: (i, 0)
    fixed = lambda i: (0, 0)
    rope = lambda i: (i % n_seq_tiles, 0)
    return pl.pallas_call(
        _inproj_body,
        grid=(t // tm,),
        in_specs=[
            pl.BlockSpec((tm, D_MODEL), row),
            pl.BlockSpec((1, D_MODEL), fixed),
            pl.BlockSpec((D_MODEL, RWKV_COLS + DIFF_COLS + GATE_COLS), fixed),
            pl.BlockSpec((tm, DIFF_QK_WIDTH), rope),
            pl.BlockSpec((tm, DIFF_QK_WIDTH), rope),
            pl.BlockSpec((1, DIFF_QK_WIDTH), fixed),
            pl.BlockSpec((1, DIFF_QK_WIDTH), fixed),
            pl.BlockSpec((DIFF_QK_WIDTH, DIFF_QK_WIDTH), fixed),
        ],
        out_specs=[
            pl.BlockSpec((tm, RWKV_COLS), row),
            pl.BlockSpec((tm, DIFF_QK_WIDTH), row),
            pl.BlockSpec((tm, DIFF_QK_WIDTH), row),
            pl.BlockSpec((tm, DIFF_V_WIDTH), row),
            pl.BlockSpec((tm, GATE_COLS), row),
        ],
        out_shape=[
            jax.ShapeDtypeStruct((t, RWKV_COLS), F32),
            jax.ShapeDtypeStruct((t, DIFF_QK_WIDTH), BF16),
            jax.ShapeDtypeStruct((t, DIFF_QK_WIDTH), BF16),
            jax.ShapeDtypeStruct((t, DIFF_V_WIDTH), BF16),
            jax.ShapeDtypeStruct((t, GATE_COLS), BF16),
        ],
        compiler_params=pltpu.CompilerParams(
            dimension_semantics=("parallel",), vmem_limit_bytes=VMEM_LIMIT),
        name="inproj",
    )(x2, norm_g, w_in, cos, sin, q_gain, k_gain, _group_ones(DIFF_QK_WIDTH, HEAD_DIM))


def _attn_body(q_ref, k_ref, v_ref, lq1_ref, lk1_ref, lq2_ref, lk2_ref, sw_ref, o_ref,
               qs_sc, m_sc, l_sc, acc_sc, s0_sc, s1_sc, *, tq, wide, lam_init):
    qi = pl.program_id(2)
    heads = range(q_ref.shape[1] // LANES)
    cols = [slice(h * LANES, (h + 1) * LANES) for h in heads]
    for h in heads:
        q = q_ref[:, cols[h]]
        lane = lax.broadcasted_iota(jnp.int32, q.shape, 1)
        qs_sc[h, :tq, :] = jnp.where(lane < HEAD_DIM, q, jnp.zeros_like(q))
        qs_sc[h, tq:, :] = jnp.where(lane >= HEAD_DIM, q, jnp.zeros_like(q))
    m_sc[...] = jnp.full(m_sc.shape, -jnp.inf, F32)
    l_sc[...] = jnp.zeros(l_sc.shape, F32)
    acc_sc[...] = jnp.zeros(acc_sc.shape, F32)

    width = wide * tq

    def scores(j, s_sc):
        for h in heads:
            kb = k_ref[pl.ds(pl.multiple_of(j * width, width), width), cols[h]]
            s_sc[h] = _dot_nt(qs_sc[h], kb)

    def update(j, s_sc, masked):
        for h in heads:
            s = s_sc[h]
            if masked:
                row = lax.broadcasted_iota(jnp.int32, s.shape, 0) % tq + qi * tq
                col = lax.broadcasted_iota(jnp.int32, s.shape, 1) + j * width
                s = jnp.where(col <= row, s, -jnp.inf)
            vb = v_ref[pl.ds(pl.multiple_of(j * width, width), width), cols[h]]
            m_prev = m_sc[h]
            m_new = jnp.maximum(m_prev, jnp.max(s, axis=-1, keepdims=True))
            alpha = jnp.exp(m_prev - m_new)
            p = jnp.exp(s - m_new[:, :1])
            l_sc[h] = alpha * l_sc[h] + jnp.sum(p, axis=-1, keepdims=True)
            acc_sc[h] = alpha * acc_sc[h] + _dot(p.astype(BF16), vb)
            m_sc[h] = m_new

    last = qi // wide
    pairs = last // 2
    scores(0, s0_sc)

    def body(jj, carry):
        scores(2 * jj + 1, s1_sc)
        update(2 * jj, s0_sc, False)
        scores(2 * jj + 2, s0_sc)
        update(2 * jj + 1, s1_sc, False)
        return carry

    lax.fori_loop(0, pairs, body, 0)

    @pl.when(last % 2 == 1)
    def _():
        scores(last, s1_sc)
        update(last - 1, s0_sc, False)
        update(last, s1_sc, True)

    @pl.when(last % 2 == 0)
    def _():
        update(last, s0_sc, True)

    lam = (jnp.exp(jnp.sum(lq1_ref[...] * lk1_ref[...], axis=-1, keepdims=True))
           - jnp.exp(jnp.sum(lq2_ref[...] * lk2_ref[...], axis=-1, keepdims=True)) + lam_init)
    for h in heads:
        o = acc_sc[h] / l_sc[h]
        o = o[:tq] - lam * o[tq:]
        ms = jnp.mean(o * o, axis=-1, keepdims=True)
        o_ref[:, cols[h]] = o * lax.rsqrt(ms + RMS_EPS) * sw_ref[...] * (1.0 - lam_init)


def _diff_attention(q, k, v, lq1, lk1, lq2, lk2, subln_w, batch, seq, lam_init, tq):
    t = q.shape[0]
    nq = seq // tq
    vec = pl.BlockSpec((1, HEAD_DIM), lambda b, h, i: (0, 0))
    wide = max(1, min(4, nq // 2))
    assert nq % wide == 0
    hpg = 2
    return pl.pallas_call(
        functools.partial(_attn_body, tq=tq, wide=wide, lam_init=lam_init),
        grid=(batch, DIFF_HEADS // hpg, nq),
        in_specs=[
            pl.BlockSpec((tq, hpg * LANES), lambda b, h, i: (b * nq + i, h)),
            pl.BlockSpec((seq, hpg * LANES), lambda b, h, i: (b, h)),
            pl.BlockSpec((seq, hpg * LANES), lambda b, h, i: (b, h)),
            vec, vec, vec, vec,
            pl.BlockSpec((1, DIFF_V_DIM), lambda b, h, i: (0, 0)),
        ],
        out_specs=pl.BlockSpec((tq, hpg * LANES), lambda b, h, i: (b * nq + i, h)),
        out_shape=jax.ShapeDtypeStruct((t, DIFF_V_WIDTH), F32),
        scratch_shapes=[
            pltpu.VMEM((hpg, 2 * tq, LANES), BF16),
            pltpu.VMEM((hpg, 2 * tq, LANES), F32),
            pltpu.VMEM((hpg, 2 * tq, LANES), F32),
            pltpu.VMEM((hpg, 2 * tq, LANES), F32),
            pltpu.VMEM((hpg, 2 * tq, wide * tq), F32),
            pltpu.VMEM((hpg, 2 * tq, wide * tq), F32),
        ],
        compiler_params=pltpu.CompilerParams(
            dimension_semantics=("parallel", "parallel", "arbitrary"),
            vmem_limit_bytes=VMEM_LIMIT),
        name="diff_attn",
    )(q, k, v, lq1, lk1, lq2, lk2, subln_w)


def _dot_tn(a, b):
    return lax.dot_general(a, b, (((0,), (0,)), ((), ())), preferred_element_type=F32)


def _wkv_prep(problems):
    c = CHUNK
    lane = lax.broadcasted_iota(jnp.int32, (c, LANES), 1)
    m0 = lane < HEAD_DIM
    row = lax.broadcasted_iota(jnp.int32, (LANES, LANES), 0)
    col = lax.broadcasted_iota(jnp.int32, (LANES, LANES), 1)
    same = (row // c) == (col // c)
    strict = same & ((col % c) < (row % c))
    incl = same & ((col % c) <= (row % c))
    eye = row == col

    def stack(t):
        zero = jnp.zeros_like(t)
        return jnp.concatenate([jnp.where(m0, t, zero), jnp.where(m0, zero, t)], axis=0)

    def dup(t):
        return jnp.concatenate([t, t], axis=0)

    def fold(t):
        return t[:c] + t[c:]

    n = len(problems)
    xs = [_dot_nt(jnp.concatenate([q['kkt'], q['rt']], axis=0),
                  jnp.concatenate([stack(q['bt']), stack(q['kt'])], axis=0)) for q in problems]
    l_pow = [jnp.where(strict, dup(x[:c, :LANES]), 0.0) for x in xs]
    m_ak = [jnp.where(strict, dup(x[:c, LANES:]), 0.0).astype(BF16) for x in xs]
    a_rbk = [jnp.concatenate([jnp.where(incl, dup(x[c:, :LANES]), 0.0),
                              jnp.where(incl, dup(x[c:, LANES:]), 0.0)], axis=1).astype(BF16) for x in xs]
    t_inv = [jnp.where(eye, 1.0, 0.0) - l for l in l_pow]
    for _ in range(5):
        l16 = [l.astype(BF16) for l in l_pow]
        l_pow = [_dot(l, l) for l in l16]
        upd = [_dot(t.astype(BF16), l.astype(BF16)) for t, l in zip(t_inv, l_pow)]
        t_inv = [t + u for t, u in zip(t_inv, upd)]

    vs = [stack(q['v']) for q in problems]
    mv = [_dot(m, v) for m, v in zip(m_ak, vs)]
    z = [_dot(t.astype(BF16), jnp.concatenate([stack(q['kkt']), m.astype(BF16)], axis=1))
         for t, q, m in zip(t_inv, problems, mv)]
    zero_sq = jnp.zeros((LANES, LANES), BF16)
    qy = [_dot(a, jnp.concatenate([(-zz).astype(BF16), jnp.concatenate([zero_sq, v], axis=1)], axis=0))
          for a, zz, v in zip(a_rbk, z, vs)]
    zero_c = jnp.zeros((c, LANES), BF16)
    pd = [_dot_tn(jnp.concatenate([q['bh'], q['kh']], axis=0),
                  jnp.concatenate([jnp.concatenate([fold(-zz[:, :LANES]), fold(-zz[:, LANES:])], axis=1).astype(BF16),
                                   jnp.concatenate([zero_c, q['v']], axis=1)], axis=0))
          for q, zz in zip(problems, z)]
    out = []
    for i in range(n):
        q = problems[i]
        qs = stack(q['rt']).astype(F32) + qy[i][:, :LANES]
        phi = (jnp.where(eye, jnp.broadcast_to(q['pc'], (LANES, LANES)), 0.0)
               + jnp.where(same, pd[i][:, :LANES], 0.0))
        delta = jnp.where(same, pd[i][:, LANES:], 0.0)
        qphi = jnp.concatenate([fold(qs), phi], axis=0).astype(BF16)
        out.append((qphi, fold(qy[i][:, LANES:]), delta))
    return out


def _rwkv_body(pr_ref, halo_ref, mu_ref, w0_ref, w2_ref, a0_ref, a2_ref, g2_ref, kk_ref, ka_ref,
               rk_ref, lnw_ref, lnb_ref, ones_ref, tri_ref, o_ref,
               hp_sc, rt_sc, kt_sc, bt_sc, kkt_sc, kh_sc, bh_sc, v_sc, pc_sc, qphi_sc, delta_sc, y_sc,
               *, rows):
    i = pl.program_id(1)
    n_pairs = RWKV_WIDTH // LANES
    n_chunks = rows // CHUNK

    @pl.when(i == 0)
    def _():
        hp_sc[...] = jnp.zeros(hp_sc.shape, F32)

    pr = pr_ref[0]
    first = jnp.where(i == 0, 0.0, halo_ref[0, 7:8, :])
    prev = jnp.concatenate([first, pr[:-1]], axis=0)
    xs = pr + (prev - pr) * mu_ref[...]
    w = RWKV_WIDTH
    r = xs[:, :w]
    k = xs[:, w:2 * w]
    v = xs[:, 2 * w:3 * w]
    x_lora = xs[:, 3 * w:3 * w + DECAY_LORA + AAA_LORA]
    xg = xs[:, 3 * w + DECAY_LORA + AAA_LORA:]
    wl = w0_ref[...] + _dot(jnp.tanh(x_lora).astype(BF16), w2_ref[...])
    sp = jnp.maximum(-wl, 0.0) + jnp.log(1.0 + jnp.exp(-jnp.abs(wl)))
    lw = -jnp.exp(-sp - 0.5)
    a = jax.nn.sigmoid(a0_ref[...] + _dot(x_lora.astype(BF16), a2_ref[...]))
    g = _dot(jax.nn.sigmoid(xg).astype(BF16), g2_ref[...])
    ones = ones_ref[...]
    kx = k * kk_ref[...]
    ssq = _dot((kx * kx).astype(BF16), ones)
    kn = kx / jnp.maximum(jnp.sqrt(ssq), 1e-12)
    km = k * (1.0 + (a - 1.0) * ka_ref[...])
    bonus = _dot((r * km * rk_ref[...]).astype(BF16), ones) * v

    hi = lw.astype(BF16)
    r1 = lw - hi.astype(F32)
    mid = r1.astype(BF16)
    lo = (r1 - mid.astype(F32)).astype(BF16)
    tri = tri_ref[...]
    cum = _dot(tri, hi) + _dot(tri, mid) + _dot(tri, lo)

    cum3 = cum.reshape(n_chunks, CHUNK, w)
    end3 = cum3[:, CHUNK - 1:CHUNK, :]
    p_inv = jnp.exp(-cum)
    p_end = jnp.exp(end3 - cum3).reshape(rows, w)
    b = kn * a
    rt_sc[...] = (r * jnp.exp(cum)).astype(BF16)
    kt_sc[...] = (km * p_inv).astype(BF16)
    bt_sc[...] = (b * p_inv).astype(BF16)
    kkt_sc[...] = (kn * jnp.exp(cum - lw)).astype(BF16)
    kh_sc[...] = (km * p_end).astype(BF16)
    bh_sc[...] = (b * p_end).astype(BF16)
    v_sc[...] = v.astype(BF16)
    pc_sc[...] = jnp.broadcast_to(jnp.exp(end3), pc_sc.shape)

    group = 4 if n_chunks % 4 == 0 else 1

    def prep_chunks(gi, carry):
        where = []
        for g in range(group):
            ci = gi * group + g
            rs = pl.ds(pl.multiple_of(ci * CHUNK, CHUNK), CHUNK)
            for p in range(n_pairs):
                where.append((ci, rs, p, slice(p * LANES, (p + 1) * LANES)))
        problems = [dict(rt=rt_sc[rs, sl], kt=kt_sc[rs, sl], bt=bt_sc[rs, sl], kkt=kkt_sc[rs, sl],
                         kh=kh_sc[rs, sl], bh=bh_sc[rs, sl], v=v_sc[rs, sl], pc=pc_sc[ci, 0:1, sl])
                    for ci, rs, p, sl in where]
        for (ci, rs, p, sl), (qphi, y0, delta) in zip(where, _wkv_prep(problems)):
            qphi_sc[ci, p] = qphi
            delta_sc[ci, p] = delta
            y_sc[rs, sl] = y0
        return carry

    lax.fori_loop(0, n_chunks // group, prep_chunks, 0)

    def scan_chunk(ci, carry):
        rs = pl.ds(pl.multiple_of(ci * CHUNK, CHUNK), CHUNK)
        for p in range(n_pairs):
            sl = slice(p * LANES, (p + 1) * LANES)
            res = _dot(qphi_sc[ci, p], hp_sc[p].astype(BF16))
            y_sc[rs, sl] += res[:CHUNK]
            hp_sc[p] = res[CHUNK:] + delta_sc[ci, p]
        return carry

    lax.fori_loop(0, n_chunks, scan_chunk, 0, unroll=True)

    y = y_sc[...]
    mean = _dot(y.astype(BF16), ones) * (1.0 / HEAD_DIM)
    yc = y - mean
    var = _dot((yc * yc).astype(BF16), ones) * (1.0 / HEAD_DIM)
    yn = yc * lax.rsqrt(var + GN_EPS) * lnw_ref[...] + lnb_ref[...]
    o_ref[0] = ((yn + bonus) * g).astype(o_ref.dtype)


def _rwkv(pr3, mu, w0, w2p, a0, a2p, g2, k_k, k_a, r_k, ln_w, ln_b, rows):
    batch, seq, _ = pr3.shape
    nb = seq // rows
    fixed = lambda b, i: (0, 0)
    vec = pl.BlockSpec((1, RWKV_WIDTH), fixed)
    lora = DECAY_LORA + AAA_LORA
    n_pairs = RWKV_WIDTH // LANES
    ci = jnp.arange(rows) // CHUNK
    ti = jnp.arange(rows)
    tri = ((ci[:, None] == ci[None, :]) & (ti[None, :] <= ti[:, None])).astype(BF16)
    big = pltpu.VMEM((rows, RWKV_WIDTH), F32)
    n_chunks = rows // CHUNK
    return pl.pallas_call(
        functools.partial(_rwkv_body, rows=rows),
        grid=(batch, nb),
        in_specs=[
            pl.BlockSpec((1, rows, RWKV_COLS), lambda b, i: (b, i, 0)),
            pl.BlockSpec((1, 8, RWKV_COLS), lambda b, i: (b, jnp.maximum(i * (rows // 8) - 1, 0), 0)),
            pl.BlockSpec((1, RWKV_COLS), fixed),
            vec,
            pl.BlockSpec((lora, RWKV_WIDTH), fixed),
            vec,
            pl.BlockSpec((lora, RWKV_WIDTH), fixed),
            pl.BlockSpec((GATE_LORA, RWKV_WIDTH), fixed),
            vec, vec, vec, vec, vec,
            pl.BlockSpec((RWKV_WIDTH, RWKV_WIDTH), fixed),
            pl.BlockSpec((rows, rows), fixed),
        ],
        out_specs=pl.BlockSpec((1, rows, RWKV_WIDTH), lambda b, i: (b, i, 0)),
        out_shape=jax.ShapeDtypeStruct((batch, seq, RWKV_WIDTH), F32),
        scratch_shapes=(
            [pltpu.VMEM((n_pairs, LANES, LANES), F32)]
            + [pltpu.VMEM((rows, RWKV_WIDTH), BF16)] * 7
            + [pltpu.VMEM((n_chunks, 8, RWKV_WIDTH), F32),
               pltpu.VMEM((n_chunks, n_pairs, CHUNK + LANES, LANES), BF16),
               pltpu.VMEM((n_chunks, n_pairs, LANES, LANES), F32),
               big]),
        compiler_params=pltpu.CompilerParams(
            dimension_semantics=("parallel", "arbitrary"), vmem_limit_bytes=VMEM_LIMIT),
        name="rwkv7",
    )(pr3, pr3, mu, w0, w2p, a0, a2p, g2, k_k, k_a, r_k, ln_w, ln_b,
      _group_ones(RWKV_WIDTH, HEAD_DIM), tri)


ROUTE_COLS = N_GROUPS + N_EXPERTS


def _merge_body(x_ref, gate_ref, orw_ref, odf_ref, wbr_ref, wbd_ref, wout_ref, nf_ref, wr_ref,
                br_ref, x1_ref, h2_ref, route_ref):
    gates = gate_ref[...].astype(F32)
    mixed = (gates[:, :D_MODEL] * _dot(orw_ref[...].astype(BF16), wbr_ref[...])
             + gates[:, D_MODEL:] * _dot(odf_ref[...].astype(BF16), wbd_ref[...]))
    x1 = x_ref[...] + _dot(mixed.astype(BF16), wout_ref[...])
    x1_ref[...] = x1
    ms = jnp.mean(x1 * x1, axis=-1, keepdims=True)
    h2 = x1 * lax.rsqrt(ms + RMS_EPS) * nf_ref[...]
    _store_token_tiles(h2_ref, 0, h2)

    hi = h2.astype(BF16)
    lo = (h2 - hi.astype(F32)).astype(BF16)
    acc = _dot(hi, wr_ref[...])
    logits = acc[:, :LANES] + acc[:, LANES:] + _dot(lo, wr_ref[:, :LANES]) + br_ref[...]
    lane = lax.broadcasted_iota(jnp.int32, logits.shape, 1).astype(F32)
    neg = -jnp.inf

    def top(vals):
        m = jnp.max(vals, axis=-1, keepdims=True)
        return m, jnp.min(jnp.where(vals == m, lane, float(LANES)), axis=-1, keepdims=True)

    gl = jnp.where(lane < N_GROUPS, logits, neg)
    gm, g_idx = top(gl)
    g_top = 1.0 / jnp.sum(jnp.exp(gl - gm), axis=-1, keepdims=True)
    first = N_GROUPS + EXPERTS_PER_GROUP * g_idx
    el = jnp.where((lane >= first) & (lane < first + EXPERTS_PER_GROUP), logits, neg)
    t1, i1 = top(el)
    t2, i2 = top(jnp.where(lane == i1, neg, el))
    e2 = jnp.exp(t2 - t1)
    w1 = g_top / (1.0 + e2)
    w2 = g_top * e2 / (1.0 + e2)
    route_ref[...] = jnp.where(lane == 0, i1 - N_GROUPS, jnp.where(lane == 1, i2 - N_GROUPS,
                               jnp.where(lane == 2, w1, jnp.where(lane == 3, w2, 0.0))))


def _merge(x2, gates, o_rwkv, o_diff, w_br, w_bd, w_out, norm_ffn, w_route, b_route, tm):
    t = x2.shape[0]
    row = lambda i: (i, 0)
    fixed = lambda i: (0, 0)
    return pl.pallas_call(
        _merge_body,
        grid=(t // tm,),
        in_specs=[
            pl.BlockSpec((tm, D_MODEL), row),
            pl.BlockSpec((tm, GATE_COLS), row),
            pl.BlockSpec((tm, RWKV_WIDTH), row),
            pl.BlockSpec((tm, DIFF_V_WIDTH), row),
            pl.BlockSpec((RWKV_WIDTH, D_MODEL), fixed),
            pl.BlockSpec((DIFF_V_WIDTH, D_MODEL), fixed),
            pl.BlockSpec((D_MODEL, D_MODEL), fixed),
            pl.BlockSpec((1, D_MODEL), fixed),
            pl.BlockSpec((D_MODEL, 2 * LANES), fixed),
            pl.BlockSpec((1, LANES), fixed),
        ],
        out_specs=[
            pl.BlockSpec((tm, D_MODEL), row),
            pl.BlockSpec((tm * TILE_ROWS, LANES), row),
            pl.BlockSpec((tm, LANES), row),
        ],
        out_shape=[
            jax.ShapeDtypeStruct((t, D_MODEL), F32),
            jax.ShapeDtypeStruct((t * TILE_ROWS, LANES), F32),
            jax.ShapeDtypeStruct((t, LANES), F32),
        ],
        compiler_params=pltpu.CompilerParams(
            dimension_semantics=("parallel",), vmem_limit_bytes=VMEM_LIMIT),
        name="merge_router",
    )(x2, gates, o_rwkv, o_diff, w_br, w_bd, w_out, norm_ffn, w_route, b_route)


def _expert_body(blk_a_ref, blk_b_ref, rows_ref, h2_hbm, wga_ref, wua_ref, wda_ref, wgb_ref, wub_ref,
                 wdb_ref, yb_ref, xbuf, wg_sc, wu_sc, wd_sc, gsem, *, n_steps):
    g = pl.program_id(0)

    def refresh(s, blk_ref, w_refs):
        changed = (g == 0) | (blk_ref[g] != blk_ref[jnp.maximum(g - 1, 0)])

        @pl.when(changed)
        def _():
            for cache, w_ref in zip((wg_sc, wu_sc, wd_sc), w_refs):
                cache[s] = w_ref[0].astype(BF16)

    refresh(0, blk_a_ref, (wga_ref, wua_ref, wda_ref))
    refresh(1, blk_b_ref, (wgb_ref, wub_ref, wdb_ref))

    def start_gather(block, buf):
        for r in range(ROW_BLOCK):
            tok = rows_ref[block * ROW_BLOCK + r]
            pltpu.make_async_copy(_tile(h2_hbm, tok), _tile(xbuf, buf * ROW_BLOCK + r),
                                  gsem.at[buf]).start(r % 2)

    def wait_gather(buf):
        for _ in range(ROW_BLOCK):
            pltpu.make_async_copy(_tile(h2_hbm, 0), _tile(xbuf, 0), gsem.at[buf]).wait()

    @pl.when(g == 0)
    def _():
        for buf in range(4):
            start_gather(buf, buf)

    for parity in range(2):
        @pl.when(g % 2 == parity)
        def _():
            for s in range(2):
                buf = 2 * parity + s
                wait_gather(buf)
                x = _load_token_tiles(xbuf, buf * ROW_BLOCK * TILE_ROWS, ROW_BLOCK).astype(BF16)
                start_gather(2 * g + s + 4, buf)
                gt = _dot(x, wg_sc[s])
                up = _dot(x, wu_sc[s])
                mid = (gt * jax.nn.sigmoid(gt) * up).astype(BF16)
                _store_token_tiles(yb_ref, s * ROW_BLOCK * TILE_ROWS, _dot(mid, wd_sc[s]))

    @pl.when(g == n_steps - 1)
    def _():
        for buf in range(4):
            wait_gather(buf)


def _experts(blk_e, rows, h2, w_gate, w_up, w_down):
    n_blocks = blk_e.shape[0]
    assert n_blocks % 2 == 0
    n_steps = n_blocks // 2
    wspec_a = lambda shape: pl.BlockSpec((1,) + shape, lambda g, ba, bb, rows: (ba[g], 0, 0))
    wspec_b = lambda shape: pl.BlockSpec((1,) + shape, lambda g, ba, bb, rows: (bb[g], 0, 0))
    grid_spec = pltpu.PrefetchScalarGridSpec(
        num_scalar_prefetch=3,
        grid=(n_steps,),
        in_specs=[
            pl.BlockSpec(memory_space=pl.ANY),
            wspec_a((D_MODEL, EXPERT_FF)), wspec_a((D_MODEL, EXPERT_FF)), wspec_a((EXPERT_FF, D_MODEL)),
            wspec_b((D_MODEL, EXPERT_FF)), wspec_b((D_MODEL, EXPERT_FF)), wspec_b((EXPERT_FF, D_MODEL)),
        ],
        out_specs=pl.BlockSpec((2 * ROW_BLOCK * TILE_ROWS, LANES), lambda g, ba, bb, rows: (g, 0)),
        scratch_shapes=[pltpu.VMEM((4 * ROW_BLOCK * TILE_ROWS, LANES), F32),
                        pltpu.VMEM((2, D_MODEL, EXPERT_FF), BF16), pltpu.VMEM((2, D_MODEL, EXPERT_FF), BF16),
                        pltpu.VMEM((2, EXPERT_FF, D_MODEL), BF16), pltpu.SemaphoreType.DMA((4,))],
    )
    return pl.pallas_call(
        functools.partial(_expert_body, n_steps=n_steps),
        grid_spec=grid_spec,
        out_shape=jax.ShapeDtypeStruct((n_blocks * ROW_BLOCK * TILE_ROWS, LANES), F32),
        compiler_params=pltpu.CompilerParams(
            dimension_semantics=("arbitrary",), vmem_limit_bytes=VMEM_LIMIT),
        name="experts",
    )(blk_e[0::2], blk_e[1::2], rows, h2, w_gate, w_up, w_down, w_gate, w_up, w_down)


def _final_body(pos_ref, x1_ref, route_ref, p_ref, yb_hbm, np_ref, wpg_ref, wpp_ref, o_ref,
                ybuf, sem, *, tm, n_steps):
    i = pl.program_id(0)
    per_tile = TOP_K * tm

    def start_gather(tile_idx, s):
        for r in range(per_tile):
            row = pos_ref[tile_idx * per_tile + r]
            pltpu.make_async_copy(_tile(yb_hbm, row), _tile(ybuf, s * per_tile + r), sem.at[s]).start(r % 2)

    def wait_gather(s):
        for _ in range(per_tile):
            pltpu.make_async_copy(_tile(yb_hbm, 0), _tile(ybuf, 0), sem.at[s]).wait()

    @pl.when(i == 0)
    def _():
        start_gather(0, 0)
        start_gather(1, 1)

    for s in range(2):
        rows = pl.ds(s * tm, tm)
        wait_gather(s)
        y_first = _load_token_tiles(ybuf, s * per_tile * TILE_ROWS, tm)
        y_second = _load_token_tiles(ybuf, (s * per_tile + tm) * TILE_ROWS, tm)
        start_gather(2 * i + s + 2, s)
        route = route_ref[rows, :]
        x2 = x1_ref[rows, :] + route[:, 2:3] * y_first + route[:, 3:4] * y_second
        ms = jnp.mean(x2 * x2, axis=-1, keepdims=True)
        hn = (x2 * lax.rsqrt(ms + RMS_EPS) * np_ref[...]).astype(BF16)
        gate = jax.nn.sigmoid(_dot(hn, wpg_ref[...]))
        o_ref[rows, :] = x2 + gate * _dot(p_ref[rows, :].astype(BF16), wpp_ref[...])

    @pl.when(i == n_steps - 1)
    def _():
        for s in range(2):
            wait_gather(s)


def _final(pos, x1, route, p2, yb, norm_ple, w_pg, w_pp, tm):
    t = x1.shape[0]
    assert t % (2 * tm) == 0
    n_steps = t // (2 * tm)
    row = lambda i, pos: (i, 0)
    fixed = lambda i, pos: (0, 0)
    grid_spec = pltpu.PrefetchScalarGridSpec(
        num_scalar_prefetch=1,
        grid=(n_steps,),
        in_specs=[
            pl.BlockSpec((2 * tm, D_MODEL), row),
            pl.BlockSpec((2 * tm, LANES), row),
            pl.BlockSpec((2 * tm, PLE_DIM), row),
            pl.BlockSpec(memory_space=pl.ANY),
            pl.BlockSpec((1, D_MODEL), fixed),
            pl.BlockSpec((D_MODEL, D_MODEL), fixed),
            pl.BlockSpec((PLE_DIM, D_MODEL), fixed),
        ],
        out_specs=pl.BlockSpec((2 * tm, D_MODEL), row),
        scratch_shapes=[pltpu.VMEM((2 * TOP_K * tm * TILE_ROWS, LANES), F32), pltpu.SemaphoreType.DMA((2,))],
    )
    return pl.pallas_call(
        functools.partial(_final_body, tm=tm, n_steps=n_steps),
        grid_spec=grid_spec,
        out_shape=jax.ShapeDtypeStruct((t, D_MODEL), F32),
        compiler_params=pltpu.CompilerParams(
            dimension_semantics=("arbitrary",), vmem_limit_bytes=VMEM_LIMIT),
        name="combine_ple",
    )(pos, x1, route, p2, yb, norm_ple, w_pg, w_pp)


def _dispatch_plan(route, n_tokens, tm):
    eid = route[:, :TOP_K].astype(jnp.int32).reshape(-1)
    n_assign = n_tokens * TOP_K
    order = jnp.argsort(eid).astype(jnp.int32)
    rank = jnp.argsort(order).astype(jnp.int32)
    experts = jnp.arange(N_EXPERTS, dtype=jnp.int32)
    onehot = eid[:, None] == experts[None, :]
    counts = jnp.sum(onehot.astype(jnp.int32), axis=0)
    starts = jnp.cumsum(counts) - counts
    pcounts = ((counts + ROW_BLOCK - 1) // ROW_BLOCK) * ROW_BLOCK
    pends = jnp.cumsum(pcounts)
    pstarts = pends - pcounts
    n_blocks = (n_assign + N_EXPERTS * ROW_BLOCK) // ROW_BLOCK
    row0 = jnp.arange(n_blocks, dtype=jnp.int32) * ROW_BLOCK
    blk_e = jnp.minimum(jnp.sum((pends[None, :] <= row0[:, None]).astype(jnp.int32), axis=1), N_EXPERTS - 1)
    offset = row0 - pstarts[blk_e]
    valid = counts[blk_e] - offset
    first = jnp.clip(starts[blk_e] + offset, 0, n_assign - 1)
    r = jnp.arange(ROW_BLOCK, dtype=jnp.int32)[None, :]
    idx = jnp.minimum(first[:, None] + r, n_assign - 1)
    rows = jnp.where(r < valid[:, None], order[idx] // TOP_K, 0).reshape(-1)
    rows = jnp.concatenate([rows, jnp.zeros((4 * ROW_BLOCK,), jnp.int32)])
    shift = jnp.sum(jnp.where(onehot, (pstarts - starts)[None, :], 0), axis=1)
    pos = (rank + shift).reshape(n_tokens // tm, tm, TOP_K).transpose(0, 2, 1).reshape(-1)
    pos = jnp.concatenate([pos, jnp.zeros((2 * TOP_K * tm,), jnp.int32)])
    return blk_e, rows, pos


def _rope_tables(seq):
    inv = ROPE_THETA ** (-jnp.arange(0, HEAD_DIM, 2, dtype=F32) / HEAD_DIM)
    ang = jnp.arange(seq, dtype=F32)[:, None] * inv[None, :]
    ang = jnp.concatenate([ang, ang], axis=-1)
    reps = DIFF_QK_WIDTH // HEAD_DIM
    return jnp.tile(jnp.cos(ang), (1, reps)), jnp.tile(jnp.sin(ang), (1, reps))


def _tile_plan(seq):
    return dict(
        proj=min(512, seq),
        combine=256,
        rwkv=min(512, seq),
        attn=min(256, seq),
    )


def _layer(x2, p2, batch, seq, lam_init, prm):
    t = x2.shape[0]
    tiles = _tile_plan(seq)
    tm = tiles['proj']
    cos, sin = _rope_tables(seq)
    reps = DIFF_QK_WIDTH // HEAD_DIM
    prw, q, k, v, gates = _inproj(
        x2, prm['norm_mix'][None], prm['w_in'].astype(BF16), cos, sin,
        jnp.tile(prm['q_norm'], reps)[None], jnp.tile(prm['k_norm'], reps)[None], seq, tm)

    zpad = jnp.zeros((DECAY_LORA, RWKV_WIDTH), F32)
    w2p = jnp.concatenate([prm['rwkv_w2'], zpad], axis=0).astype(BF16)
    a2p = jnp.concatenate([zpad, prm['rwkv_a2']], axis=0).astype(BF16)
    o_rwkv = _rwkv(
        prw.reshape(batch, seq, RWKV_COLS), prm['rwkv_mu'][None], prm['rwkv_w0'][None], w2p,
        prm['rwkv_a0'][None], a2p, prm['rwkv_g2'].astype(BF16), prm['rwkv_k_k'][None],
        prm['rwkv_k_a'][None], prm['rwkv_r_k'].reshape(1, RWKV_WIDTH), prm['rwkv_ln_w'][None],
        prm['rwkv_ln_b'][None], tiles['rwkv']).reshape(t, RWKV_WIDTH)

    o_diff = _diff_attention(
        q, k, v, prm['lambda_q1'][None], prm['lambda_k1'][None], prm['lambda_q2'][None],
        prm['lambda_k2'][None], prm['subln_w'][None], batch, seq, lam_init, tiles['attn'])

    w_r = jnp.concatenate([prm['w_group'], prm['w_expert_router']], axis=1)
    w_r = jnp.pad(w_r, ((0, 0), (0, LANES - ROUTE_COLS)))
    w_r_hi = w_r.astype(BF16)
    w_r_lo = (w_r - w_r_hi.astype(F32)).astype(BF16)
    b_r = jnp.pad(jnp.concatenate([prm['b_group'], prm['b_expert_router']]), (0, LANES - ROUTE_COLS))[None]
    x1, h2, route = _merge(
        x2, gates, o_rwkv, o_diff, prm['w_branch_rwkv'].astype(BF16), prm['w_branch_diff'].astype(BF16),
        prm['w_out'].astype(BF16), prm['norm_ffn'][None], jnp.concatenate([w_r_hi, w_r_lo], axis=1), b_r, tm)

    blk_e, rows, pos = _dispatch_plan(route, t, tiles['combine'])
    yb = _experts(blk_e, rows, h2, prm['w_gate'], prm['w_up'], prm['w_down'])
    return _final(pos, x1, route, p2, yb, prm['norm_ple'][None], prm['w_ple_gate'].astype(BF16),
                  prm['w_ple_proj'].astype(BF16), tiles['combine'])


_PARAM_NAMES = (
    'norm_mix', 'w_in', 'rwkv_mu', 'rwkv_w0', 'rwkv_w2', 'rwkv_a0', 'rwkv_a2', 'rwkv_g2', 'rwkv_k_k',
    'rwkv_k_a', 'rwkv_r_k', 'rwkv_ln_w', 'rwkv_ln_b', 'q_norm', 'k_norm', 'lambda_q1', 'lambda_k1',
    'lambda_q2', 'lambda_k2', 'subln_w', 'w_branch_rwkv', 'w_branch_diff', 'w_out', 'norm_ffn',
    'w_group', 'b_group', 'w_expert_router', 'b_expert_router', 'w_gate', 'w_up', 'w_down', 'norm_ple',
    'w_ple_gate', 'w_ple_proj')


def kernel(x, p, norm_mix, w_in, rwkv_mu, rwkv_w0, rwkv_w2, rwkv_a0, rwkv_a2, rwkv_g2, rwkv_k_k,
           rwkv_k_a, rwkv_r_k, rwkv_ln_w, rwkv_ln_b, q_norm, k_norm, lambda_q1, lambda_k1, lambda_q2,
           lambda_k2, subln_w, w_branch_rwkv, w_branch_diff, w_out, norm_ffn, w_group, b_group,
           w_expert_router, b_expert_router, w_gate, w_up, w_down, norm_ple, w_ple_gate, w_ple_proj):
    stacked = dict(zip(_PARAM_NAMES, (
        norm_mix, w_in, rwkv_mu, rwkv_w0, rwkv_w2, rwkv_a0, rwkv_a2, rwkv_g2, rwkv_k_k, rwkv_k_a,
        rwkv_r_k, rwkv_ln_w, rwkv_ln_b, q_norm, k_norm, lambda_q1, lambda_k1, lambda_q2, lambda_k2,
        subln_w, w_branch_rwkv, w_branch_diff, w_out, norm_ffn, w_group, b_group, w_expert_router,
        b_expert_router, w_gate, w_up, w_down, norm_ple, w_ple_gate, w_ple_proj)))
    batch, seq, _ = x.shape
    depth = p.shape[0]
    x2 = x.reshape(batch * seq, D_MODEL)
    for layer in range(depth):
        lam_init = 0.8 - 0.6 * math.exp(-0.3 * layer)
        prm = {name: val[layer] for name, val in stacked.items()}
        x2 = _layer(x2, p[layer].reshape(batch * seq, PLE_DIM), batch, seq, lam_init, prm)
    return x2.reshape(batch, seq, D_MODEL)
```

```python
import functools
import math

import jax
import jax.numpy as jnp
from jax import lax
from jax.experimental import pallas as pl
from jax.experimental.pallas import tpu as pltpu

F32 = jnp.float32
BF16 = jnp.bfloat16

D_MODEL = 1024
PLE_DIM = 256
RMS_EPS = 1e-6

RWKV_HEADS = 8
HEAD_DIM = 64
RWKV_WIDTH = RWKV_HEADS * HEAD_DIM
DECAY_LORA = 64
AAA_LORA = 64
GATE_LORA = 128
GN_EPS = 64e-5
RWKV_COLS = 3 * RWKV_WIDTH + DECAY_LORA + AAA_LORA + GATE_LORA

DIFF_HEADS = 4
DIFF_QK_WIDTH = DIFF_HEADS * 2 * HEAD_DIM
DIFF_V_DIM = 2 * HEAD_DIM
DIFF_V_WIDTH = DIFF_HEADS * DIFF_V_DIM
DIFF_COLS = 2 * DIFF_QK_WIDTH + DIFF_V_WIDTH
ROPE_THETA = 10000.0
GATE_COLS = 2 * D_MODEL

N_GROUPS = 4
EXPERTS_PER_GROUP = 8
N_EXPERTS = N_GROUPS * EXPERTS_PER_GROUP
TOP_K = 2
EXPERT_FF = 512
ROW_BLOCK = 128

LANES = 128
CHUNK = 64
VMEM_LIMIT = 56 * 1024 * 1024


def _dot(a, b):
    return jnp.dot(a, b, preferred_element_type=F32)


def _dot_nt(a, b):
    return lax.dot_general(a, b, (((1,), (1,)), ((), ())), preferred_element_type=F32)


SUBLANES = 8
TILE_ROWS = D_MODEL // LANES
assert TILE_ROWS == SUBLANES


def _store_token_tiles(ref, base, val, pitch=TILE_ROWS):
    n = val.shape[0]
    for j in range(TILE_ROWS):
        ref[pl.ds(base + j, n, stride=pitch), :] = val[:, j * LANES:(j + 1) * LANES]


def _load_token_tiles(ref, base, n, pitch=TILE_ROWS):
    return jnp.concatenate([ref[pl.ds(base + j, n, stride=pitch), :] for j in range(TILE_ROWS)], axis=1)


def _tile(ref, index):
    return ref.at[pl.ds(pl.multiple_of(index * TILE_ROWS, TILE_ROWS), TILE_ROWS)]


def _group_ones(width, group):
    i = jnp.arange(width) // group
    return (i[:, None] == i[None, :]).astype(BF16)


def _rotate_half(t):
    width = t.shape[-1]
    lane = lax.broadcasted_iota(jnp.int32, t.shape, 1)
    fwd = pltpu.roll(t, width - HEAD_DIM // 2, 1)
    bwd = pltpu.roll(t, HEAD_DIM // 2, 1)
    return jnp.where(lane % HEAD_DIM < HEAD_DIM // 2, -fwd, bwd)


def _inproj_body(x_ref, g_ref, w_ref, cos_ref, sin_ref, qg_ref, kg_ref, ones_ref,
                 prw_ref, q_ref, k_ref, v_ref, gate_ref):
    x = x_ref[...]
    ms = jnp.mean(x * x, axis=-1, keepdims=True)
    h = (x * lax.rsqrt(ms + RMS_EPS) * g_ref[...]).astype(BF16)
    prw_ref[...] = _dot(h, w_ref[:, :RWKV_COLS])
    pd = _dot(h, w_ref[:, RWKV_COLS:RWKV_COLS + DIFF_COLS])
    cos = cos_ref[...]
    sin = sin_ref[...]

    def qk_prep(t, gain, scale):
        ssq = _dot((t * t).astype(BF16), ones_ref[...])
        t = t * lax.rsqrt(ssq * (1.0 / HEAD_DIM) + RMS_EPS) * gain
        return ((t * cos + _rotate_half(t) * sin) * scale).astype(BF16)

    q_ref[...] = qk_prep(pd[:, :DIFF_QK_WIDTH], qg_ref[...], HEAD_DIM ** -0.5 * math.log2(math.e))
    k_ref[...] = qk_prep(pd[:, DIFF_QK_WIDTH:2 * DIFF_QK_WIDTH], kg_ref[...], 1.0)
    v_ref[...] = pd[:, 2 * DIFF_QK_WIDTH:].astype(BF16)
    gate_ref[...] = jax.nn.sigmoid(_dot(h, w_ref[:, RWKV_COLS + DIFF_COLS:])).astype(BF16)


def _inproj(x2, norm_g, w_in, cos, sin, q_gain, k_gain, seq, tm):
    t = x2.shape[0]
    n_seq_tiles = seq // tm
    row = lambda i: (i, 0)
    fixed = lambda i: (0, 0)
    rope = lambda i: (i % n_seq_tiles, 0)
    return pl.pallas_call(
        _inproj_body,
        grid=(t // tm,),
        in_specs=[
            pl.BlockSpec((tm, D_MODEL), row),
            pl.BlockSpec((1, D_MODEL), fixed),
            pl.BlockSpec((D_MODEL, RWKV_COLS + DIFF_COLS + GATE_COLS), fixed),
            pl.BlockSpec((tm, DIFF_QK_WIDTH), rope),
            pl.BlockSpec((tm, DIFF_QK_WIDTH), rope),
            pl.BlockSpec((1, DIFF_QK_WIDTH), fixed),
            pl.BlockSpec((1, DIFF_QK_WIDTH), fixed),
            pl.BlockSpec((DIFF_QK_WIDTH, DIFF_QK_WIDTH), fixed),
        ],
        out_specs=[
            pl.BlockSpec((tm, RWKV_COLS), row),
            pl.BlockSpec((tm, DIFF_QK_WIDTH), row),
            pl.BlockSpec((tm, DIFF_QK_WIDTH), row),
            pl.BlockSpec((tm, DIFF_V_WIDTH), row),
            pl.BlockSpec((tm, GATE_COLS), row),
        ],
        out_shape=[
            jax.ShapeDtypeStruct((t, RWKV_COLS), F32),
            jax.ShapeDtypeStruct((t, DIFF_QK_WIDTH), BF16),
            jax.ShapeDtypeStruct((t, DIFF_QK_WIDTH), BF16),
            jax.ShapeDtypeStruct((t, DIFF_V_WIDTH), BF16),
            jax.ShapeDtypeStruct((t, GATE_COLS), BF16),
        ],
        compiler_params=pltpu.CompilerParams(
            dimension_semantics=("parallel",), vmem_limit_bytes=VMEM_LIMIT),
        name="inproj",
    )(x2, norm_g, w_in, cos, sin, q_gain, k_gain, _group_ones(DIFF_QK_WIDTH, HEAD_DIM))


def _attn_body(q_ref, k_ref, v_ref, lq1_ref, lk1_ref, lq2_ref, lk2_ref, sw_ref, o_ref,
               qs_sc, m_sc, l_sc, acc_sc, s0_sc, s1_sc, *, tq, wide, lam_init):
    qi = pl.program_id(2)
    heads = range(q_ref.shape[1] // LANES)
    cols = [slice(h * LANES, (h + 1) * LANES) for h in heads]
    for h in heads:
        q = q_ref[:, cols[h]]
        lane = lax.broadcasted_iota(jnp.int32, q.shape, 1)
        qs_sc[h, :tq, :] = jnp.where(lane < HEAD_DIM, q, jnp.zeros_like(q))
        qs_sc[h, tq:, :] = jnp.where(lane >= HEAD_DIM, q, jnp.zeros_like(q))
    m_sc[...] = jnp.full(m_sc.shape, -jnp.inf, F32)
    l_sc[...] = jnp.zeros(l_sc.shape, F32)
    acc_sc[...] = jnp.zeros(acc_sc.shape, F32)

    width = wide * tq

    def scores(j, s_sc):
        for h in heads:
            kb = k_ref[pl.ds(pl.multiple_of(j * width, width), width), cols[h]]
            s_sc[h] = _dot_nt(kb, qs_sc[h])

    def update(j, s_sc, masked):
        for h in heads:
            s = s_sc[h]
            if masked:
                key = lax.broadcasted_iota(jnp.int32, s.shape, 0) + j * width
                query = lax.broadcasted_iota(jnp.int32, s.shape, 1) % tq + qi * tq
                s = jnp.where(key <= query, s, -jnp.inf)
            vb = v_ref[pl.ds(pl.multiple_of(j * width, width), width), cols[h]]
            m_prev = m_sc[h]
            m_new = jnp.maximum(m_prev, jnp.max(s, axis=0, keepdims=True))
            alpha = jnp.exp2(m_prev - m_new)
            p = jnp.exp2(s - m_new)
            l_sc[h] = alpha * l_sc[h] + jnp.sum(p, axis=0, keepdims=True)
            acc_sc[h] = alpha * acc_sc[h] + _dot_tn(vb, p.astype(BF16))
            m_sc[h] = m_new

    last = qi // wide
    pairs = last // 2
    scores(0, s0_sc)

    def body(jj, carry):
        scores(2 * jj + 1, s1_sc)
        update(2 * jj, s0_sc, False)
        scores(2 * jj + 2, s0_sc)
        update(2 * jj + 1, s1_sc, False)
        return carry

    lax.fori_loop(0, pairs, body, 0)

    @pl.when(last % 2 == 1)
    def _():
        scores(last, s1_sc)
        update(last - 1, s0_sc, False)
        update(last, s1_sc, True)

    @pl.when(last % 2 == 0)
    def _():
        update(last, s0_sc, True)

    lam = (jnp.exp(jnp.sum(lq1_ref[...] * lk1_ref[...], axis=-1, keepdims=True))
           - jnp.exp(jnp.sum(lq2_ref[...] * lk2_ref[...], axis=-1, keepdims=True)) + lam_init)
    for h in heads:
        o = acc_sc[h] / l_sc[h]
        o = o[:, :tq] - lam * o[:, tq:]
        ms = jnp.mean(o * o, axis=0, keepdims=True)
        o = (o * lax.rsqrt(ms + RMS_EPS)).T
        o_ref[:, cols[h]] = o * sw_ref[...] * (1.0 - lam_init)


def _diff_attention(q, k, v, lq1, lk1, lq2, lk2, subln_w, batch, seq, lam_init, tq):
    t = q.shape[0]
    nq = seq // tq
    vec = pl.BlockSpec((1, HEAD_DIM), lambda b, h, i: (0, 0))
    wide = max(1, min(4, nq // 2))
    assert nq % wide == 0
    hpg = 2
    return pl.pallas_call(
        functools.partial(_attn_body, tq=tq, wide=wide, lam_init=lam_init),
        grid=(batch, DIFF_HEADS // hpg, nq),
        in_specs=[
            pl.BlockSpec((tq, hpg * LANES), lambda b, h, i: (b * nq + i, h)),
            pl.BlockSpec((seq, hpg * LANES), lambda b, h, i: (b, h)),
            pl.BlockSpec((seq, hpg * LANES), lambda b, h, i: (b, h)),
            vec, vec, vec, vec,
            pl.BlockSpec((1, DIFF_V_DIM), lambda b, h, i: (0, 0)),
        ],
        out_specs=pl.BlockSpec((tq, hpg * LANES), lambda b, h, i: (b * nq + i, h)),
        out_shape=jax.ShapeDtypeStruct((t, DIFF_V_WIDTH), F32),
        scratch_shapes=[
            pltpu.VMEM((hpg, 2 * tq, LANES), BF16),
            pltpu.VMEM((hpg, 1, 2 * tq), F32),
            pltpu.VMEM((hpg, 1, 2 * tq), F32),
            pltpu.VMEM((hpg, DIFF_V_DIM, 2 * tq), F32),
            pltpu.VMEM((hpg, wide * tq, 2 * tq), F32),
            pltpu.VMEM((hpg, wide * tq, 2 * tq), F32),
        ],
        compiler_params=pltpu.CompilerParams(
            dimension_semantics=("parallel", "parallel", "arbitrary"),
            vmem_limit_bytes=VMEM_LIMIT),
        name="diff_attn",
    )(q, k, v, lq1, lk1, lq2, lk2, subln_w)


def _dot_tn(a, b):
    return lax.dot_general(a, b, (((0,), (0,)), ((), ())), preferred_element_type=F32)


def _wkv_prep(problems):
    c = CHUNK
    lane = lax.broadcasted_iota(jnp.int32, (c, LANES), 1)
    m0 = lane < HEAD_DIM
    row = lax.broadcasted_iota(jnp.int32, (LANES, LANES), 0)
    col = lax.broadcasted_iota(jnp.int32, (LANES, LANES), 1)
    same = (row // c) == (col // c)
    strict = same & ((col % c) < (row % c))
    incl = same & ((col % c) <= (row % c))
    eye = row == col

    def stack(t):
        zero = jnp.zeros_like(t)
        return jnp.concatenate([jnp.where(m0, t, zero), jnp.where(m0, zero, t)], axis=0)

    def dup(t):
        return jnp.concatenate([t, t], axis=0)

    def fold(t):
        return t[:c] + t[c:]

    n = len(problems)
    xs = [_dot_nt(jnp.concatenate([q['kkt'], q['rt']], axis=0),
                  jnp.concatenate([stack(q['bt']), stack(q['kt'])], axis=0)) for q in problems]
    l_pow = [jnp.where(strict, dup(x[:c, :LANES]), 0.0) for x in xs]
    m_ak = [jnp.where(strict, dup(x[:c, LANES:]), 0.0).astype(BF16) for x in xs]
    a_rbk = [jnp.concatenate([jnp.where(incl, dup(x[c:, :LANES]), 0.0),
                              jnp.where(incl, dup(x[c:, LANES:]), 0.0)], axis=1).astype(BF16) for x in xs]
    t_inv = [jnp.where(eye, 1.0, 0.0) - l for l in l_pow]
    for _ in range(5):
        l16 = [l.astype(BF16) for l in l_pow]
        l_pow = [_dot(l, l) for l in l16]
        upd = [_dot(t.astype(BF16), l.astype(BF16)) for t, l in zip(t_inv, l_pow)]
        t_inv = [t + u for t, u in zip(t_inv, upd)]

    vs = [stack(q['v']) for q in problems]
    mv = [_dot(m, v) for m, v in zip(m_ak, vs)]
    z = [_dot(t.astype(BF16), jnp.concatenate([stack(q['kkt']), m.astype(BF16)], axis=1))
         for t, q, m in zip(t_inv, problems, mv)]
    zero_sq = jnp.zeros((LANES, LANES), BF16)
    qy = [_dot(a, jnp.concatenate([(-zz).astype(BF16), jnp.concatenate([zero_sq, v], axis=1)], axis=0))
          for a, zz, v in zip(a_rbk, z, vs)]
    zero_c = jnp.zeros((c, LANES), BF16)
    pd = [_dot_tn(jnp.concatenate([q['bh'], q['kh']], axis=0),
                  jnp.concatenate([jnp.concatenate([fold(-zz[:, :LANES]), fold(-zz[:, LANES:])], axis=1).astype(BF16),
                                   jnp.concatenate([zero_c, q['v']], axis=1)], axis=0))
          for q, zz in zip(problems, z)]
    out = []
    for i in range(n):
        q = problems[i]
        qs = stack(q['rt']).astype(F32) + qy[i][:, :LANES]
        phi = (jnp.where(eye, jnp.broadcast_to(q['pc'], (LANES, LANES)), 0.0)
               + jnp.where(same, pd[i][:, :LANES], 0.0))
        delta = jnp.where(same, pd[i][:, LANES:], 0.0)
        qphi = jnp.concatenate([fold(qs), phi], axis=0).astype(BF16)
        out.append((qphi, fold(qy[i][:, LANES:]), delta))
    return out


def _rwkv_body(pr_ref, halo_ref, mu_ref, w0_ref, w2_ref, a0_ref, a2_ref, g2_ref, kk_ref, ka_ref,
               rk_ref, lnw_ref, lnb_ref, ones_ref, tri_ref, o_ref,
               hp_sc, rt_sc, kt_sc, bt_sc, kkt_sc, kh_sc, bh_sc, v_sc, pc_sc, qphi_sc, delta_sc, y_sc,
               *, rows):
    i = pl.program_id(1)
    n_pairs = RWKV_WIDTH // LANES
    n_chunks = rows // CHUNK

    @pl.when(i == 0)
    def _():
        hp_sc[...] = jnp.zeros(hp_sc.shape, F32)

    pr = pr_ref[0]
    first = jnp.where(i == 0, 0.0, halo_ref[0, 7:8, :])
    prev = jnp.concatenate([first, pr[:-1]], axis=0)
    xs = pr + (prev - pr) * mu_ref[...]
    w = RWKV_WIDTH
    r = xs[:, :w]
    k = xs[:, w:2 * w]
    v = xs[:, 2 * w:3 * w]
    x_lora = xs[:, 3 * w:3 * w + DECAY_LORA + AAA_LORA]
    xg = xs[:, 3 * w + DECAY_LORA + AAA_LORA:]
    wl = w0_ref[...] + _dot(jnp.tanh(x_lora).astype(BF16), w2_ref[...])
    sp = jnp.maximum(-wl, 0.0) + jnp.log(1.0 + jnp.exp(-jnp.abs(wl)))
    lw = -jnp.exp(-sp - 0.5)
    a = jax.nn.sigmoid(a0_ref[...] + _dot(x_lora.astype(BF16), a2_ref[...]))
    g = _dot(jax.nn.sigmoid(xg).astype(BF16), g2_ref[...])
    ones = ones_ref[...]
    kx = k * kk_ref[...]
    ssq = _dot((kx * kx).astype(BF16), ones)
    kn = kx / jnp.maximum(jnp.sqrt(ssq), 1e-12)
    km = k * (1.0 + (a - 1.0) * ka_ref[...])
    bonus = _dot((r * km * rk_ref[...]).astype(BF16), ones) * v

    hi = lw.astype(BF16)
    r1 = lw - hi.astype(F32)
    mid = r1.astype(BF16)
    lo = (r1 - mid.astype(F32)).astype(BF16)
    tri = tri_ref[...]
    cum = _dot(tri, hi) + _dot(tri, mid) + _dot(tri, lo)

    cum3 = cum.reshape(n_chunks, CHUNK, w)
    end3 = cum3[:, CHUNK - 1:CHUNK, :]
    p_inv = jnp.exp(-cum)
    p_end = jnp.exp(end3 - cum3).reshape(rows, w)
    b = kn * a
    rt_sc[...] = (r * jnp.exp(cum)).astype(BF16)
    kt_sc[...] = (km * p_inv).astype(BF16)
    bt_sc[...] = (b * p_inv).astype(BF16)
    kkt_sc[...] = (kn * jnp.exp(cum - lw)).astype(BF16)
    kh_sc[...] = (km * p_end).astype(BF16)
    bh_sc[...] = (b * p_end).astype(BF16)
    v_sc[...] = v.astype(BF16)
    pc_sc[...] = jnp.broadcast_to(jnp.exp(end3), pc_sc.shape)

    group = 4 if n_chunks % 4 == 0 else 1

    def prep_chunks(gi, carry):
        where = []
        for g in range(group):
            ci = gi * group + g
            rs = pl.ds(pl.multiple_of(ci * CHUNK, CHUNK), CHUNK)
            for p in range(n_pairs):
                where.append((ci, rs, p, slice(p * LANES, (p + 1) * LANES)))
        problems = [dict(rt=rt_sc[rs, sl], kt=kt_sc[rs, sl], bt=bt_sc[rs, sl], kkt=kkt_sc[rs, sl],
                         kh=kh_sc[rs, sl], bh=bh_sc[rs, sl], v=v_sc[rs, sl], pc=pc_sc[ci, 0:1, sl])
                    for ci, rs, p, sl in where]
        for (ci, rs, p, sl), (qphi, y0, delta) in zip(where, _wkv_prep(problems)):
            qphi_sc[ci, p] = qphi
            delta_sc[ci, p] = delta
            y_sc[rs, sl] = y0
        return carry

    lax.fori_loop(0, n_chunks // group, prep_chunks, 0)

    def scan_chunk(ci, carry):
        rs = pl.ds(pl.multiple_of(ci * CHUNK, CHUNK), CHUNK)
        for p in range(n_pairs):
            sl = slice(p * LANES, (p + 1) * LANES)
            res = _dot(qphi_sc[ci, p], hp_sc[p].astype(BF16))
            y_sc[rs, sl] += res[:CHUNK]
            hp_sc[p] = res[CHUNK:] + delta_sc[ci, p]
        return carry

    lax.fori_loop(0, n_chunks, scan_chunk, 0, unroll=True)

    y = y_sc[...]
    mean = _dot(y.astype(BF16), ones) * (1.0 / HEAD_DIM)
    yc = y - mean
    var = _dot((yc * yc).astype(BF16), ones) * (1.0 / HEAD_DIM)
    yn = yc * lax.rsqrt(var + GN_EPS) * lnw_ref[...] + lnb_ref[...]
    o_ref[0] = ((yn + bonus) * g).astype(o_ref.dtype)


def _rwkv(pr3, mu, w0, w2p, a0, a2p, g2, k_k, k_a, r_k, ln_w, ln_b, rows):
    batch, seq, _ = pr3.shape
    nb = seq // rows
    fixed = lambda b, i: (0, 0)
    vec = pl.BlockSpec((1, RWKV_WIDTH), fixed)
    lora = DECAY_LORA + AAA_LORA
    n_pairs = RWKV_WIDTH // LANES
    ci = jnp.arange(rows) // CHUNK
    ti = jnp.arange(rows)
    tri = ((ci[:, None] == ci[None, :]) & (ti[None, :] <= ti[:, None])).astype(BF16)
    big = pltpu.VMEM((rows, RWKV_WIDTH), F32)
    n_chunks = rows // CHUNK
    return pl.pallas_call(
        functools.partial(_rwkv_body, rows=rows),
        grid=(batch, nb),
        in_specs=[
            pl.BlockSpec((1, rows, RWKV_COLS), lambda b, i: (b, i, 0)),
            pl.BlockSpec((1, 8, RWKV_COLS), lambda b, i: (b, jnp.maximum(i * (rows // 8) - 1, 0), 0)),
            pl.BlockSpec((1, RWKV_COLS), fixed),
            vec,
            pl.BlockSpec((lora, RWKV_WIDTH), fixed),
            vec,
            pl.BlockSpec((lora, RWKV_WIDTH), fixed),
            pl.BlockSpec((GATE_LORA, RWKV_WIDTH), fixed),
            vec, vec, vec, vec, vec,
            pl.BlockSpec((RWKV_WIDTH, RWKV_WIDTH), fixed),
            pl.BlockSpec((rows, rows), fixed),
        ],
        out_specs=pl.BlockSpec((1, rows, RWKV_WIDTH), lambda b, i: (b, i, 0)),
        out_shape=jax.ShapeDtypeStruct((batch, seq, RWKV_WIDTH), F32),
        scratch_shapes=(
            [pltpu.VMEM((n_pairs, LANES, LANES), F32)]
            + [pltpu.VMEM((rows, RWKV_WIDTH), BF16)] * 7
            + [pltpu.VMEM((n_chunks, 8, RWKV_WIDTH), F32),
               pltpu.VMEM((n_chunks, n_pairs, CHUNK + LANES, LANES), BF16),
               pltpu.VMEM((n_chunks, n_pairs, LANES, LANES), F32),
               big]),
        compiler_params=pltpu.CompilerParams(
            dimension_semantics=("parallel", "arbitrary"), vmem_limit_bytes=VMEM_LIMIT),
        name="rwkv7",
    )(pr3, pr3, mu, w0, w2p, a0, a2p, g2, k_k, k_a, r_k, ln_w, ln_b,
      _group_ones(RWKV_WIDTH, HEAD_DIM), tri)


ROUTE_COLS = N_GROUPS + N_EXPERTS


def _merge_body(x_ref, gate_ref, orw_ref, odf_ref, wbr_ref, wbd_ref, wout_ref, nf_ref, wr_ref,
                br_ref, x1_ref, h2_ref, route_ref):
    gates = gate_ref[...].astype(F32)
    mixed = (gates[:, :D_MODEL] * _dot(orw_ref[...].astype(BF16), wbr_ref[...])
             + gates[:, D_MODEL:] * _dot(odf_ref[...].astype(BF16), wbd_ref[...]))
    x1 = x_ref[...] + _dot(mixed.astype(BF16), wout_ref[...])
    x1_ref[...] = x1
    ms = jnp.mean(x1 * x1, axis=-1, keepdims=True)
    h2 = x1 * lax.rsqrt(ms + RMS_EPS) * nf_ref[...]
    _store_token_tiles(h2_ref, 0, h2)

    hi = h2.astype(BF16)
    lo = (h2 - hi.astype(F32)).astype(BF16)
    acc = _dot(hi, wr_ref[...])
    logits = acc[:, :LANES] + acc[:, LANES:] + _dot(lo, wr_ref[:, :LANES]) + br_ref[...]
    lane = lax.broadcasted_iota(jnp.int32, logits.shape, 1).astype(F32)
    neg = -jnp.inf

    def top(vals):
        m = jnp.max(vals, axis=-1, keepdims=True)
        return m, jnp.min(jnp.where(vals == m, lane, float(LANES)), axis=-1, keepdims=True)

    gl = jnp.where(lane < N_GROUPS, logits, neg)
    gm, g_idx = top(gl)
    g_top = 1.0 / jnp.sum(jnp.exp(gl - gm), axis=-1, keepdims=True)
    first = N_GROUPS + EXPERTS_PER_GROUP * g_idx
    el = jnp.where((lane >= first) & (lane < first + EXPERTS_PER_GROUP), logits, neg)
    t1, i1 = top(el)
    t2, i2 = top(jnp.where(lane == i1, neg, el))
    e2 = jnp.exp(t2 - t1)
    w1 = g_top / (1.0 + e2)
    w2 = g_top * e2 / (1.0 + e2)
    route_ref[...] = jnp.where(lane == 0, i1 - N_GROUPS, jnp.where(lane == 1, i2 - N_GROUPS,
                               jnp.where(lane == 2, w1, jnp.where(lane == 3, w2, 0.0))))


def _merge(x2, gates, o_rwkv, o_diff, w_br, w_bd, w_out, norm_ffn, w_route, b_route, tm):
    t = x2.shape[0]
    row = lambda i: (i, 0)
    fixed = lambda i: (0, 0)
    return pl.pallas_call(
        _merge_body,
        grid=(t // tm,),
        in_specs=[
            pl.BlockSpec((tm, D_MODEL), row),
            pl.BlockSpec((tm, GATE_COLS), row),
            pl.BlockSpec((tm, RWKV_WIDTH), row),
            pl.BlockSpec((tm, DIFF_V_WIDTH), row),
            pl.BlockSpec((RWKV_WIDTH, D_MODEL), fixed),
            pl.BlockSpec((DIFF_V_WIDTH, D_MODEL), fixed),
            pl.BlockSpec((D_MODEL, D_MODEL), fixed),
            pl.BlockSpec((1, D_MODEL), fixed),
            pl.BlockSpec((D_MODEL, 2 * LANES), fixed),
            pl.BlockSpec((1, LANES), fixed),
        ],
        out_specs=[
            pl.BlockSpec((tm, D_MODEL), row),
            pl.BlockSpec((tm * TILE_ROWS, LANES), row),
            pl.BlockSpec((tm, LANES), row),
        ],
        out_shape=[
            jax.ShapeDtypeStruct((t, D_MODEL), F32),
            jax.ShapeDtypeStruct((t * TILE_ROWS, LANES), F32),
            jax.ShapeDtypeStruct((t, LANES), F32),
        ],
        compiler_params=pltpu.CompilerParams(
            dimension_semantics=("parallel",), vmem_limit_bytes=VMEM_LIMIT),
        name="merge_router",
    )(x2, gates, o_rwkv, o_diff, w_br, w_bd, w_out, norm_ffn, w_route, b_route)


def _expert_body(blk_a_ref, blk_b_ref, rows_ref, h2_hbm, wga_ref, wua_ref, wda_ref, wgb_ref, wub_ref,
                 wdb_ref, yb_ref, xbuf, wg_sc, wu_sc, wd_sc, gsem, *, n_steps):
    g = pl.program_id(0)

    def refresh(s, blk_ref, w_refs):
        changed = (g == 0) | (blk_ref[g] != blk_ref[jnp.maximum(g - 1, 0)])

        @pl.when(changed)
        def _():
            for cache, w_ref in zip((wg_sc, wu_sc, wd_sc), w_refs):
                cache[s] = w_ref[0].astype(BF16)

    refresh(0, blk_a_ref, (wga_ref, wua_ref, wda_ref))
    refresh(1, blk_b_ref, (wgb_ref, wub_ref, wdb_ref))

    def start_gather(block, buf):
        for r in range(ROW_BLOCK):
            tok = rows_ref[block * ROW_BLOCK + r]
            pltpu.make_async_copy(_tile(h2_hbm, tok), _tile(xbuf, buf * ROW_BLOCK + r),
                                  gsem.at[buf]).start(r % 2)

    def wait_gather(buf):
        for _ in range(ROW_BLOCK):
            pltpu.make_async_copy(_tile(h2_hbm, 0), _tile(xbuf, 0), gsem.at[buf]).wait()

    @pl.when(g == 0)
    def _():
        for buf in range(4):
            start_gather(buf, buf)

    for parity in range(2):
        @pl.when(g % 2 == parity)
        def _():
            for s in range(2):
                buf = 2 * parity + s
                wait_gather(buf)
                x = _load_token_tiles(xbuf, buf * ROW_BLOCK * TILE_ROWS, ROW_BLOCK).astype(BF16)
                start_gather(2 * g + s + 4, buf)
                gt = _dot(x, wg_sc[s])
                up = _dot(x, wu_sc[s])
                mid = (gt * jax.nn.sigmoid(gt) * up).astype(BF16)
                _store_token_tiles(yb_ref, s * ROW_BLOCK * TILE_ROWS, _dot(mid, wd_sc[s]))

    @pl.when(g == n_steps - 1)
    def _():
        for buf in range(4):
            wait_gather(buf)


def _experts(blk_e, rows, h2, w_gate, w_up, w_down):
    n_blocks = blk_e.shape[0]
    assert n_blocks % 2 == 0
    n_steps = n_blocks // 2
    wspec_a = lambda shape: pl.BlockSpec((1,) + shape, lambda g, ba, bb, rows: (ba[g], 0, 0))
    wspec_b = lambda shape: pl.BlockSpec((1,) + shape, lambda g, ba, bb, rows: (bb[g], 0, 0))
    grid_spec = pltpu.PrefetchScalarGridSpec(
        num_scalar_prefetch=3,
        grid=(n_steps,),
        in_specs=[
            pl.BlockSpec(memory_space=pl.ANY),
            wspec_a((D_MODEL, EXPERT_FF)), wspec_a((D_MODEL, EXPERT_FF)), wspec_a((EXPERT_FF, D_MODEL)),
            wspec_b((D_MODEL, EXPERT_FF)), wspec_b((D_MODEL, EXPERT_FF)), wspec_b((EXPERT_FF, D_MODEL)),
        ],
        out_specs=pl.BlockSpec((2 * ROW_BLOCK * TILE_ROWS, LANES), lambda g, ba, bb, rows: (g, 0)),
        scratch_shapes=[pltpu.VMEM((4 * ROW_BLOCK * TILE_ROWS, LANES), F32),
                        pltpu.VMEM((2, D_MODEL, EXPERT_FF), BF16), pltpu.VMEM((2, D_MODEL, EXPERT_FF), BF16),
                        pltpu.VMEM((2, EXPERT_FF, D_MODEL), BF16), pltpu.SemaphoreType.DMA((4,))],
    )
    return pl.pallas_call(
        functools.partial(_expert_body, n_steps=n_steps),
        grid_spec=grid_spec,
        out_shape=jax.ShapeDtypeStruct((n_blocks * ROW_BLOCK * TILE_ROWS, LANES), F32),
        compiler_params=pltpu.CompilerParams(
            dimension_semantics=("arbitrary",), vmem_limit_bytes=VMEM_LIMIT),
        name="experts",
    )(blk_e[0::2], blk_e[1::2], rows, h2, w_gate, w_up, w_down, w_gate, w_up, w_down)


def _final_body(pos_ref, x1_ref, route_ref, p_ref, yb_hbm, np_ref, wpg_ref, wpp_ref, o_ref,
                ybuf, sem, *, tm, n_steps):
    i = pl.program_id(0)
    per_tile = TOP_K * tm

    def start_gather(tile_idx, s):
        for r in range(per_tile):
            row = pos_ref[tile_idx * per_tile + r]
            pltpu.make_async_copy(_tile(yb_hbm, row), _tile(ybuf, s * per_tile + r), sem.at[s]).start(r % 2)

    def wait_gather(s):
        for _ in range(per_tile):
            pltpu.make_async_copy(_tile(yb_hbm, 0), _tile(ybuf, 0), sem.at[s]).wait()

    @pl.when(i == 0)
    def _():
        start_gather(0, 0)
        start_gather(1, 1)

    for s in range(2):
        rows = pl.ds(s * tm, tm)
        wait_gather(s)
        y_first = _load_token_tiles(ybuf, s * per_tile * TILE_ROWS, tm)
        y_second = _load_token_tiles(ybuf, (s * per_tile + tm) * TILE_ROWS, tm)
        start_gather(2 * i + s + 2, s)
        route = route_ref[rows, :]
        x2 = x1_ref[rows, :] + route[:, 2:3] * y_first + route[:, 3:4] * y_second
        ms = jnp.mean(x2 * x2, axis=-1, keepdims=True)
        hn = (x2 * lax.rsqrt(ms + RMS_EPS) * np_ref[...]).astype(BF16)
        gate = jax.nn.sigmoid(_dot(hn, wpg_ref[...]))
        o_ref[rows, :] = x2 + gate * _dot(p_ref[rows, :].astype(BF16), wpp_ref[...])

    @pl.when(i == n_steps - 1)
    def _():
        for s in range(2):
            wait_gather(s)


def _final(pos, x1, route, p2, yb, norm_ple, w_pg, w_pp, tm):
    t = x1.shape[0]
    assert t % (2 * tm) == 0
    n_steps = t // (2 * tm)
    row = lambda i, pos: (i, 0)
    fixed = lambda i, pos: (0, 0)
    grid_spec = pltpu.PrefetchScalarGridSpec(
        num_scalar_prefetch=1,
        grid=(n_steps,),
        in_specs=[
            pl.BlockSpec((2 * tm, D_MODEL), row),
            pl.BlockSpec((2 * tm, LANES), row),
            pl.BlockSpec((2 * tm, PLE_DIM), row),
            pl.BlockSpec(memory_space=pl.ANY),
            pl.BlockSpec((1, D_MODEL), fixed),
            pl.BlockSpec((D_MODEL, D_MODEL), fixed),
            pl.BlockSpec((PLE_DIM, D_MODEL), fixed),
        ],
        out_specs=pl.BlockSpec((2 * tm, D_MODEL), row),
        scratch_shapes=[pltpu.VMEM((2 * TOP_K * tm * TILE_ROWS, LANES), F32), pltpu.SemaphoreType.DMA((2,))],
    )
    return pl.pallas_call(
        functools.partial(_final_body, tm=tm, n_steps=n_steps),
        grid_spec=grid_spec,
        out_shape=jax.ShapeDtypeStruct((t, D_MODEL), F32),
        compiler_params=pltpu.CompilerParams(
            dimension_semantics=("arbitrary",), vmem_limit_bytes=VMEM_LIMIT),
        name="combine_ple",
    )(pos, x1, route, p2, yb, norm_ple, w_pg, w_pp)


def _dispatch_plan(route, n_tokens, tm):
    eid = route[:, :TOP_K].astype(jnp.int32).reshape(-1)
    n_assign = n_tokens * TOP_K
    order = jnp.argsort(eid).astype(jnp.int32)
    rank = jnp.argsort(order).astype(jnp.int32)
    experts = jnp.arange(N_EXPERTS, dtype=jnp.int32)
    onehot = eid[:, None] == experts[None, :]
    counts = jnp.sum(onehot.astype(jnp.int32), axis=0)
    starts = jnp.cumsum(counts) - counts
    pcounts = ((counts + ROW_BLOCK - 1) // ROW_BLOCK) * ROW_BLOCK
    pends = jnp.cumsum(pcounts)
    pstarts = pends - pcounts
    n_blocks = (n_assign + N_EXPERTS * ROW_BLOCK) // ROW_BLOCK
    row0 = jnp.arange(n_blocks, dtype=jnp.int32) * ROW_BLOCK
    blk_e = jnp.minimum(jnp.sum((pends[None, :] <= row0[:, None]).astype(jnp.int32), axis=1), N_EXPERTS - 1)
    offset = row0 - pstarts[blk_e]
    valid = counts[blk_e] - offset
    first = jnp.clip(starts[blk_e] + offset, 0, n_assign - 1)
    r = jnp.arange(ROW_BLOCK, dtype=jnp.int32)[None, :]
    idx = jnp.minimum(first[:, None] + r, n_assign - 1)
    rows = jnp.where(r < valid[:, None], order[idx] // TOP_K, 0).reshape(-1)
    rows = jnp.concatenate([rows, jnp.zeros((4 * ROW_BLOCK,), jnp.int32)])
    shift = jnp.sum(jnp.where(onehot, (pstarts - starts)[None, :], 0), axis=1)
    pos = (rank + shift).reshape(n_tokens // tm, tm, TOP_K).transpose(0, 2, 1).reshape(-1)
    pos = jnp.concatenate([pos, jnp.zeros((2 * TOP_K * tm,), jnp.int32)])
    return blk_e, rows, pos


def _rope_tables(seq):
    inv = ROPE_THETA ** (-jnp.arange(0, HEAD_DIM, 2, dtype=F32) / HEAD_DIM)
    ang = jnp.arange(seq, dtype=F32)[:, None] * inv[None, :]
    ang = jnp.concatenate([ang, ang], axis=-1)
    reps = DIFF_QK_WIDTH // HEAD_DIM
    return jnp.tile(jnp.cos(ang), (1, reps)), jnp.tile(jnp.sin(ang), (1, reps))


def _tile_plan(seq):
    return dict(
        proj=min(512, seq),
        combine=256,
        rwkv=min(512, seq),
        attn=min(256, seq),
    )


def _layer(x2, p2, batch, seq, lam_init, prm):
    t = x2.shape[0]
    tiles = _tile_plan(seq)
    tm = tiles['proj']
    cos, sin = _rope_tables(seq)
    reps = DIFF_QK_WIDTH // HEAD_DIM
    prw, q, k, v, gates = _inproj(
        x2, prm['norm_mix'][None], prm['w_in'].astype(BF16), cos, sin,
        jnp.tile(prm['q_norm'], reps)[None], jnp.tile(prm['k_norm'], reps)[None], seq, tm)

    zpad = jnp.zeros((DECAY_LORA, RWKV_WIDTH), F32)
    w2p = jnp.concatenate([prm['rwkv_w2'], zpad], axis=0).astype(BF16)
    a2p = jnp.concatenate([zpad, prm['rwkv_a2']], axis=0).astype(BF16)
    o_rwkv = _rwkv(
        prw.reshape(batch, seq, RWKV_COLS), prm['rwkv_mu'][None], prm['rwkv_w0'][None], w2p,
        prm['rwkv_a0'][None], a2p, prm['rwkv_g2'].astype(BF16), prm['rwkv_k_k'][None],
        prm['rwkv_k_a'][None], prm['rwkv_r_k'].reshape(1, RWKV_WIDTH), prm['rwkv_ln_w'][None],
        prm['rwkv_ln_b'][None], tiles['rwkv']).reshape(t, RWKV_WIDTH)

    o_diff = _diff_attention(
        q, k, v, prm['lambda_q1'][None], prm['lambda_k1'][None], prm['lambda_q2'][None],
        prm['lambda_k2'][None], prm['subln_w'][None], batch, seq, lam_init, tiles['attn'])

    w_r = jnp.concatenate([prm['w_group'], prm['w_expert_router']], axis=1)
    w_r = jnp.pad(w_r, ((0, 0), (0, LANES - ROUTE_COLS)))
    w_r_hi = w_r.astype(BF16)
    w_r_lo = (w_r - w_r_hi.astype(F32)).astype(BF16)
    b_r = jnp.pad(jnp.concatenate([prm['b_group'], prm['b_expert_router']]), (0, LANES - ROUTE_COLS))[None]
    x1, h2, route = _merge(
        x2, gates, o_rwkv, o_diff, prm['w_branch_rwkv'].astype(BF16), prm['w_branch_diff'].astype(BF16),
        prm['w_out'].astype(BF16), prm['norm_ffn'][None], jnp.concatenate([w_r_hi, w_r_lo], axis=1), b_r, tm)

    blk_e, rows, pos = _dispatch_plan(route, t, tiles['combine'])
    yb = _experts(blk_e, rows, h2, prm['w_gate'], prm['w_up'], prm['w_down'])
    return _final(pos, x1, route, p2, yb, prm['norm_ple'][None], prm['w_ple_gate'].astype(BF16),
                  prm['w_ple_proj'].astype(BF16), tiles['combine'])


_PARAM_NAMES = (
    'norm_mix', 'w_in', 'rwkv_mu', 'rwkv_w0', 'rwkv_w2', 'rwkv_a0', 'rwkv_a2', 'rwkv_g2', 'rwkv_k_k',
    'rwkv_k_a', 'rwkv_r_k', 'rwkv_ln_w', 'rwkv_ln_b', 'q_norm', 'k_norm', 'lambda_q1', 'lambda_k1',
    'lambda_q2', 'lambda_k2', 'subln_w', 'w_branch_rwkv', 'w_branch_diff', 'w_out', 'norm_ffn',
    'w_group', 'b_group', 'w_expert_router', 'b_expert_router', 'w_gate', 'w_up', 'w_down', 'norm_ple',
    'w_ple_gate', 'w_ple_proj')


def kernel(x, p, norm_mix, w_in, rwkv_mu, rwkv_w0, rwkv_w2, rwkv_a0, rwkv_a2, rwkv_g2, rwkv_k_k,
           rwkv_k_a, rwkv_r_k, rwkv_ln_w, rwkv_ln_b, q_norm, k_norm, lambda_q1, lambda_k1, lambda_q2,
           lambda_k2, subln_w, w_branch_rwkv, w_branch_diff, w_out, norm_ffn, w_group, b_group,
           w_expert_router, b_expert_router, w_gate, w_up, w_down, norm_ple, w_ple_gate, w_ple_proj):
    stacked = dict(zip(_PARAM_NAMES, (
        norm_mix, w_in, rwkv_mu, rwkv_w0, rwkv_w2, rwkv_a0, rwkv_a2, rwkv_g2, rwkv_k_k, rwkv_k_a,
        rwkv_r_k, rwkv_ln_w, rwkv_ln_b, q_norm, k_norm, lambda_q1, lambda_k1, lambda_q2, lambda_k2,
        subln_w, w_branch_rwkv, w_branch_diff, w_out, norm_ffn, w_group, b_group, w_expert_router,
        b_expert_router, w_gate, w_up, w_down, norm_ple, w_ple_gate, w_ple_proj)))
    batch, seq, _ = x.shape
    depth = p.shape[0]
    x2 = x.reshape(batch * seq, D_MODEL)
    for layer in range(depth):
        lam_init = 0.8 - 0.6 * math.exp(-0.3 * layer)
        prm = {name: val[layer] for name, val in stacked.items()}
        x2 = _layer(x2, p[layer].reshape(batch * seq, PLE_DIM), batch, seq, lam_init, prm)
    return x2.reshape(batch, seq, D_MODEL)
```

```python
import functools
import math

import jax
import jax.numpy as jnp
from jax import lax
from jax.experimental import pallas as pl
from jax.experimental.pallas import tpu as pltpu

F32 = jnp.float32
BF16 = jnp.bfloat16

D_MODEL = 1024
PLE_DIM = 256
RMS_EPS = 1e-6

RWKV_HEADS = 8
HEAD_DIM = 64
RWKV_WIDTH = RWKV_HEADS * HEAD_DIM
DECAY_LORA = 64
AAA_LORA = 64
GATE_LORA = 128
GN_EPS = 64e-5
RWKV_COLS = 3 * RWKV_WIDTH + DECAY_LORA + AAA_LORA + GATE_LORA

DIFF_HEADS = 4
DIFF_QK_WIDTH = DIFF_HEADS * 2 * HEAD_DIM
DIFF_V_DIM = 2 * HEAD_DIM
DIFF_V_WIDTH = DIFF_HEADS * DIFF_V_DIM
DIFF_COLS = 2 * DIFF_QK_WIDTH + DIFF_V_WIDTH
ROPE_THETA = 10000.0
GATE_COLS = 2 * D_MODEL

N_GROUPS = 4
EXPERTS_PER_GROUP = 8
N_EXPERTS = N_GROUPS * EXPERTS_PER_GROUP
TOP_K = 2
EXPERT_FF = 512
ROW_BLOCK = 128
STEP_ROWS = 2 * ROW_BLOCK

LANES = 128
CHUNK = 64
VMEM_LIMIT = 56 * 1024 * 1024


def _dot(a, b):
    return jnp.dot(a, b, preferred_element_type=F32)


def _dot_nt(a, b):
    return lax.dot_general(a, b, (((1,), (1,)), ((), ())), preferred_element_type=F32)


SUBLANES = 8
TILE_ROWS = D_MODEL // LANES
assert TILE_ROWS == SUBLANES


def _store_token_tiles(ref, base, val, pitch=TILE_ROWS):
    n = val.shape[0]
    for j in range(TILE_ROWS):
        ref[pl.ds(base + j, n, stride=pitch), :] = val[:, j * LANES:(j + 1) * LANES]


def _load_token_tiles(ref, base, n, pitch=TILE_ROWS):
    return jnp.concatenate([ref[pl.ds(base + j, n, stride=pitch), :] for j in range(TILE_ROWS)], axis=1)


def _tile(ref, index):
    return ref.at[pl.ds(pl.multiple_of(index * TILE_ROWS, TILE_ROWS), TILE_ROWS)]


MXU_DIM = 256


def _group_ones(width, group):
    i = jnp.arange(width) // group
    return (i[:, None] == i[None, :]).astype(BF16)


def _head_sums(t, ones):
    t16 = t.astype(BF16)
    return jnp.concatenate([_dot(t16[:, i:i + MXU_DIM], ones) for i in range(0, t.shape[1], MXU_DIM)], axis=1)


def _rotate_half(t):
    width = t.shape[-1]
    lane = lax.broadcasted_iota(jnp.int32, t.shape, 1)
    fwd = pltpu.roll(t, width - HEAD_DIM // 2, 1)
    bwd = pltpu.roll(t, HEAD_DIM // 2, 1)
    return jnp.where(lane % HEAD_DIM < HEAD_DIM // 2, -fwd, bwd)


def _inproj_body(x_ref, g_ref, w_ref, cos_ref, sin_ref, qg_ref, kg_ref, ones_ref,
                 prw_ref, q_ref, k_ref, v_ref, gate_ref):
    x = x_ref[...]
    ms = jnp.mean(x * x, axis=-1, keepdims=True)
    h = (x * lax.rsqrt(ms + RMS_EPS) * g_ref[...]).astype(BF16)
    prw_ref[...] = _dot(h, w_ref[:, :RWKV_COLS])
    pd = _dot(h, w_ref[:, RWKV_COLS:RWKV_COLS + DIFF_COLS])
    cos = cos_ref[...]
    sin = sin_ref[...]

    def qk_prep(t, gain, scale):
        ssq = _head_sums(t * t, ones_ref[...])
        t = t * lax.rsqrt(ssq * (1.0 / HEAD_DIM) + RMS_EPS) * gain
        return ((t * cos + _rotate_half(t) * sin) * scale).astype(BF16)

    q_ref[...] = qk_prep(pd[:, :DIFF_QK_WIDTH], qg_ref[...], HEAD_DIM ** -0.5 * math.log2(math.e))
    k_ref[...] = qk_prep(pd[:, DIFF_QK_WIDTH:2 * DIFF_QK_WIDTH], kg_ref[...], 1.0)
    v_ref[...] = pd[:, 2 * DIFF_QK_WIDTH:].astype(BF16)
    gate_ref[...] = jax.nn.sigmoid(_dot(h, w_ref[:, RWKV_COLS + DIFF_COLS:])).astype(BF16)


def _inproj(x2, norm_g, w_in, cos, sin, q_gain, k_gain, seq, tm):
    t = x2.shape[0]
    n_seq_tiles = seq // tm
    row = lambda i: (i, 0)
    fixed = lambda i: (0, 0)
    rope = lambda i: (i % n_seq_tiles, 0)
    return pl.pallas_call(
        _inproj_body,
        grid=(t // tm,),
        in_specs=[
            pl.BlockSpec((tm, D_MODEL), row),
            pl.BlockSpec((1, D_MODEL), fixed),
            pl.BlockSpec((D_MODEL, RWKV_COLS + DIFF_COLS + GATE_COLS), fixed),
            pl.BlockSpec((tm, DIFF_QK_WIDTH), rope),
            pl.BlockSpec((tm, DIFF_QK_WIDTH), rope),
            pl.BlockSpec((1, DIFF_QK_WIDTH), fixed),
            pl.BlockSpec((1, DIFF_QK_WIDTH), fixed),
            pl.BlockSpec((MXU_DIM, MXU_DIM), fixed),
        ],
        out_specs=[
            pl.BlockSpec((tm, RWKV_COLS), row),
            pl.BlockSpec((tm, DIFF_QK_WIDTH), row),
            pl.BlockSpec((tm, DIFF_QK_WIDTH), row),
            pl.BlockSpec((tm, DIFF_V_WIDTH), row),
            pl.BlockSpec((tm, GATE_COLS), row),
        ],
        out_shape=[
            jax.ShapeDtypeStruct((t, RWKV_COLS), F32),
            jax.ShapeDtypeStruct((t, DIFF_QK_WIDTH), BF16),
            jax.ShapeDtypeStruct((t, DIFF_QK_WIDTH), BF16),
            jax.ShapeDtypeStruct((t, DIFF_V_WIDTH), BF16),
            jax.ShapeDtypeStruct((t, GATE_COLS), BF16),
        ],
        compiler_params=pltpu.CompilerParams(
            dimension_semantics=("parallel",), vmem_limit_bytes=VMEM_LIMIT),
        name="inproj",
    )(x2, norm_g, w_in, cos, sin, q_gain, k_gain, _group_ones(MXU_DIM, HEAD_DIM))


def _attn_body(q_ref, k_ref, v_ref, lq1_ref, lk1_ref, lq2_ref, lk2_ref, sw_ref, o_ref,
               qs_sc, m_sc, l_sc, acc_sc, s0_sc, s1_sc, *, tq, wide, lam_init):
    qi = pl.program_id(2)
    heads = range(q_ref.shape[1] // LANES)
    cols = [slice(h * LANES, (h + 1) * LANES) for h in heads]
    for h in heads:
        q = q_ref[:, cols[h]]
        lane = lax.broadcasted_iota(jnp.int32, q.shape, 1)
        qs_sc[h, :tq, :] = jnp.where(lane < HEAD_DIM, q, jnp.zeros_like(q))
        qs_sc[h, tq:, :] = jnp.where(lane >= HEAD_DIM, q, jnp.zeros_like(q))
    m_sc[...] = jnp.full(m_sc.shape, -jnp.inf, F32)
    l_sc[...] = jnp.zeros(l_sc.shape, F32)
    acc_sc[...] = jnp.zeros(acc_sc.shape, F32)

    width = wide * tq

    def scores(j, s_sc):
        for h in heads:
            kb = k_ref[pl.ds(pl.multiple_of(j * width, width), width), cols[h]]
            s_sc[h] = _dot_nt(kb, qs_sc[h])

    def update(j, s_sc, masked):
        for h in heads:
            s = s_sc[h]
            if masked:
                key = lax.broadcasted_iota(jnp.int32, s.shape, 0) + j * width
                query = lax.broadcasted_iota(jnp.int32, s.shape, 1) % tq + qi * tq
                s = jnp.where(key <= query, s, -jnp.inf)
            vb = v_ref[pl.ds(pl.multiple_of(j * width, width), width), cols[h]]
            m_prev = m_sc[h]
            m_new = jnp.maximum(m_prev, jnp.max(s, axis=0, keepdims=True))
            alpha = jnp.exp2(m_prev - m_new)
            p = jnp.exp2(s - m_new)
            l_sc[h] = alpha * l_sc[h] + jnp.sum(p, axis=0, keepdims=True)
            acc_sc[h] = alpha * acc_sc[h] + _dot_tn(vb, p.astype(BF16))
            m_sc[h] = m_new

    last = qi // wide
    pairs = last // 2
    scores(0, s0_sc)

    def body(jj, carry):
        scores(2 * jj + 1, s1_sc)
        update(2 * jj, s0_sc, False)
        scores(2 * jj + 2, s0_sc)
        update(2 * jj + 1, s1_sc, False)
        return carry

    lax.fori_loop(0, pairs, body, 0)

    @pl.when(last % 2 == 1)
    def _():
        scores(last, s1_sc)
        update(last - 1, s0_sc, False)
        update(last, s1_sc, True)

    @pl.when(last % 2 == 0)
    def _():
        update(last, s0_sc, True)

    lam = (jnp.exp(jnp.sum(lq1_ref[...] * lk1_ref[...], axis=-1, keepdims=True))
           - jnp.exp(jnp.sum(lq2_ref[...] * lk2_ref[...], axis=-1, keepdims=True)) + lam_init)
    for h in heads:
        o = acc_sc[h] / l_sc[h]
        o = o[:, :tq] - lam * o[:, tq:]
        ms = jnp.mean(o * o, axis=0, keepdims=True)
        o = (o * lax.rsqrt(ms + RMS_EPS)).T
        o_ref[:, cols[h]] = o * sw_ref[...] * (1.0 - lam_init)


def _diff_attention(q, k, v, lq1, lk1, lq2, lk2, subln_w, batch, seq, lam_init, tq):
    t = q.shape[0]
    nq = seq // tq
    vec = pl.BlockSpec((1, HEAD_DIM), lambda b, h, i: (0, 0))
    wide = max(1, min(4, nq // 2))
    assert nq % wide == 0
    hpg = 2
    return pl.pallas_call(
        functools.partial(_attn_body, tq=tq, wide=wide, lam_init=lam_init),
        grid=(batch, DIFF_HEADS // hpg, nq),
        in_specs=[
            pl.BlockSpec((tq, hpg * LANES), lambda b, h, i: (b * nq + i, h)),
            pl.BlockSpec((seq, hpg * LANES), lambda b, h, i: (b, h)),
            pl.BlockSpec((seq, hpg * LANES), lambda b, h, i: (b, h)),
            vec, vec, vec, vec,
            pl.BlockSpec((1, DIFF_V_DIM), lambda b, h, i: (0, 0)),
        ],
        out_specs=pl.BlockSpec((tq, hpg * LANES), lambda b, h, i: (b * nq + i, h)),
        out_shape=jax.ShapeDtypeStruct((t, DIFF_V_WIDTH), F32),
        scratch_shapes=[
            pltpu.VMEM((hpg, 2 * tq, LANES), BF16),
            pltpu.VMEM((hpg, 1, 2 * tq), F32),
            pltpu.VMEM((hpg, 1, 2 * tq), F32),
            pltpu.VMEM((hpg, DIFF_V_DIM, 2 * tq), F32),
            pltpu.VMEM((hpg, wide * tq, 2 * tq), F32),
            pltpu.VMEM((hpg, wide * tq, 2 * tq), F32),
        ],
        compiler_params=pltpu.CompilerParams(
            dimension_semantics=("parallel", "parallel", "arbitrary"),
            vmem_limit_bytes=VMEM_LIMIT),
        name="diff_attn",
    )(q, k, v, lq1, lk1, lq2, lk2, subln_w)


def _dot_tn(a, b):
    return lax.dot_general(a, b, (((0,), (0,)), ((), ())), preferred_element_type=F32)


def _wkv_prep(problems):
    c = CHUNK
    lane = lax.broadcasted_iota(jnp.int32, (c, LANES), 1)
    m0 = lane < HEAD_DIM
    row = lax.broadcasted_iota(jnp.int32, (LANES, LANES), 0)
    col = lax.broadcasted_iota(jnp.int32, (LANES, LANES), 1)
    same = (row // c) == (col // c)
    strict = same & ((col % c) < (row % c))
    incl = same & ((col % c) <= (row % c))
    eye = row == col

    def stack(t):
        zero = jnp.zeros_like(t)
        return jnp.concatenate([jnp.where(m0, t, zero), jnp.where(m0, zero, t)], axis=0)

    def dup(t):
        return jnp.concatenate([t, t], axis=0)

    def fold(t):
        return t[:c] + t[c:]

    n = len(problems)
    xs = [_dot_nt(jnp.concatenate([q['kkt'], q['rt']], axis=0),
                  jnp.concatenate([stack(q['bt']), stack(q['kt'])], axis=0)) for q in problems]
    l_pow = [jnp.where(strict, dup(x[:c, :LANES]), 0.0) for x in xs]
    m_ak = [jnp.where(strict, dup(x[:c, LANES:]), 0.0).astype(BF16) for x in xs]
    a_rbk = [jnp.concatenate([jnp.where(incl, dup(x[c:, :LANES]), 0.0),
                              jnp.where(incl, dup(x[c:, LANES:]), 0.0)], axis=1).astype(BF16) for x in xs]
    t_inv = [jnp.where(eye, 1.0, 0.0) - l for l in l_pow]
    for _ in range(5):
        l16 = [l.astype(BF16) for l in l_pow]
        l_pow = [_dot(l, l) for l in l16]
        upd = [_dot(t.astype(BF16), l.astype(BF16)) for t, l in zip(t_inv, l_pow)]
        t_inv = [t + u for t, u in zip(t_inv, upd)]

    vs = [stack(q['v']) for q in problems]
    mv = [_dot(m, v) for m, v in zip(m_ak, vs)]
    z = [_dot(t.astype(BF16), jnp.concatenate([stack(q['kkt']), m.astype(BF16)], axis=1))
         for t, q, m in zip(t_inv, problems, mv)]
    zero_sq = jnp.zeros((LANES, LANES), BF16)
    qy = [_dot(a, jnp.concatenate([(-zz).astype(BF16), jnp.concatenate([zero_sq, v], axis=1)], axis=0))
          for a, zz, v in zip(a_rbk, z, vs)]
    zero_c = jnp.zeros((c, LANES), BF16)
    pd = [_dot_tn(jnp.concatenate([q['bh'], q['kh']], axis=0),
                  jnp.concatenate([jnp.concatenate([fold(-zz[:, :LANES]), fold(-zz[:, LANES:])], axis=1).astype(BF16),
                                   jnp.concatenate([zero_c, q['v']], axis=1)], axis=0))
          for q, zz in zip(problems, z)]
    out = []
    for i in range(n):
        q = problems[i]
        qs = stack(q['rt']).astype(F32) + qy[i][:, :LANES]
        phi = (jnp.where(eye, jnp.broadcast_to(q['pc'], (LANES, LANES)), 0.0)
               + jnp.where(same, pd[i][:, :LANES], 0.0))
        delta = jnp.where(same, pd[i][:, LANES:], 0.0)
        qphi = jnp.concatenate([fold(qs), phi], axis=0).astype(BF16)
        out.append((qphi, fold(qy[i][:, LANES:]), delta))
    return out


def _rwkv_body(pr_ref, halo_ref, mu_ref, w0_ref, w2_ref, a0_ref, a2_ref, g2_ref, kk_ref, ka_ref,
               rk_ref, lnw_ref, lnb_ref, ones_ref, tri_ref, o_ref,
               hp_sc, rt_sc, kt_sc, bt_sc, kkt_sc, kh_sc, bh_sc, v_sc, pc_sc, qphi_sc, delta_sc, y_sc,
               *, rows):
    i = pl.program_id(1)
    n_pairs = RWKV_WIDTH // LANES
    n_chunks = rows // CHUNK

    @pl.when(i == 0)
    def _():
        hp_sc[...] = jnp.zeros(hp_sc.shape, F32)

    pr = pr_ref[0]
    first = jnp.where(i == 0, 0.0, halo_ref[0, 7:8, :])
    prev = jnp.concatenate([first, pr[:-1]], axis=0)
    xs = pr + (prev - pr) * mu_ref[...]
    w = RWKV_WIDTH
    r = xs[:, :w]
    k = xs[:, w:2 * w]
    v = xs[:, 2 * w:3 * w]
    x_lora = xs[:, 3 * w:3 * w + DECAY_LORA + AAA_LORA]
    xg = xs[:, 3 * w + DECAY_LORA + AAA_LORA:]
    wl = w0_ref[...] + _dot(jnp.tanh(x_lora).astype(BF16), w2_ref[...])
    sp = jnp.maximum(-wl, 0.0) + jnp.log(1.0 + jnp.exp(-jnp.abs(wl)))
    lw = -jnp.exp(-sp - 0.5)
    a = jax.nn.sigmoid(a0_ref[...] + _dot(x_lora.astype(BF16), a2_ref[...]))
    g = _dot(jax.nn.sigmoid(xg).astype(BF16), g2_ref[...])
    ones = ones_ref[...]
    kx = k * kk_ref[...]
    ssq = _head_sums(kx * kx, ones)
    kn = kx / jnp.maximum(jnp.sqrt(ssq), 1e-12)
    km = k * (1.0 + (a - 1.0) * ka_ref[...])
    bonus = _head_sums(r * km * rk_ref[...], ones) * v

    hi = lw.astype(BF16)
    r1 = lw - hi.astype(F32)
    mid = r1.astype(BF16)
    lo = (r1 - mid.astype(F32)).astype(BF16)
    tri = tri_ref[...]
    parts = jnp.concatenate([hi, mid, lo], axis=1)
    cum = jnp.concatenate([_dot(tri, parts[i:i + MXU_DIM]) for i in range(0, rows, MXU_DIM)], axis=0)
    cum = cum[:, :w] + cum[:, w:2 * w] + cum[:, 2 * w:]

    cum3 = cum.reshape(n_chunks, CHUNK, w)
    end3 = cum3[:, CHUNK - 1:CHUNK, :]
    p_inv = jnp.exp(-cum)
    p_end = jnp.exp(end3 - cum3).reshape(rows, w)
    b = kn * a
    rt_sc[...] = (r * jnp.exp(cum)).astype(BF16)
    kt_sc[...] = (km * p_inv).astype(BF16)
    bt_sc[...] = (b * p_inv).astype(BF16)
    kkt_sc[...] = (kn * jnp.exp(cum - lw)).astype(BF16)
    kh_sc[...] = (km * p_end).astype(BF16)
    bh_sc[...] = (b * p_end).astype(BF16)
    v_sc[...] = v.astype(BF16)
    pc_sc[...] = jnp.broadcast_to(jnp.exp(end3), pc_sc.shape)

    group = 4 if n_chunks % 4 == 0 else 1

    def prep_chunks(gi, carry):
        where = []
        for g in range(group):
            ci = gi * group + g
            rs = pl.ds(pl.multiple_of(ci * CHUNK, CHUNK), CHUNK)
            for p in range(n_pairs):
                where.append((ci, rs, p, slice(p * LANES, (p + 1) * LANES)))
        problems = [dict(rt=rt_sc[rs, sl], kt=kt_sc[rs, sl], bt=bt_sc[rs, sl], kkt=kkt_sc[rs, sl],
                         kh=kh_sc[rs, sl], bh=bh_sc[rs, sl], v=v_sc[rs, sl], pc=pc_sc[ci, 0:1, sl])
                    for ci, rs, p, sl in where]
        for (ci, rs, p, sl), (qphi, y0, delta) in zip(where, _wkv_prep(problems)):
            qphi_sc[ci, p] = qphi
            delta_sc[ci, p] = delta
            y_sc[rs, sl] = y0
        return carry

    lax.fori_loop(0, n_chunks // group, prep_chunks, 0)

    def scan_chunk(ci, carry):
        rs = pl.ds(pl.multiple_of(ci * CHUNK, CHUNK), CHUNK)
        for p in range(n_pairs):
            sl = slice(p * LANES, (p + 1) * LANES)
            res = _dot(qphi_sc[ci, p], hp_sc[p].astype(BF16))
            y_sc[rs, sl] += res[:CHUNK]
            hp_sc[p] = res[CHUNK:] + delta_sc[ci, p]
        return carry

    lax.fori_loop(0, n_chunks, scan_chunk, 0, unroll=True)

    y = y_sc[...]
    mean = _head_sums(y, ones) * (1.0 / HEAD_DIM)
    yc = y - mean
    var = _head_sums(yc * yc, ones) * (1.0 / HEAD_DIM)
    yn = yc * lax.rsqrt(var + GN_EPS) * lnw_ref[...] + lnb_ref[...]
    o_ref[0] = ((yn + bonus) * g).astype(o_ref.dtype)


def _rwkv(pr3, mu, w0, w2p, a0, a2p, g2, k_k, k_a, r_k, ln_w, ln_b, rows):
    batch, seq, _ = pr3.shape
    nb = seq // rows
    fixed = lambda b, i: (0, 0)
    vec = pl.BlockSpec((1, RWKV_WIDTH), fixed)
    lora = DECAY_LORA + AAA_LORA
    n_pairs = RWKV_WIDTH // LANES
    assert rows % MXU_DIM == 0 and MXU_DIM % CHUNK == 0
    ci = jnp.arange(MXU_DIM) // CHUNK
    ti = jnp.arange(MXU_DIM)
    tri = ((ci[:, None] == ci[None, :]) & (ti[None, :] <= ti[:, None])).astype(BF16)
    big = pltpu.VMEM((rows, RWKV_WIDTH), F32)
    n_chunks = rows // CHUNK
    return pl.pallas_call(
        functools.partial(_rwkv_body, rows=rows),
        grid=(batch, nb),
        in_specs=[
            pl.BlockSpec((1, rows, RWKV_COLS), lambda b, i: (b, i, 0)),
            pl.BlockSpec((1, 8, RWKV_COLS), lambda b, i: (b, jnp.maximum(i * (rows // 8) - 1, 0), 0)),
            pl.BlockSpec((1, RWKV_COLS), fixed),
            vec,
            pl.BlockSpec((lora, RWKV_WIDTH), fixed),
            vec,
            pl.BlockSpec((lora, RWKV_WIDTH), fixed),
            pl.BlockSpec((GATE_LORA, RWKV_WIDTH), fixed),
            vec, vec, vec, vec, vec,
            pl.BlockSpec((MXU_DIM, MXU_DIM), fixed),
            pl.BlockSpec((MXU_DIM, MXU_DIM), fixed),
        ],
        out_specs=pl.BlockSpec((1, rows, RWKV_WIDTH), lambda b, i: (b, i, 0)),
        out_shape=jax.ShapeDtypeStruct((batch, seq, RWKV_WIDTH), F32),
        scratch_shapes=(
            [pltpu.VMEM((n_pairs, LANES, LANES), F32)]
            + [pltpu.VMEM((rows, RWKV_WIDTH), BF16)] * 7
            + [pltpu.VMEM((n_chunks, 8, RWKV_WIDTH), F32),
               pltpu.VMEM((n_chunks, n_pairs, CHUNK + LANES, LANES), BF16),
               pltpu.VMEM((n_chunks, n_pairs, LANES, LANES), F32),
               big]),
        compiler_params=pltpu.CompilerParams(
            dimension_semantics=("parallel", "arbitrary"), vmem_limit_bytes=VMEM_LIMIT),
        name="rwkv7",
    )(pr3, pr3, mu, w0, w2p, a0, a2p, g2, k_k, k_a, r_k, ln_w, ln_b,
      _group_ones(MXU_DIM, HEAD_DIM), tri)


ROUTE_COLS = N_GROUPS + N_EXPERTS


def _merge_body(x_ref, gate_ref, orw_ref, odf_ref, wbr_ref, wbd_ref, wout_ref, nf_ref, wr_ref,
                br_ref, x1_ref, h2_ref, route_ref):
    gates = gate_ref[...].astype(F32)
    mixed = (gates[:, :D_MODEL] * _dot(orw_ref[...].astype(BF16), wbr_ref[...])
             + gates[:, D_MODEL:] * _dot(odf_ref[...].astype(BF16), wbd_ref[...]))
    x1 = x_ref[...] + _dot(mixed.astype(BF16), wout_ref[...])
    x1_ref[...] = x1
    ms = jnp.mean(x1 * x1, axis=-1, keepdims=True)
    h2 = x1 * lax.rsqrt(ms + RMS_EPS) * nf_ref[...]
    _store_token_tiles(h2_ref, 0, h2)

    hi = h2.astype(BF16)
    lo = (h2 - hi.astype(F32)).astype(BF16)
    acc = _dot(hi, wr_ref[...])
    logits = acc[:, :LANES] + acc[:, LANES:] + _dot(lo, wr_ref[:, :LANES]) + br_ref[...]
    lane = lax.broadcasted_iota(jnp.int32, logits.shape, 1).astype(F32)
    neg = -jnp.inf

    def top(vals):
        m = jnp.max(vals, axis=-1, keepdims=True)
        return m, jnp.min(jnp.where(vals == m, lane, float(LANES)), axis=-1, keepdims=True)

    gl = jnp.where(lane < N_GROUPS, logits, neg)
    gm, g_idx = top(gl)
    g_top = 1.0 / jnp.sum(jnp.exp(gl - gm), axis=-1, keepdims=True)
    first = N_GROUPS + EXPERTS_PER_GROUP * g_idx
    el = jnp.where((lane >= first) & (lane < first + EXPERTS_PER_GROUP), logits, neg)
    t1, i1 = top(el)
    t2, i2 = top(jnp.where(lane == i1, neg, el))
    e2 = jnp.exp(t2 - t1)
    w1 = g_top / (1.0 + e2)
    w2 = g_top * e2 / (1.0 + e2)
    route_ref[...] = jnp.where(lane == 0, i1 - N_GROUPS, jnp.where(lane == 1, i2 - N_GROUPS,
                               jnp.where(lane == 2, w1, jnp.where(lane == 3, w2, 0.0))))


def _merge(x2, gates, o_rwkv, o_diff, w_br, w_bd, w_out, norm_ffn, w_route, b_route, tm):
    t = x2.shape[0]
    row = lambda i: (i, 0)
    fixed = lambda i: (0, 0)
    return pl.pallas_call(
        _merge_body,
        grid=(t // tm,),
        in_specs=[
            pl.BlockSpec((tm, D_MODEL), row),
            pl.BlockSpec((tm, GATE_COLS), row),
            pl.BlockSpec((tm, RWKV_WIDTH), row),
            pl.BlockSpec((tm, DIFF_V_WIDTH), row),
            pl.BlockSpec((RWKV_WIDTH, D_MODEL), fixed),
            pl.BlockSpec((DIFF_V_WIDTH, D_MODEL), fixed),
            pl.BlockSpec((D_MODEL, D_MODEL), fixed),
            pl.BlockSpec((1, D_MODEL), fixed),
            pl.BlockSpec((D_MODEL, 2 * LANES), fixed),
            pl.BlockSpec((1, LANES), fixed),
        ],
        out_specs=[
            pl.BlockSpec((tm, D_MODEL), row),
            pl.BlockSpec((tm * TILE_ROWS, LANES), row),
            pl.BlockSpec((tm, LANES), row),
        ],
        out_shape=[
            jax.ShapeDtypeStruct((t, D_MODEL), F32),
            jax.ShapeDtypeStruct((t * TILE_ROWS, LANES), F32),
            jax.ShapeDtypeStruct((t, LANES), F32),
        ],
        compiler_params=pltpu.CompilerParams(
            dimension_semantics=("parallel",), vmem_limit_bytes=VMEM_LIMIT),
        name="merge_router",
    )(x2, gates, o_rwkv, o_diff, w_br, w_bd, w_out, norm_ffn, w_route, b_route)


def _expert_body(e_ref, first_ref, next_ref, wslot_ref, rows_ref, h2_hbm, wg_hbm, wu_hbm, wd_hbm, yb_ref,
                 xbuf, wg_f32, wu_f32, wd_f32, wg_sc, wu_sc, wd_sc, gsem, wsem, *, n_steps):
    g = pl.program_id(0)

    def weight_copies(e, slot):
        return [pltpu.make_async_copy(w_hbm.at[e], buf.at[slot], wsem.at[slot])
                for w_hbm, buf in ((wg_hbm, wg_f32), (wu_hbm, wu_f32), (wd_hbm, wd_f32))]

    @pl.when(g == 0)
    def _():
        for cp in weight_copies(e_ref[0], wslot_ref[0]):
            cp.start()

    @pl.when(first_ref[g] == 1)
    def _():
        slot = wslot_ref[g]
        for cp in weight_copies(e_ref[g], slot):
            cp.wait()
        nxt = next_ref[g]

        @pl.when(nxt >= 0)
        def _():
            for cp in weight_copies(nxt, 1 - slot):
                cp.start()

        wg_sc[...] = wg_f32[slot].astype(BF16)
        wu_sc[...] = wu_f32[slot].astype(BF16)
        wd_sc[...] = wd_f32[slot].astype(BF16)

    def start_gather(block, buf):
        for r in range(ROW_BLOCK):
            tok = rows_ref[block * ROW_BLOCK + r]
            pltpu.make_async_copy(_tile(h2_hbm, tok), _tile(xbuf, buf * ROW_BLOCK + r),
                                  gsem.at[buf]).start(r % 2)

    def wait_gather(buf):
        for _ in range(ROW_BLOCK):
            pltpu.make_async_copy(_tile(h2_hbm, 0), _tile(xbuf, 0), gsem.at[buf]).wait()

    @pl.when(g == 0)
    def _():
        for buf in range(4):
            start_gather(buf, buf)

    for parity in range(2):
        @pl.when(g % 2 == parity)
        def _():
            for s in range(2):
                buf = 2 * parity + s
                wait_gather(buf)
                x = _load_token_tiles(xbuf, buf * ROW_BLOCK * TILE_ROWS, ROW_BLOCK).astype(BF16)
                start_gather(2 * g + s + 4, buf)
                gt = _dot(x, wg_sc[...])
                up = _dot(x, wu_sc[...])
                mid = (gt * jax.nn.sigmoid(gt) * up).astype(BF16)
                _store_token_tiles(yb_ref, s * ROW_BLOCK * TILE_ROWS, _dot(mid, wd_sc[...]))

    @pl.when(g == n_steps - 1)
    def _():
        for buf in range(4):
            wait_gather(buf)


def _experts(step_e, rows, h2, w_gate, w_up, w_down):
    n_steps = step_e.shape[0]
    idx = jnp.arange(n_steps, dtype=jnp.int32)
    first = jnp.concatenate([jnp.ones((1,), jnp.int32), (step_e[1:] != step_e[:-1]).astype(jnp.int32)])
    wslot = (jnp.cumsum(first) - 1) % 2
    change_at = jnp.where(first == 1, idx, n_steps)
    next_change = jnp.concatenate([lax.cummin(change_at, reverse=True)[1:], jnp.full((1,), n_steps, jnp.int32)])
    nxt = jnp.where(next_change < n_steps, step_e[jnp.minimum(next_change, n_steps - 1)], -1)
    any_hbm = pl.BlockSpec(memory_space=pl.ANY)
    grid_spec = pltpu.PrefetchScalarGridSpec(
        num_scalar_prefetch=5,
        grid=(n_steps,),
        in_specs=[any_hbm, any_hbm, any_hbm, any_hbm],
        out_specs=pl.BlockSpec((STEP_ROWS * TILE_ROWS, LANES), lambda g, *_: (g, 0)),
        scratch_shapes=[pltpu.VMEM((2 * STEP_ROWS * TILE_ROWS, LANES), F32),
                        pltpu.VMEM((2, D_MODEL, EXPERT_FF), F32), pltpu.VMEM((2, D_MODEL, EXPERT_FF), F32),
                        pltpu.VMEM((2, EXPERT_FF, D_MODEL), F32),
                        pltpu.VMEM((D_MODEL, EXPERT_FF), BF16), pltpu.VMEM((D_MODEL, EXPERT_FF), BF16),
                        pltpu.VMEM((EXPERT_FF, D_MODEL), BF16),
                        pltpu.SemaphoreType.DMA((4,)), pltpu.SemaphoreType.DMA((2,))],
    )
    return pl.pallas_call(
        functools.partial(_expert_body, n_steps=n_steps),
        grid_spec=grid_spec,
        out_shape=jax.ShapeDtypeStruct((n_steps * STEP_ROWS * TILE_ROWS, LANES), F32),
        compiler_params=pltpu.CompilerParams(
            dimension_semantics=("arbitrary",), vmem_limit_bytes=VMEM_LIMIT),
        name="experts",
    )(step_e, first, nxt.astype(jnp.int32), wslot.astype(jnp.int32), rows, h2, w_gate, w_up, w_down)


def _final_body(pos_ref, x1_ref, route_ref, p_ref, yb_hbm, np_ref, wpg_ref, wpp_ref, o_ref,
                ybuf, sem, *, tm, n_steps):
    i = pl.program_id(0)
    per_tile = TOP_K * tm

    def start_gather(tile_idx, s):
        for r in range(per_tile):
            row = pos_ref[tile_idx * per_tile + r]
            pltpu.make_async_copy(_tile(yb_hbm, row), _tile(ybuf, s * per_tile + r), sem.at[s]).start(r % 2)

    def wait_gather(s):
        for _ in range(per_tile):
            pltpu.make_async_copy(_tile(yb_hbm, 0), _tile(ybuf, 0), sem.at[s]).wait()

    @pl.when(i == 0)
    def _():
        start_gather(0, 0)
        start_gather(1, 1)

    for s in range(2):
        rows = pl.ds(s * tm, tm)
        wait_gather(s)
        y_first = _load_token_tiles(ybuf, s * per_tile * TILE_ROWS, tm)
        y_second = _load_token_tiles(ybuf, (s * per_tile + tm) * TILE_ROWS, tm)
        start_gather(2 * i + s + 2, s)
        route = route_ref[rows, :]
        x2 = x1_ref[rows, :] + route[:, 2:3] * y_first + route[:, 3:4] * y_second
        ms = jnp.mean(x2 * x2, axis=-1, keepdims=True)
        hn = (x2 * lax.rsqrt(ms + RMS_EPS) * np_ref[...]).astype(BF16)
        gate = jax.nn.sigmoid(_dot(hn, wpg_ref[...]))
        o_ref[rows, :] = x2 + gate * _dot(p_ref[rows, :].astype(BF16), wpp_ref[...])

    @pl.when(i == n_steps - 1)
    def _():
        for s in range(2):
            wait_gather(s)


def _final(pos, x1, route, p2, yb, norm_ple, w_pg, w_pp, tm):
    t = x1.shape[0]
    assert t % (2 * tm) == 0
    n_steps = t // (2 * tm)
    row = lambda i, pos: (i, 0)
    fixed = lambda i, pos: (0, 0)
    grid_spec = pltpu.PrefetchScalarGridSpec(
        num_scalar_prefetch=1,
        grid=(n_steps,),
        in_specs=[
            pl.BlockSpec((2 * tm, D_MODEL), row),
            pl.BlockSpec((2 * tm, LANES), row),
            pl.BlockSpec((2 * tm, PLE_DIM), row),
            pl.BlockSpec(memory_space=pl.ANY),
            pl.BlockSpec((1, D_MODEL), fixed),
            pl.BlockSpec((D_MODEL, D_MODEL), fixed),
            pl.BlockSpec((PLE_DIM, D_MODEL), fixed),
        ],
        out_specs=pl.BlockSpec((2 * tm, D_MODEL), row),
        scratch_shapes=[pltpu.VMEM((2 * TOP_K * tm * TILE_ROWS, LANES), F32), pltpu.SemaphoreType.DMA((2,))],
    )
    return pl.pallas_call(
        functools.partial(_final_body, tm=tm, n_steps=n_steps),
        grid_spec=grid_spec,
        out_shape=jax.ShapeDtypeStruct((t, D_MODEL), F32),
        compiler_params=pltpu.CompilerParams(
            dimension_semantics=("arbitrary",), vmem_limit_bytes=VMEM_LIMIT),
        name="combine_ple",
    )(pos, x1, route, p2, yb, norm_ple, w_pg, w_pp)


def _dispatch_plan(route, n_tokens, tm):
    eid = route[:, :TOP_K].astype(jnp.int32).reshape(-1)
    n_assign = n_tokens * TOP_K
    order = jnp.argsort(eid).astype(jnp.int32)
    rank = jnp.argsort(order).astype(jnp.int32)
    experts = jnp.arange(N_EXPERTS, dtype=jnp.int32)
    onehot = eid[:, None] == experts[None, :]
    counts = jnp.sum(onehot.astype(jnp.int32), axis=0)
    starts = jnp.cumsum(counts) - counts
    pcounts = ((counts + STEP_ROWS - 1) // STEP_ROWS) * STEP_ROWS
    pends = jnp.cumsum(pcounts)
    pstarts = pends - pcounts
    n_steps = (n_assign + N_EXPERTS * STEP_ROWS) // STEP_ROWS
    row0 = jnp.arange(n_steps, dtype=jnp.int32) * STEP_ROWS
    step_e = jnp.minimum(jnp.sum((pends[None, :] <= row0[:, None]).astype(jnp.int32), axis=1), N_EXPERTS - 1)
    offset = row0 - pstarts[step_e]
    valid = counts[step_e] - offset
    first = jnp.clip(starts[step_e] + offset, 0, n_assign - 1)
    r = jnp.arange(STEP_ROWS, dtype=jnp.int32)[None, :]
    idx = jnp.minimum(first[:, None] + r, n_assign - 1)
    rows = jnp.where(r < valid[:, None], order[idx] // TOP_K, 0).reshape(-1)
    rows = jnp.concatenate([rows, jnp.zeros((2 * STEP_ROWS,), jnp.int32)])
    shift = jnp.sum(jnp.where(onehot, (pstarts - starts)[None, :], 0), axis=1)
    pos = (rank + shift).reshape(n_tokens // tm, tm, TOP_K).transpose(0, 2, 1).reshape(-1)
    pos = jnp.concatenate([pos, jnp.zeros((2 * TOP_K * tm,), jnp.int32)])
    return step_e, rows, pos


def _rope_tables(seq):
    inv = ROPE_THETA ** (-jnp.arange(0, HEAD_DIM, 2, dtype=F32) / HEAD_DIM)
    ang = jnp.arange(seq, dtype=F32)[:, None] * inv[None, :]
    ang = jnp.concatenate([ang, ang], axis=-1)
    reps = DIFF_QK_WIDTH // HEAD_DIM
    return jnp.tile(jnp.cos(ang), (1, reps)), jnp.tile(jnp.sin(ang), (1, reps))


def _tile_plan(seq):
    return dict(
        proj=min(512, seq),
        combine=256,
        rwkv=min(512, seq),
        attn=min(256, seq),
    )


def _layer(x2, p2, batch, seq, lam_init, prm):
    t = x2.shape[0]
    tiles = _tile_plan(seq)
    tm = tiles['proj']
    cos, sin = _rope_tables(seq)
    reps = DIFF_QK_WIDTH // HEAD_DIM
    prw, q, k, v, gates = _inproj(
        x2, prm['norm_mix'][None], prm['w_in'].astype(BF16), cos, sin,
        jnp.tile(prm['q_norm'], reps)[None], jnp.tile(prm['k_norm'], reps)[None], seq, tm)

    zpad = jnp.zeros((DECAY_LORA, RWKV_WIDTH), F32)
    w2p = jnp.concatenate([prm['rwkv_w2'], zpad], axis=0).astype(BF16)
    a2p = jnp.concatenate([zpad, prm['rwkv_a2']], axis=0).astype(BF16)
    o_rwkv = _rwkv(
        prw.reshape(batch, seq, RWKV_COLS), prm['rwkv_mu'][None], prm['rwkv_w0'][None], w2p,
        prm['rwkv_a0'][None], a2p, prm['rwkv_g2'].astype(BF16), prm['rwkv_k_k'][None],
        prm['rwkv_k_a'][None], prm['rwkv_r_k'].reshape(1, RWKV_WIDTH), prm['rwkv_ln_w'][None],
        prm['rwkv_ln_b'][None], tiles['rwkv']).reshape(t, RWKV_WIDTH)

    o_diff = _diff_attention(
        q, k, v, prm['lambda_q1'][None], prm['lambda_k1'][None], prm['lambda_q2'][None],
        prm['lambda_k2'][None], prm['subln_w'][None], batch, seq, lam_init, tiles['attn'])

    w_r = jnp.concatenate([prm['w_group'], prm['w_expert_router']], axis=1)
    w_r = jnp.pad(w_r, ((0, 0), (0, LANES - ROUTE_COLS)))
    w_r_hi = w_r.astype(BF16)
    w_r_lo = (w_r - w_r_hi.astype(F32)).astype(BF16)
    b_r = jnp.pad(jnp.concatenate([prm['b_group'], prm['b_expert_router']]), (0, LANES - ROUTE_COLS))[None]
    x1, h2, route = _merge(
        x2, gates, o_rwkv, o_diff, prm['w_branch_rwkv'].astype(BF16), prm['w_branch_diff'].astype(BF16),
        prm['w_out'].astype(BF16), prm['norm_ffn'][None], jnp.concatenate([w_r_hi, w_r_lo], axis=1), b_r, tm)

    step_e, rows, pos = _dispatch_plan(route, t, tiles['combine'])
    yb = _experts(step_e, rows, h2, prm['w_gate'], prm['w_up'], prm['w_down'])
    return _final(pos, x1, route, p2, yb, prm['norm_ple'][None], prm['w_ple_gate'].astype(BF16),
                  prm['w_ple_proj'].astype(BF16), tiles['combine'])


_PARAM_NAMES = (
    'norm_mix', 'w_in', 'rwkv_mu', 'rwkv_w0', 'rwkv_w2', 'rwkv_a0', 'rwkv_a2', 'rwkv_g2', 'rwkv_k_k',
    'rwkv_k_a', 'rwkv_r_k', 'rwkv_ln_w', 'rwkv_ln_b', 'q_norm', 'k_norm', 'lambda_q1', 'lambda_k1',
    'lambda_q2', 'lambda_k2', 'subln_w', 'w_branch_rwkv', 'w_branch_diff', 'w_out', 'norm_ffn',
    'w_group', 'b_group', 'w_expert_router', 'b_expert_router', 'w_gate', 'w_up', 'w_down', 'norm_ple',
    'w_ple_gate', 'w_ple_proj')


def kernel(x, p, norm_mix, w_in, rwkv_mu, rwkv_w0, rwkv_w2, rwkv_a0, rwkv_a2, rwkv_g2, rwkv_k_k,
           rwkv_k_a, rwkv_r_k, rwkv_ln_w, rwkv_ln_b, q_norm, k_norm, lambda_q1, lambda_k1, lambda_q2,
           lambda_k2, subln_w, w_branch_rwkv, w_branch_diff, w_out, norm_ffn, w_group, b_group,
           w_expert_router, b_expert_router, w_gate, w_up, w_down, norm_ple, w_ple_gate, w_ple_proj):
    stacked = dict(zip(_PARAM_NAMES, (
        norm_mix, w_in, rwkv_mu, rwkv_w0, rwkv_w2, rwkv_a0, rwkv_a2, rwkv_g2, rwkv_k_k, rwkv_k_a,
        rwkv_r_k, rwkv_ln_w, rwkv_ln_b, q_norm, k_norm, lambda_q1, lambda_k1, lambda_q2, lambda_k2,
        subln_w, w_branch_rwkv, w_branch_diff, w_out, norm_ffn, w_group, b_group, w_expert_router,
        b_expert_router, w_gate, w_up, w_down, norm_ple, w_ple_gate, w_ple_proj)))
    batch, seq, _ = x.shape
    depth = p.shape[0]
    x2 = x.reshape(batch * seq, D_MODEL)
    for layer in range(depth):
        lam_init = 0.8 - 0.6 * math.exp(-0.3 * layer)
        prm = {name: val[layer] for name, val in stacked.items()}
        x2 = _layer(x2, p[layer].reshape(batch * seq, PLE_DIM), batch, seq, lam_init, prm)
    return x2.reshape(batch, seq, D_MODEL)
```

```python
import functools
import math

import jax
import jax.numpy as jnp
from jax import lax
from jax.experimental import pallas as pl
from jax.experimental.pallas import tpu as pltpu

F32 = jnp.float32
BF16 = jnp.bfloat16

D_MODEL = 1024
PLE_DIM = 256
RMS_EPS = 1e-6

RWKV_HEADS = 8
HEAD_DIM = 64
RWKV_WIDTH = RWKV_HEADS * HEAD_DIM
DECAY_LORA = 64
AAA_LORA = 64
GATE_LORA = 128
GN_EPS = 64e-5
RWKV_COLS = 3 * RWKV_WIDTH + DECAY_LORA + AAA_LORA + GATE_LORA

DIFF_HEADS = 4
DIFF_QK_WIDTH = DIFF_HEADS * 2 * HEAD_DIM
DIFF_V_DIM = 2 * HEAD_DIM
DIFF_V_WIDTH = DIFF_HEADS * DIFF_V_DIM
DIFF_COLS = 2 * DIFF_QK_WIDTH + DIFF_V_WIDTH
ROPE_THETA = 10000.0
GATE_COLS = 2 * D_MODEL

N_GROUPS = 4
EXPERTS_PER_GROUP = 8
N_EXPERTS = N_GROUPS * EXPERTS_PER_GROUP
TOP_K = 2
EXPERT_FF = 512
ROW_BLOCK = 128
STEP_ROWS = 2 * ROW_BLOCK

LANES = 128
CHUNK = 64
VMEM_LIMIT = 56 * 1024 * 1024


def _dot(a, b):
    return jnp.dot(a, b, preferred_element_type=F32)


def _dot_nt(a, b):
    return lax.dot_general(a, b, (((1,), (1,)), ((), ())), preferred_element_type=F32)


SUBLANES = 8
TILE_ROWS = D_MODEL // LANES
assert TILE_ROWS == SUBLANES


def _store_token_tiles(ref, base, val, pitch=TILE_ROWS):
    n = val.shape[0]
    for j in range(TILE_ROWS):
        ref[pl.ds(base + j, n, stride=pitch), :] = val[:, j * LANES:(j + 1) * LANES]


def _load_token_tiles(ref, base, n, pitch=TILE_ROWS):
    return jnp.concatenate([ref[pl.ds(base + j, n, stride=pitch), :] for j in range(TILE_ROWS)], axis=1)


def _tile(ref, index):
    return ref.at[pl.ds(pl.multiple_of(index * TILE_ROWS, TILE_ROWS), TILE_ROWS)]


MXU_DIM = 256


def _group_ones(width, group):
    i = jnp.arange(width) // group
    return (i[:, None] == i[None, :]).astype(BF16)


def _head_sums(t, ones):
    t16 = t.astype(BF16)
    return jnp.concatenate([_dot(t16[:, i:i + MXU_DIM], ones) for i in range(0, t.shape[1], MXU_DIM)], axis=1)


def _rotate_half(t):
    width = t.shape[-1]
    lane = lax.broadcasted_iota(jnp.int32, t.shape, 1)
    fwd = pltpu.roll(t, width - HEAD_DIM // 2, 1)
    bwd = pltpu.roll(t, HEAD_DIM // 2, 1)
    return jnp.where(lane % HEAD_DIM < HEAD_DIM // 2, -fwd, bwd)


def _inproj_body(x_ref, g_ref, w_ref, cos_ref, sin_ref, qg_ref, kg_ref, ones_ref,
                 prw_ref, q_ref, k_ref, v_ref, gate_ref):
    x = x_ref[...]
    ms = jnp.mean(x * x, axis=-1, keepdims=True)
    h = (x * lax.rsqrt(ms + RMS_EPS) * g_ref[...]).astype(BF16)
    prw_ref[...] = _dot(h, w_ref[:, :RWKV_COLS])
    pd = _dot(h, w_ref[:, RWKV_COLS:RWKV_COLS + DIFF_COLS])
    cos = cos_ref[...]
    sin = sin_ref[...]

    def qk_prep(t, gain, scale):
        ssq = _head_sums(t * t, ones_ref[...])
        t = t * lax.rsqrt(ssq * (1.0 / HEAD_DIM) + RMS_EPS) * gain
        return ((t * cos + _rotate_half(t) * sin) * scale).astype(BF16)

    q_ref[...] = qk_prep(pd[:, :DIFF_QK_WIDTH], qg_ref[...], HEAD_DIM ** -0.5 * math.log2(math.e))
    k_ref[...] = qk_prep(pd[:, DIFF_QK_WIDTH:2 * DIFF_QK_WIDTH], kg_ref[...], 1.0)
    v_ref[...] = pd[:, 2 * DIFF_QK_WIDTH:].astype(BF16)
    gate_ref[...] = jax.nn.sigmoid(_dot(h, w_ref[:, RWKV_COLS + DIFF_COLS:])).astype(BF16)


def _inproj(x2, norm_g, w_in, cos, sin, q_gain, k_gain, seq, tm):
    t = x2.shape[0]
    n_seq_tiles = seq // tm
    row = lambda i: (i, 0)
    fixed = lambda i: (0, 0)
    rope = lambda i: (i % n_seq_tiles, 0)
    return pl.pallas_call(
        _inproj_body,
        grid=(t // tm,),
        in_specs=[
            pl.BlockSpec((tm, D_MODEL), row),
            pl.BlockSpec((1, D_MODEL), fixed),
            pl.BlockSpec((D_MODEL, RWKV_COLS + DIFF_COLS + GATE_COLS), fixed),
            pl.BlockSpec((tm, DIFF_QK_WIDTH), rope),
            pl.BlockSpec((tm, DIFF_QK_WIDTH), rope),
            pl.BlockSpec((1, DIFF_QK_WIDTH), fixed),
            pl.BlockSpec((1, DIFF_QK_WIDTH), fixed),
            pl.BlockSpec((MXU_DIM, MXU_DIM), fixed),
        ],
        out_specs=[
            pl.BlockSpec((tm, RWKV_COLS), row),
            pl.BlockSpec((tm, DIFF_QK_WIDTH), row),
            pl.BlockSpec((tm, DIFF_QK_WIDTH), row),
            pl.BlockSpec((tm, DIFF_V_WIDTH), row),
            pl.BlockSpec((tm, GATE_COLS), row),
        ],
        out_shape=[
            jax.ShapeDtypeStruct((t, RWKV_COLS), F32),
            jax.ShapeDtypeStruct((t, DIFF_QK_WIDTH), BF16),
            jax.ShapeDtypeStruct((t, DIFF_QK_WIDTH), BF16),
            jax.ShapeDtypeStruct((t, DIFF_V_WIDTH), BF16),
            jax.ShapeDtypeStruct((t, GATE_COLS), BF16),
        ],
        compiler_params=pltpu.CompilerParams(
            dimension_semantics=("parallel",), vmem_limit_bytes=VMEM_LIMIT),
        name="inproj",
    )(x2, norm_g, w_in, cos, sin, q_gain, k_gain, _group_ones(MXU_DIM, HEAD_DIM))


def _attn_body(q_ref, *refs, tq, wide, lam_init):
    def tile(qi, carry):
        _attn_tile(qi, q_ref, *refs, tq=tq, wide=wide, lam_init=lam_init)
        return carry

    lax.fori_loop(0, q_ref.shape[0] // tq, tile, 0)


def _attn_tile(qi, q_ref, k_ref, v_ref, lq1_ref, lk1_ref, lq2_ref, lk2_ref, sw_ref, o_ref,
               qs_sc, m_sc, l_sc, acc_sc, s0_sc, s1_sc, *, tq, wide, lam_init):
    q_rows = pl.ds(pl.multiple_of(qi * tq, tq), tq)
    heads = range(q_ref.shape[1] // LANES)
    cols = [slice(h * LANES, (h + 1) * LANES) for h in heads]
    for h in heads:
        q = q_ref[q_rows, cols[h]]
        lane = lax.broadcasted_iota(jnp.int32, q.shape, 1)
        qs_sc[h, :tq, :] = jnp.where(lane < HEAD_DIM, q, jnp.zeros_like(q))
        qs_sc[h, tq:, :] = jnp.where(lane >= HEAD_DIM, q, jnp.zeros_like(q))
    m_sc[...] = jnp.full(m_sc.shape, -jnp.inf, F32)
    l_sc[...] = jnp.zeros(l_sc.shape, F32)
    acc_sc[...] = jnp.zeros(acc_sc.shape, F32)

    width = wide * tq

    def scores(j, s_sc):
        for h in heads:
            kb = k_ref[pl.ds(pl.multiple_of(j * width, width), width), cols[h]]
            s_sc[h] = _dot_nt(kb, qs_sc[h])

    def update(j, s_sc, masked):
        for h in heads:
            s = s_sc[h]
            if masked:
                key = lax.broadcasted_iota(jnp.int32, s.shape, 0) + j * width
                query = lax.broadcasted_iota(jnp.int32, s.shape, 1) % tq + qi * tq
                s = jnp.where(key <= query, s, -jnp.inf)
            vb = v_ref[pl.ds(pl.multiple_of(j * width, width), width), cols[h]]
            m_prev = m_sc[h]
            m_new = jnp.maximum(m_prev, jnp.max(s, axis=0, keepdims=True))
            alpha = jnp.exp2(m_prev - m_new)
            p = jnp.exp2(s - m_new)
            l_sc[h] = alpha * l_sc[h] + jnp.sum(p, axis=0, keepdims=True)
            acc_sc[h] = alpha * acc_sc[h] + _dot_tn(vb, p.astype(BF16))
            m_sc[h] = m_new

    last = qi // wide
    pairs = last // 2
    scores(0, s0_sc)

    def body(jj, carry):
        scores(2 * jj + 1, s1_sc)
        update(2 * jj, s0_sc, False)
        scores(2 * jj + 2, s0_sc)
        update(2 * jj + 1, s1_sc, False)
        return carry

    lax.fori_loop(0, pairs, body, 0)

    @pl.when(last % 2 == 1)
    def _():
        scores(last, s1_sc)
        update(last - 1, s0_sc, False)
        update(last, s1_sc, True)

    @pl.when(last % 2 == 0)
    def _():
        update(last, s0_sc, True)

    lam = (jnp.exp(jnp.sum(lq1_ref[...] * lk1_ref[...], axis=-1, keepdims=True))
           - jnp.exp(jnp.sum(lq2_ref[...] * lk2_ref[...], axis=-1, keepdims=True)) + lam_init)
    for h in heads:
        o = acc_sc[h] / l_sc[h]
        o = o[:, :tq] - lam * o[:, tq:]
        ms = jnp.mean(o * o, axis=0, keepdims=True)
        o = (o * lax.rsqrt(ms + RMS_EPS)).T
        o_ref[q_rows, cols[h]] = o * sw_ref[...] * (1.0 - lam_init)


def _diff_attention(q, k, v, lq1, lk1, lq2, lk2, subln_w, batch, seq, lam_init, tq):
    t = q.shape[0]
    nq = seq // tq
    vec = pl.BlockSpec((1, HEAD_DIM), lambda b, h: (0, 0))
    wide = max(1, min(4, nq // 2))
    assert nq % wide == 0
    hpg = 2
    seq_block = pl.BlockSpec((seq, hpg * LANES), lambda b, h: (b, h))
    return pl.pallas_call(
        functools.partial(_attn_body, tq=tq, wide=wide, lam_init=lam_init),
        grid=(batch, DIFF_HEADS // hpg),
        in_specs=[
            seq_block, seq_block, seq_block,
            vec, vec, vec, vec,
            pl.BlockSpec((1, DIFF_V_DIM), lambda b, h: (0, 0)),
        ],
        out_specs=seq_block,
        out_shape=jax.ShapeDtypeStruct((t, DIFF_V_WIDTH), F32),
        scratch_shapes=[
            pltpu.VMEM((hpg, 2 * tq, LANES), BF16),
            pltpu.VMEM((hpg, 1, 2 * tq), F32),
            pltpu.VMEM((hpg, 1, 2 * tq), F32),
            pltpu.VMEM((hpg, DIFF_V_DIM, 2 * tq), F32),
            pltpu.VMEM((hpg, wide * tq, 2 * tq), F32),
            pltpu.VMEM((hpg, wide * tq, 2 * tq), F32),
        ],
        compiler_params=pltpu.CompilerParams(
            dimension_semantics=("parallel", "parallel"),
            vmem_limit_bytes=VMEM_LIMIT),
        name="diff_attn",
    )(q, k, v, lq1, lk1, lq2, lk2, subln_w)


def _dot_tn(a, b):
    return lax.dot_general(a, b, (((0,), (0,)), ((), ())), preferred_element_type=F32)


def _wkv_prep(problems):
    c = CHUNK
    lane = lax.broadcasted_iota(jnp.int32, (c, LANES), 1)
    m0 = lane < HEAD_DIM
    row = lax.broadcasted_iota(jnp.int32, (LANES, LANES), 0)
    col = lax.broadcasted_iota(jnp.int32, (LANES, LANES), 1)
    same = (row // c) == (col // c)
    strict = same & ((col % c) < (row % c))
    incl = same & ((col % c) <= (row % c))
    eye = row == col

    def stack(t):
        zero = jnp.zeros_like(t)
        return jnp.concatenate([jnp.where(m0, t, zero), jnp.where(m0, zero, t)], axis=0)

    def dup(t):
        return jnp.concatenate([t, t], axis=0)

    def fold(t):
        return t[:c] + t[c:]

    n = len(problems)
    xs = [_dot_nt(jnp.concatenate([q['kkt'], q['rt']], axis=0),
                  jnp.concatenate([stack(q['bt']), stack(q['kt'])], axis=0)) for q in problems]
    l_pow = [jnp.where(strict, dup(x[:c, :LANES]), 0.0) for x in xs]
    m_ak = [jnp.where(strict, dup(x[:c, LANES:]), 0.0).astype(BF16) for x in xs]
    a_rbk = [jnp.concatenate([jnp.where(incl, dup(x[c:, :LANES]), 0.0),
                              jnp.where(incl, dup(x[c:, LANES:]), 0.0)], axis=1).astype(BF16) for x in xs]
    t_inv = [jnp.where(eye, 1.0, 0.0) - l for l in l_pow]
    for _ in range(5):
        l16 = [l.astype(BF16) for l in l_pow]
        l_pow = [_dot(l, l) for l in l16]
        upd = [_dot(t.astype(BF16), l.astype(BF16)) for t, l in zip(t_inv, l_pow)]
        t_inv = [t + u for t, u in zip(t_inv, upd)]

    vs = [stack(q['v']) for q in problems]
    mv = [_dot(m, v) for m, v in zip(m_ak, vs)]
    z = [_dot(t.astype(BF16), jnp.concatenate([stack(q['kkt']), m.astype(BF16)], axis=1))
         for t, q, m in zip(t_inv, problems, mv)]
    zero_sq = jnp.zeros((LANES, LANES), BF16)
    qy = [_dot(a, jnp.concatenate([(-zz).astype(BF16), jnp.concatenate([zero_sq, v], axis=1)], axis=0))
          for a, zz, v in zip(a_rbk, z, vs)]
    zero_c = jnp.zeros((c, LANES), BF16)
    pd = [_dot_tn(jnp.concatenate([q['bh'], q['kh']], axis=0),
                  jnp.concatenate([jnp.concatenate([fold(-zz[:, :LANES]), fold(-zz[:, LANES:])], axis=1).astype(BF16),
                                   jnp.concatenate([zero_c, q['v']], axis=1)], axis=0))
          for q, zz in zip(problems, z)]
    out = []
    for i in range(n):
        q = problems[i]
        qs = stack(q['rt']).astype(F32) + qy[i][:, :LANES]
        phi = (jnp.where(eye, jnp.broadcast_to(q['pc'], (LANES, LANES)), 0.0)
               + jnp.where(same, pd[i][:, :LANES], 0.0))
        delta = jnp.where(same, pd[i][:, LANES:], 0.0)
        qphi = jnp.concatenate([fold(qs), phi], axis=0).astype(BF16)
        out.append((qphi, fold(qy[i][:, LANES:]), delta))
    return out


def _rwkv_body(pr_ref, halo_ref, mu_ref, w0_ref, w2_ref, a0_ref, a2_ref, g2_ref, kk_ref, ka_ref,
               rk_ref, lnw_ref, lnb_ref, ones_ref, tri_ref, o_ref,
               hp_sc, rt_sc, kt_sc, bt_sc, kkt_sc, kh_sc, bh_sc, v_sc, pc_sc, qphi_sc, delta_sc, y_sc,
               *, rows):
    i = pl.program_id(1)
    n_pairs = RWKV_WIDTH // LANES
    n_chunks = rows // CHUNK

    @pl.when(i == 0)
    def _():
        hp_sc[...] = jnp.zeros(hp_sc.shape, F32)

    pr = pr_ref[0]
    first = jnp.where(i == 0, 0.0, halo_ref[0, 7:8, :])
    prev = jnp.concatenate([first, pr[:-1]], axis=0)
    xs = pr + (prev - pr) * mu_ref[...]
    w = RWKV_WIDTH
    r = xs[:, :w]
    k = xs[:, w:2 * w]
    v = xs[:, 2 * w:3 * w]
    x_lora = xs[:, 3 * w:3 * w + DECAY_LORA + AAA_LORA]
    xg = xs[:, 3 * w + DECAY_LORA + AAA_LORA:]
    wl = w0_ref[...] + _dot(jnp.tanh(x_lora).astype(BF16), w2_ref[...])
    sp = jnp.maximum(-wl, 0.0) + jnp.log(1.0 + jnp.exp(-jnp.abs(wl)))
    lw = -jnp.exp(-sp - 0.5)
    a = jax.nn.sigmoid(a0_ref[...] + _dot(x_lora.astype(BF16), a2_ref[...]))
    g = _dot(jax.nn.sigmoid(xg).astype(BF16), g2_ref[...])
    ones = ones_ref[...]
    kx = k * kk_ref[...]
    ssq = _head_sums(kx * kx, ones)
    kn = kx / jnp.maximum(jnp.sqrt(ssq), 1e-12)
    km = k * (1.0 + (a - 1.0) * ka_ref[...])
    bonus = _head_sums(r * km * rk_ref[...], ones) * v

    hi = lw.astype(BF16)
    r1 = lw - hi.astype(F32)
    mid = r1.astype(BF16)
    lo = (r1 - mid.astype(F32)).astype(BF16)
    tri = tri_ref[...]
    parts = jnp.concatenate([hi, mid, lo], axis=1)
    cum = jnp.concatenate([_dot(tri, parts[i:i + MXU_DIM]) for i in range(0, rows, MXU_DIM)], axis=0)
    cum = cum[:, :w] + cum[:, w:2 * w] + cum[:, 2 * w:]

    cum3 = cum.reshape(n_chunks, CHUNK, w)
    end3 = cum3[:, CHUNK - 1:CHUNK, :]
    p_inv = jnp.exp(-cum)
    p_end = jnp.exp(end3 - cum3).reshape(rows, w)
    b = kn * a
    rt_sc[...] = (r * jnp.exp(cum)).astype(BF16)
    kt_sc[...] = (km * p_inv).astype(BF16)
    bt_sc[...] = (b * p_inv).astype(BF16)
    kkt_sc[...] = (kn * jnp.exp(cum - lw)).astype(BF16)
    kh_sc[...] = (km * p_end).astype(BF16)
    bh_sc[...] = (b * p_end).astype(BF16)
    v_sc[...] = v.astype(BF16)
    pc_sc[...] = jnp.broadcast_to(jnp.exp(end3), pc_sc.shape)

    group = 4 if n_chunks % 4 == 0 else 1

    def prep_chunks(gi, carry):
        where = []
        for g in range(group):
            ci = gi * group + g
            rs = pl.ds(pl.multiple_of(ci * CHUNK, CHUNK), CHUNK)
            for p in range(n_pairs):
                where.append((ci, rs, p, slice(p * LANES, (p + 1) * LANES)))
        problems = [dict(rt=rt_sc[rs, sl], kt=kt_sc[rs, sl], bt=bt_sc[rs, sl], kkt=kkt_sc[rs, sl],
                         kh=kh_sc[rs, sl], bh=bh_sc[rs, sl], v=v_sc[rs, sl], pc=pc_sc[ci, 0:1, sl])
                    for ci, rs, p, sl in where]
        for (ci, rs, p, sl), (qphi, y0, delta) in zip(where, _wkv_prep(problems)):
            qphi_sc[ci, p] = qphi
            delta_sc[ci, p] = delta
            y_sc[rs, sl] = y0
        return carry

    lax.fori_loop(0, n_chunks // group, prep_chunks, 0)

    def scan_chunk(ci, carry):
        rs = pl.ds(pl.multiple_of(ci * CHUNK, CHUNK), CHUNK)
        for p in range(n_pairs):
            sl = slice(p * LANES, (p + 1) * LANES)
            res = _dot(qphi_sc[ci, p], hp_sc[p].astype(BF16))
            y_sc[rs, sl] += res[:CHUNK]
            hp_sc[p] = res[CHUNK:] + delta_sc[ci, p]
        return carry

    lax.fori_loop(0, n_chunks, scan_chunk, 0, unroll=True)

    y = y_sc[...]
    mean = _head_sums(y, ones) * (1.0 / HEAD_DIM)
    yc = y - mean
    var = _head_sums(yc * yc, ones) * (1.0 / HEAD_DIM)
    yn = yc * lax.rsqrt(var + GN_EPS) * lnw_ref[...] + lnb_ref[...]
    o_ref[0] = ((yn + bonus) * g).astype(o_ref.dtype)


def _rwkv(pr3, mu, w0, w2p, a0, a2p, g2, k_k, k_a, r_k, ln_w, ln_b, rows):
    batch, seq, _ = pr3.shape
    nb = seq // rows
    fixed = lambda b, i: (0, 0)
    vec = pl.BlockSpec((1, RWKV_WIDTH), fixed)
    lora = DECAY_LORA + AAA_LORA
    n_pairs = RWKV_WIDTH // LANES
    assert rows % MXU_DIM == 0 and MXU_DIM % CHUNK == 0
    ci = jnp.arange(MXU_DIM) // CHUNK
    ti = jnp.arange(MXU_DIM)
    tri = ((ci[:, None] == ci[None, :]) & (ti[None, :] <= ti[:, None])).astype(BF16)
    big = pltpu.VMEM((rows, RWKV_WIDTH), F32)
    n_chunks = rows // CHUNK
    return pl.pallas_call(
        functools.partial(_rwkv_body, rows=rows),
        grid=(batch, nb),
        in_specs=[
            pl.BlockSpec((1, rows, RWKV_COLS), lambda b, i: (b, i, 0)),
            pl.BlockSpec((1, 8, RWKV_COLS), lambda b, i: (b, jnp.maximum(i * (rows // 8) - 1, 0), 0)),
            pl.BlockSpec((1, RWKV_COLS), fixed),
            vec,
            pl.BlockSpec((lora, RWKV_WIDTH), fixed),
            vec,
            pl.BlockSpec((lora, RWKV_WIDTH), fixed),
            pl.BlockSpec((GATE_LORA, RWKV_WIDTH), fixed),
            vec, vec, vec, vec, vec,
            pl.BlockSpec((MXU_DIM, MXU_DIM), fixed),
            pl.BlockSpec((MXU_DIM, MXU_DIM), fixed),
        ],
        out_specs=pl.BlockSpec((1, rows, RWKV_WIDTH), lambda b, i: (b, i, 0)),
        out_shape=jax.ShapeDtypeStruct((batch, seq, RWKV_WIDTH), F32),
        scratch_shapes=(
            [pltpu.VMEM((n_pairs, LANES, LANES), F32)]
            + [pltpu.VMEM((rows, RWKV_WIDTH), BF16)] * 7
            + [pltpu.VMEM((n_chunks, 8, RWKV_WIDTH), F32),
               pltpu.VMEM((n_chunks, n_pairs, CHUNK + LANES, LANES), BF16),
               pltpu.VMEM((n_chunks, n_pairs, LANES, LANES), F32),
               big]),
        compiler_params=pltpu.CompilerParams(
            dimension_semantics=("parallel", "arbitrary"), vmem_limit_bytes=VMEM_LIMIT),
        name="rwkv7",
    )(pr3, pr3, mu, w0, w2p, a0, a2p, g2, k_k, k_a, r_k, ln_w, ln_b,
      _group_ones(MXU_DIM, HEAD_DIM), tri)


ROUTE_COLS = N_GROUPS + N_EXPERTS


def _merge_body(x_ref, gate_ref, orw_ref, odf_ref, wbr_ref, wbd_ref, wout_ref, nf_ref, wr_ref,
                br_ref, x1_ref, h2_ref, route_ref):
    gates = gate_ref[...].astype(F32)
    mixed = (gates[:, :D_MODEL] * _dot(orw_ref[...].astype(BF16), wbr_ref[...])
             + gates[:, D_MODEL:] * _dot(odf_ref[...].astype(BF16), wbd_ref[...]))
    x1 = x_ref[...] + _dot(mixed.astype(BF16), wout_ref[...])
    x1_ref[...] = x1
    ms = jnp.mean(x1 * x1, axis=-1, keepdims=True)
    h2 = x1 * lax.rsqrt(ms + RMS_EPS) * nf_ref[...]
    _store_token_tiles(h2_ref, 0, h2)

    hi = h2.astype(BF16)
    lo = (h2 - hi.astype(F32)).astype(BF16)
    acc = _dot(hi, wr_ref[...])
    logits = acc[:, :LANES] + acc[:, LANES:] + _dot(lo, wr_ref[:, :LANES]) + br_ref[...]
    lane = lax.broadcasted_iota(jnp.int32, logits.shape, 1).astype(F32)
    neg = -jnp.inf

    def top(vals):
        m = jnp.max(vals, axis=-1, keepdims=True)
        return m, jnp.min(jnp.where(vals == m, lane, float(LANES)), axis=-1, keepdims=True)

    gl = jnp.where(lane < N_GROUPS, logits, neg)
    gm, g_idx = top(gl)
    g_top = 1.0 / jnp.sum(jnp.exp(gl - gm), axis=-1, keepdims=True)
    first = N_GROUPS + EXPERTS_PER_GROUP * g_idx
    el = jnp.where((lane >= first) & (lane < first + EXPERTS_PER_GROUP), logits, neg)
    t1, i1 = top(el)
    t2, i2 = top(jnp.where(lane == i1, neg, el))
    e2 = jnp.exp(t2 - t1)
    w1 = g_top / (1.0 + e2)
    w2 = g_top * e2 / (1.0 + e2)
    route_ref[...] = jnp.where(lane == 0, i1 - N_GROUPS, jnp.where(lane == 1, i2 - N_GROUPS,
                               jnp.where(lane == 2, w1, jnp.where(lane == 3, w2, 0.0))))


def _merge(x2, gates, o_rwkv, o_diff, w_br, w_bd, w_out, norm_ffn, w_route, b_route, tm):
    t = x2.shape[0]
    row = lambda i: (i, 0)
    fixed = lambda i: (0, 0)
    return pl.pallas_call(
        _merge_body,
        grid=(t // tm,),
        in_specs=[
            pl.BlockSpec((tm, D_MODEL), row),
            pl.BlockSpec((tm, GATE_COLS), row),
            pl.BlockSpec((tm, RWKV_WIDTH), row),
            pl.BlockSpec((tm, DIFF_V_WIDTH), row),
            pl.BlockSpec((RWKV_WIDTH, D_MODEL), fixed),
            pl.BlockSpec((DIFF_V_WIDTH, D_MODEL), fixed),
            pl.BlockSpec((D_MODEL, D_MODEL), fixed),
            pl.BlockSpec((1, D_MODEL), fixed),
            pl.BlockSpec((D_MODEL, 2 * LANES), fixed),
            pl.BlockSpec((1, LANES), fixed),
        ],
        out_specs=[
            pl.BlockSpec((tm, D_MODEL), row),
            pl.BlockSpec((tm * TILE_ROWS, LANES), row),
            pl.BlockSpec((tm, LANES), row),
        ],
        out_shape=[
            jax.ShapeDtypeStruct((t, D_MODEL), F32),
            jax.ShapeDtypeStruct((t * TILE_ROWS, LANES), F32),
            jax.ShapeDtypeStruct((t, LANES), F32),
        ],
        compiler_params=pltpu.CompilerParams(
            dimension_semantics=("parallel",), vmem_limit_bytes=VMEM_LIMIT),
        name="merge_router",
    )(x2, gates, o_rwkv, o_diff, w_br, w_bd, w_out, norm_ffn, w_route, b_route)


def _expert_body(e_ref, first_ref, next_ref, wslot_ref, rows_ref, h2_hbm, wg_hbm, wu_hbm, wd_hbm, yb_hbm,
                 xbuf, obuf, wg_f32, wu_f32, wd_f32, wg_sc, wu_sc, wd_sc, gsem, wsem, osem, *, n_steps):
    step_tiles = STEP_ROWS * TILE_ROWS

    def weight_copies(e, slot):
        return [pltpu.make_async_copy(w_hbm.at[e], buf.at[slot], wsem.at[slot])
                for w_hbm, buf in ((wg_hbm, wg_f32), (wu_hbm, wu_f32), (wd_hbm, wd_f32))]

    def start_gather(block, buf):
        for r in range(ROW_BLOCK):
            tok = rows_ref[block * ROW_BLOCK + r]
            pltpu.make_async_copy(_tile(h2_hbm, tok), _tile(xbuf, buf * ROW_BLOCK + r),
                                  gsem.at[buf]).start(r % 2)

    def wait_gather(buf):
        for _ in range(ROW_BLOCK):
            pltpu.make_async_copy(_tile(h2_hbm, 0), _tile(xbuf, 0), gsem.at[buf]).wait()

    def writeback(g, parity):
        return pltpu.make_async_copy(
            obuf.at[pl.ds(parity * step_tiles, step_tiles)],
            yb_hbm.at[pl.ds(pl.multiple_of(g * step_tiles, step_tiles), step_tiles)], osem.at[parity])

    for cp in weight_copies(e_ref[0], wslot_ref[0]):
        cp.start()
    for buf in range(4):
        start_gather(buf, buf)

    def step(g, parity):
        @pl.when(first_ref[g] == 1)
        def _():
            slot = wslot_ref[g]
            for cp in weight_copies(e_ref[g], slot):
                cp.wait()
            nxt = next_ref[g]

            @pl.when(nxt >= 0)
            def _():
                for cp in weight_copies(nxt, 1 - slot):
                    cp.start()

            wg_sc[...] = wg_f32[slot].astype(BF16)
            wu_sc[...] = wu_f32[slot].astype(BF16)
            wd_sc[...] = wd_f32[slot].astype(BF16)

        @pl.when(g >= 2)
        def _():
            writeback(g - 2, parity).wait()

        for s in range(2):
            buf = 2 * parity + s
            wait_gather(buf)
            x = _load_token_tiles(xbuf, buf * ROW_BLOCK * TILE_ROWS, ROW_BLOCK).astype(BF16)
            start_gather(2 * g + s + 4, buf)
            gt = _dot(x, wg_sc[...])
            up = _dot(x, wu_sc[...])
            mid = (gt * jax.nn.sigmoid(gt) * up).astype(BF16)
            _store_token_tiles(obuf, parity * step_tiles + s * ROW_BLOCK * TILE_ROWS, _dot(mid, wd_sc[...]))
        writeback(g, parity).start()

    def step_pair(i, carry):
        step(2 * i, 0)
        step(2 * i + 1, 1)
        return carry

    lax.fori_loop(0, n_steps // 2, step_pair, 0)
    for buf in range(4):
        wait_gather(buf)
    for parity in range(2):
        writeback(n_steps - 2 + parity, parity).wait()


def _experts(step_e, rows, h2, w_gate, w_up, w_down):
    n_steps = step_e.shape[0]
    idx = jnp.arange(n_steps, dtype=jnp.int32)
    first = jnp.concatenate([jnp.ones((1,), jnp.int32), (step_e[1:] != step_e[:-1]).astype(jnp.int32)])
    wslot = (jnp.cumsum(first) - 1) % 2
    change_at = jnp.where(first == 1, idx, n_steps)
    next_change = jnp.concatenate([lax.cummin(change_at, reverse=True)[1:], jnp.full((1,), n_steps, jnp.int32)])
    at_next = idx[None, :] == next_change[:, None]
    nxt = jnp.where(next_change < n_steps, jnp.sum(jnp.where(at_next, step_e[None, :], 0), axis=1), -1)
    assert n_steps % 2 == 0
    any_hbm = pl.BlockSpec(memory_space=pl.ANY)
    grid_spec = pltpu.PrefetchScalarGridSpec(
        num_scalar_prefetch=5,
        grid=(1,),
        in_specs=[any_hbm, any_hbm, any_hbm, any_hbm],
        out_specs=any_hbm,
        scratch_shapes=[pltpu.VMEM((2 * STEP_ROWS * TILE_ROWS, LANES), F32),
                        pltpu.VMEM((2 * STEP_ROWS * TILE_ROWS, LANES), F32),
                        pltpu.VMEM((2, D_MODEL, EXPERT_FF), F32), pltpu.VMEM((2, D_MODEL, EXPERT_FF), F32),
                        pltpu.VMEM((2, EXPERT_FF, D_MODEL), F32),
                        pltpu.VMEM((D_MODEL, EXPERT_FF), BF16), pltpu.VMEM((D_MODEL, EXPERT_FF), BF16),
                        pltpu.VMEM((EXPERT_FF, D_MODEL), BF16),
                        pltpu.SemaphoreType.DMA((4,)), pltpu.SemaphoreType.DMA((2,)),
                        pltpu.SemaphoreType.DMA((2,))],
    )
    return pl.pallas_call(
        functools.partial(_expert_body, n_steps=n_steps),
        grid_spec=grid_spec,
        out_shape=jax.ShapeDtypeStruct((n_steps * STEP_ROWS * TILE_ROWS, LANES), F32),
        compiler_params=pltpu.CompilerParams(
            dimension_semantics=("arbitrary",), vmem_limit_bytes=VMEM_LIMIT),
        name="experts",
    )(step_e, first, nxt.astype(jnp.int32), wslot.astype(jnp.int32), rows, h2, w_gate, w_up, w_down)


def _final_body(pos_ref, x1_ref, route_ref, p_ref, yb_hbm, np_ref, wpg_ref, wpp_ref, o_ref,
                ybuf, sem, *, tm, n_steps):
    i = pl.program_id(0)
    per_tile = TOP_K * tm

    def start_gather(tile_idx, s):
        for r in range(per_tile):
            row = pos_ref[tile_idx * per_tile + r]
            pltpu.make_async_copy(_tile(yb_hbm, row), _tile(ybuf, s * per_tile + r), sem.at[s]).start(r % 2)

    def wait_gather(s):
        for _ in range(per_tile):
            pltpu.make_async_copy(_tile(yb_hbm, 0), _tile(ybuf, 0), sem.at[s]).wait()

    @pl.when(i == 0)
    def _():
        start_gather(0, 0)
        start_gather(1, 1)

    for s in range(2):
        rows = pl.ds(s * tm, tm)
        wait_gather(s)
        y_first = _load_token_tiles(ybuf, s * per_tile * TILE_ROWS, tm)
        y_second = _load_token_tiles(ybuf, (s * per_tile + tm) * TILE_ROWS, tm)
        start_gather(2 * i + s + 2, s)
        route = route_ref[rows, :]
        x2 = x1_ref[rows, :] + route[:, 2:3] * y_first + route[:, 3:4] * y_second
        ms = jnp.mean(x2 * x2, axis=-1, keepdims=True)
        hn = (x2 * lax.rsqrt(ms + RMS_EPS) * np_ref[...]).astype(BF16)
        gate = jax.nn.sigmoid(_dot(hn, wpg_ref[...]))
        o_ref[rows, :] = x2 + gate * _dot(p_ref[rows, :].astype(BF16), wpp_ref[...])

    @pl.when(i == n_steps - 1)
    def _():
        for s in range(2):
            wait_gather(s)


def _final(pos, x1, route, p2, yb, norm_ple, w_pg, w_pp, tm):
    t = x1.shape[0]
    assert t % (2 * tm) == 0
    n_steps = t // (2 * tm)
    row = lambda i, pos: (i, 0)
    fixed = lambda i, pos: (0, 0)
    grid_spec = pltpu.PrefetchScalarGridSpec(
        num_scalar_prefetch=1,
        grid=(n_steps,),
        in_specs=[
            pl.BlockSpec((2 * tm, D_MODEL), row),
            pl.BlockSpec((2 * tm, LANES), row),
            pl.BlockSpec((2 * tm, PLE_DIM), row),
            pl.BlockSpec(memory_space=pl.ANY),
            pl.BlockSpec((1, D_MODEL), fixed),
            pl.BlockSpec((D_MODEL, D_MODEL), fixed),
            pl.BlockSpec((PLE_DIM, D_MODEL), fixed),
        ],
        out_specs=pl.BlockSpec((2 * tm, D_MODEL), row),
        scratch_shapes=[pltpu.VMEM((2 * TOP_K * tm * TILE_ROWS, LANES), F32), pltpu.SemaphoreType.DMA((2,))],
    )
    return pl.pallas_call(
        functools.partial(_final_body, tm=tm, n_steps=n_steps),
        grid_spec=grid_spec,
        out_shape=jax.ShapeDtypeStruct((t, D_MODEL), F32),
        compiler_params=pltpu.CompilerParams(
            dimension_semantics=("arbitrary",), vmem_limit_bytes=VMEM_LIMIT),
        name="combine_ple",
    )(pos, x1, route, p2, yb, norm_ple, w_pg, w_pp)


def _dispatch_plan(route, n_tokens, tm):
    eid = route[:, :TOP_K].astype(jnp.int32).reshape(-1)
    n_assign = n_tokens * TOP_K
    order = jnp.argsort(eid).astype(jnp.int32)
    rank = jnp.argsort(order).astype(jnp.int32)
    experts = jnp.arange(N_EXPERTS, dtype=jnp.int32)
    onehot = eid[:, None] == experts[None, :]
    counts = jnp.sum(onehot.astype(jnp.int32), axis=0)
    starts = jnp.cumsum(counts) - counts
    pcounts = ((counts + STEP_ROWS - 1) // STEP_ROWS) * STEP_ROWS
    pends = jnp.cumsum(pcounts)
    pstarts = pends - pcounts
    n_steps = (n_assign + N_EXPERTS * STEP_ROWS) // STEP_ROWS
    row0 = jnp.arange(n_steps, dtype=jnp.int32) * STEP_ROWS
    step_e = jnp.minimum(jnp.sum((pends[None, :] <= row0[:, None]).astype(jnp.int32), axis=1), N_EXPERTS - 1)
    offset = row0 - pstarts[step_e]
    valid = counts[step_e] - offset
    first = jnp.clip(starts[step_e] + offset, 0, n_assign - 1)
    r = jnp.arange(STEP_ROWS, dtype=jnp.int32)[None, :]
    idx = jnp.minimum(first[:, None] + r, n_assign - 1)
    rows = jnp.where(r < valid[:, None], order[idx] // TOP_K, 0).reshape(-1)
    rows = jnp.concatenate([rows, jnp.zeros((2 * STEP_ROWS,), jnp.int32)])
    shift = jnp.sum(jnp.where(onehot, (pstarts - starts)[None, :], 0), axis=1)
    pos = (rank + shift).reshape(n_tokens // tm, tm, TOP_K).transpose(0, 2, 1).reshape(-1)
    pos = jnp.concatenate([pos, jnp.zeros((2 * TOP_K * tm,), jnp.int32)])
    return step_e, rows, pos


def _rope_tables(seq):
    inv = ROPE_THETA ** (-jnp.arange(0, HEAD_DIM, 2, dtype=F32) / HEAD_DIM)
    ang = jnp.arange(seq, dtype=F32)[:, None] * inv[None, :]
    ang = jnp.concatenate([ang, ang], axis=-1)
    reps = DIFF_QK_WIDTH // HEAD_DIM
    return jnp.tile(jnp.cos(ang), (1, reps)), jnp.tile(jnp.sin(ang), (1, reps))


def _tile_plan(seq):
    return dict(
        proj=min(512, seq),
        combine=256,
        rwkv=min(512, seq),
        attn=min(256, seq),
    )


def _layer(x2, p2, batch, seq, lam_init, prm):
    t = x2.shape[0]
    tiles = _tile_plan(seq)
    tm = tiles['proj']
    cos, sin = _rope_tables(seq)
    reps = DIFF_QK_WIDTH // HEAD_DIM
    prw, q, k, v, gates = _inproj(
        x2, prm['norm_mix'][None], prm['w_in'].astype(BF16), cos, sin,
        jnp.tile(prm['q_norm'], reps)[None], jnp.tile(prm['k_norm'], reps)[None], seq, tm)

    zpad = jnp.zeros((DECAY_LORA, RWKV_WIDTH), F32)
    w2p = jnp.concatenate([prm['rwkv_w2'], zpad], axis=0).astype(BF16)
    a2p = jnp.concatenate([zpad, prm['rwkv_a2']], axis=0).astype(BF16)
    o_rwkv = _rwkv(
        prw.reshape(batch, seq, RWKV_COLS), prm['rwkv_mu'][None], prm['rwkv_w0'][None], w2p,
        prm['rwkv_a0'][None], a2p, prm['rwkv_g2'].astype(BF16), prm['rwkv_k_k'][None],
        prm['rwkv_k_a'][None], prm['rwkv_r_k'].reshape(1, RWKV_WIDTH), prm['rwkv_ln_w'][None],
        prm['rwkv_ln_b'][None], tiles['rwkv']).reshape(t, RWKV_WIDTH)

    o_diff = _diff_attention(
        q, k, v, prm['lambda_q1'][None], prm['lambda_k1'][None], prm['lambda_q2'][None],
        prm['lambda_k2'][None], prm['subln_w'][None], batch, seq, lam_init, tiles['attn'])

    w_r = jnp.concatenate([prm['w_group'], prm['w_expert_router']], axis=1)
    w_r = jnp.pad(w_r, ((0, 0), (0, LANES - ROUTE_COLS)))
    w_r_hi = w_r.astype(BF16)
    w_r_lo = (w_r - w_r_hi.astype(F32)).astype(BF16)
    b_r = jnp.pad(jnp.concatenate([prm['b_group'], prm['b_expert_router']]), (0, LANES - ROUTE_COLS))[None]
    x1, h2, route = _merge(
        x2, gates, o_rwkv, o_diff, prm['w_branch_rwkv'].astype(BF16), prm['w_branch_diff'].astype(BF16),
        prm['w_out'].astype(BF16), prm['norm_ffn'][None], jnp.concatenate([w_r_hi, w_r_lo], axis=1), b_r, tm)

    step_e, rows, pos = _dispatch_plan(route, t, tiles['combine'])
    yb = _experts(step_e, rows, h2, prm['w_gate'], prm['w_up'], prm['w_down'])
    return _final(pos, x1, route, p2, yb, prm['norm_ple'][None], prm['w_ple_gate'].astype(BF16),
                  prm['w_ple_proj'].astype(BF16), tiles['combine'])


_PARAM_NAMES = (
    'norm_mix', 'w_in', 'rwkv_mu', 'rwkv_w0', 'rwkv_w2', 'rwkv_a0', 'rwkv_a2', 'rwkv_g2', 'rwkv_k_k',
    'rwkv_k_a', 'rwkv_r_k', 'rwkv_ln_w', 'rwkv_ln_b', 'q_norm', 'k_norm', 'lambda_q1', 'lambda_k1',
    'lambda_q2', 'lambda_k2', 'subln_w', 'w_branch_rwkv', 'w_branch_diff', 'w_out', 'norm_ffn',
    'w_group', 'b_group', 'w_expert_router', 'b_expert_router', 'w_gate', 'w_up', 'w_down', 'norm_ple',
    'w_ple_gate', 'w_ple_proj')


def kernel(x, p, norm_mix, w_in, rwkv_mu, rwkv_w0, rwkv_w2, rwkv_a0, rwkv_a2, rwkv_g2, rwkv_k_k,
           rwkv_k_a, rwkv_r_k, rwkv_ln_w, rwkv_ln_b, q_norm, k_norm, lambda_q1, lambda_k1, lambda_q2,
           lambda_k2, subln_w, w_branch_rwkv, w_branch_diff, w_out, norm_ffn, w_group, b_group,
           w_expert_router, b_expert_router, w_gate, w_up, w_down, norm_ple, w_ple_gate, w_ple_proj):
    stacked = dict(zip(_PARAM_NAMES, (
        norm_mix, w_in, rwkv_mu, rwkv_w0, rwkv_w2, rwkv_a0, rwkv_a2, rwkv_g2, rwkv_k_k, rwkv_k_a,
        rwkv_r_k, rwkv_ln_w, rwkv_ln_b, q_norm, k_norm, lambda_q1, lambda_k1, lambda_q2, lambda_k2,
        subln_w, w_branch_rwkv, w_branch_diff, w_out, norm_ffn, w_group, b_group, w_expert_router,
        b_expert_router, w_gate, w_up, w_down, norm_ple, w_ple_gate, w_ple_proj)))
    batch, seq, _ = x.shape
    depth = p.shape[0]
    x2 = x.reshape(batch * seq, D_MODEL)
    for layer in range(depth):
        lam_init = 0.8 - 0.6 * math.exp(-0.3 * layer)
        prm = {name: val[layer] for name, val in stacked.items()}
        x2 = _layer(x2, p[layer].reshape(batch * seq, PLE_DIM), batch, seq, lam_init, prm)
    return x2.reshape(batch, seq, D_MODEL)
```

```python
import functools
import math

import jax
import jax.numpy as jnp
from jax import lax
from jax.experimental import pallas as pl
from jax.experimental.pallas import tpu as pltpu

F32 = jnp.float32
BF16 = jnp.bfloat16

D_MODEL = 1024
PLE_DIM = 256
RMS_EPS = 1e-6

RWKV_HEADS = 8
HEAD_DIM = 64
RWKV_WIDTH = RWKV_HEADS * HEAD_DIM
DECAY_LORA = 64
AAA_LORA = 64
GATE_LORA = 128
GN_EPS = 64e-5
RWKV_COLS = 3 * RWKV_WIDTH + DECAY_LORA + AAA_LORA + GATE_LORA

DIFF_HEADS = 4
DIFF_QK_WIDTH = DIFF_HEADS * 2 * HEAD_DIM
DIFF_V_DIM = 2 * HEAD_DIM
DIFF_V_WIDTH = DIFF_HEADS * DIFF_V_DIM
DIFF_COLS = 2 * DIFF_QK_WIDTH + DIFF_V_WIDTH
ROPE_THETA = 10000.0
GATE_COLS = 2 * D_MODEL

N_GROUPS = 4
EXPERTS_PER_GROUP = 8
N_EXPERTS = N_GROUPS * EXPERTS_PER_GROUP
TOP_K = 2
EXPERT_FF = 512
ROW_BLOCK = 128
STEP_ROWS = 2 * ROW_BLOCK

LANES = 128
CHUNK = 64
VMEM_LIMIT = 56 * 1024 * 1024


def _dot(a, b):
    return jnp.dot(a, b, preferred_element_type=F32)


def _dot_nt(a, b):
    return lax.dot_general(a, b, (((1,), (1,)), ((), ())), preferred_element_type=F32)


SUBLANES = 8
TILE_ROWS = D_MODEL // LANES
assert TILE_ROWS == SUBLANES


def _store_token_tiles(ref, base, val, pitch=TILE_ROWS):
    n = val.shape[0]
    for j in range(TILE_ROWS):
        ref[pl.ds(base + j, n, stride=pitch), :] = val[:, j * LANES:(j + 1) * LANES]


def _load_token_tiles(ref, base, n, pitch=TILE_ROWS):
    return jnp.concatenate([ref[pl.ds(base + j, n, stride=pitch), :] for j in range(TILE_ROWS)], axis=1)


def _tile(ref, index):
    return ref.at[pl.ds(pl.multiple_of(index * TILE_ROWS, TILE_ROWS), TILE_ROWS)]


MXU_DIM = 256


def _group_ones(width, group):
    i = jnp.arange(width) // group
    return (i[:, None] == i[None, :]).astype(BF16)


def _head_sums(t, ones):
    t16 = t.astype(BF16)
    return jnp.concatenate([_dot(t16[:, i:i + MXU_DIM], ones) for i in range(0, t.shape[1], MXU_DIM)], axis=1)


def _rotate_half(t):
    width = t.shape[-1]
    lane = lax.broadcasted_iota(jnp.int32, t.shape, 1)
    fwd = pltpu.roll(t, width - HEAD_DIM // 2, 1)
    bwd = pltpu.roll(t, HEAD_DIM // 2, 1)
    return jnp.where(lane % HEAD_DIM < HEAD_DIM // 2, -fwd, bwd)


def _inproj_body(x_ref, g_ref, w_ref, cos_ref, sin_ref, qg_ref, kg_ref, ones_ref,
                 prw_ref, q_ref, k_ref, v_ref, gate_ref):
    x = x_ref[...]
    ms = jnp.mean(x * x, axis=-1, keepdims=True)
    h = (x * lax.rsqrt(ms + RMS_EPS) * g_ref[...]).astype(BF16)
    prw_ref[...] = _dot(h, w_ref[:, :RWKV_COLS])
    pd = _dot(h, w_ref[:, RWKV_COLS:RWKV_COLS + DIFF_COLS])
    cos = cos_ref[...]
    sin = sin_ref[...]

    def qk_prep(t, gain, scale):
        ssq = _head_sums(t * t, ones_ref[...])
        t = t * lax.rsqrt(ssq * (1.0 / HEAD_DIM) + RMS_EPS) * gain
        return ((t * cos + _rotate_half(t) * sin) * scale).astype(BF16)

    q_ref[...] = qk_prep(pd[:, :DIFF_QK_WIDTH], qg_ref[...], HEAD_DIM ** -0.5 * math.log2(math.e))
    k_ref[...] = qk_prep(pd[:, DIFF_QK_WIDTH:2 * DIFF_QK_WIDTH], kg_ref[...], 1.0)
    v_ref[...] = pd[:, 2 * DIFF_QK_WIDTH:].astype(BF16)
    gate_ref[...] = jax.nn.sigmoid(_dot(h, w_ref[:, RWKV_COLS + DIFF_COLS:])).astype(BF16)


def _inproj(x2, norm_g, w_in, cos, sin, q_gain, k_gain, seq, tm):
    t = x2.shape[0]
    n_seq_tiles = seq // tm
    row = lambda i: (i, 0)
    fixed = lambda i: (0, 0)
    rope = lambda i: (i % n_seq_tiles, 0)
    return pl.pallas_call(
        _inproj_body,
        grid=(t // tm,),
        in_specs=[
            pl.BlockSpec((tm, D_MODEL), row),
            pl.BlockSpec((1, D_MODEL), fixed),
            pl.BlockSpec((D_MODEL, RWKV_COLS + DIFF_COLS + GATE_COLS), fixed),
            pl.BlockSpec((tm, DIFF_QK_WIDTH), rope),
            pl.BlockSpec((tm, DIFF_QK_WIDTH), rope),
            pl.BlockSpec((1, DIFF_QK_WIDTH), fixed),
            pl.BlockSpec((1, DIFF_QK_WIDTH), fixed),
            pl.BlockSpec((MXU_DIM, MXU_DIM), fixed),
        ],
        out_specs=[
            pl.BlockSpec((tm, RWKV_COLS), row),
            pl.BlockSpec((tm, DIFF_QK_WIDTH), row),
            pl.BlockSpec((tm, DIFF_QK_WIDTH), row),
            pl.BlockSpec((tm, DIFF_V_WIDTH), row),
            pl.BlockSpec((tm, GATE_COLS), row),
        ],
        out_shape=[
            jax.ShapeDtypeStruct((t, RWKV_COLS), F32),
            jax.ShapeDtypeStruct((t, DIFF_QK_WIDTH), BF16),
            jax.ShapeDtypeStruct((t, DIFF_QK_WIDTH), BF16),
            jax.ShapeDtypeStruct((t, DIFF_V_WIDTH), BF16),
            jax.ShapeDtypeStruct((t, GATE_COLS), BF16),
        ],
        compiler_params=pltpu.CompilerParams(
            dimension_semantics=("parallel",), vmem_limit_bytes=VMEM_LIMIT),
        name="inproj",
    )(x2, norm_g, w_in, cos, sin, q_gain, k_gain, _group_ones(MXU_DIM, HEAD_DIM))


def _attn_body(q_ref, *refs, tq, wide, lam_init):
    def tile(qi, carry):
        _attn_tile(qi, q_ref, *refs, tq=tq, wide=wide, lam_init=lam_init)
        return carry

    lax.fori_loop(0, q_ref.shape[0] // tq, tile, 0)


def _attn_tile(qi, q_ref, k_ref, v_ref, lq1_ref, lk1_ref, lq2_ref, lk2_ref, sw_ref, o_ref,
               qs_sc, m_sc, l_sc, acc_sc, s0_sc, s1_sc, *, tq, wide, lam_init):
    q_rows = pl.ds(pl.multiple_of(qi * tq, tq), tq)
    heads = range(q_ref.shape[1] // LANES)
    cols = [slice(h * LANES, (h + 1) * LANES) for h in heads]
    for h in heads:
        q = q_ref[q_rows, cols[h]]
        lane = lax.broadcasted_iota(jnp.int32, q.shape, 1)
        qs_sc[h, :tq, :] = jnp.where(lane < HEAD_DIM, q, jnp.zeros_like(q))
        qs_sc[h, tq:, :] = jnp.where(lane >= HEAD_DIM, q, jnp.zeros_like(q))
    m_sc[...] = jnp.full(m_sc.shape, -jnp.inf, F32)
    l_sc[...] = jnp.zeros(l_sc.shape, F32)
    acc_sc[...] = jnp.zeros(acc_sc.shape, F32)

    width = wide * tq

    def scores(j, s_sc):
        for h in heads:
            kb = k_ref[pl.ds(pl.multiple_of(j * width, width), width), cols[h]]
            s_sc[h] = _dot_nt(kb, qs_sc[h])

    def update(j, s_sc, masked):
        for h in heads:
            s = s_sc[h]
            if masked:
                key = lax.broadcasted_iota(jnp.int32, s.shape, 0) + j * width
                query = lax.broadcasted_iota(jnp.int32, s.shape, 1) % tq + qi * tq
                s = jnp.where(key <= query, s, -jnp.inf)
            vb = v_ref[pl.ds(pl.multiple_of(j * width, width), width), cols[h]]
            m_prev = m_sc[h]
            m_new = jnp.maximum(m_prev, jnp.max(s, axis=0, keepdims=True))
            alpha = jnp.exp2(m_prev - m_new)
            p = jnp.exp2(s - m_new)
            l_sc[h] = alpha * l_sc[h] + jnp.sum(p, axis=0, keepdims=True)
            acc_sc[h] = alpha * acc_sc[h] + _dot_tn(vb, p.astype(BF16))
            m_sc[h] = m_new

    last = qi // wide
    pairs = last // 2
    scores(0, s0_sc)

    def body(jj, carry):
        scores(2 * jj + 1, s1_sc)
        update(2 * jj, s0_sc, False)
        scores(2 * jj + 2, s0_sc)
        update(2 * jj + 1, s1_sc, False)
        return carry

    lax.fori_loop(0, pairs, body, 0)

    @pl.when(last % 2 == 1)
    def _():
        scores(last, s1_sc)
        update(last - 1, s0_sc, False)
        update(last, s1_sc, True)

    @pl.when(last % 2 == 0)
    def _():
        update(last, s0_sc, True)

    lam = (jnp.exp(jnp.sum(lq1_ref[...] * lk1_ref[...], axis=-1, keepdims=True))
           - jnp.exp(jnp.sum(lq2_ref[...] * lk2_ref[...], axis=-1, keepdims=True)) + lam_init)
    for h in heads:
        o = acc_sc[h] / l_sc[h]
        o = o[:, :tq] - lam * o[:, tq:]
        ms = jnp.mean(o * o, axis=0, keepdims=True)
        o = (o * lax.rsqrt(ms + RMS_EPS)).T
        o_ref[q_rows, cols[h]] = o * sw_ref[...] * (1.0 - lam_init)


def _diff_attention(q, k, v, lq1, lk1, lq2, lk2, subln_w, batch, seq, lam_init, tq):
    t = q.shape[0]
    nq = seq // tq
    vec = pl.BlockSpec((1, HEAD_DIM), lambda b, h: (0, 0))
    wide = max(1, min(4, nq // 2))
    assert nq % wide == 0
    hpg = 2
    seq_block = pl.BlockSpec((seq, hpg * LANES), lambda b, h: (b, h))
    return pl.pallas_call(
        functools.partial(_attn_body, tq=tq, wide=wide, lam_init=lam_init),
        grid=(batch, DIFF_HEADS // hpg),
        in_specs=[
            seq_block, seq_block, seq_block,
            vec, vec, vec, vec,
            pl.BlockSpec((1, DIFF_V_DIM), lambda b, h: (0, 0)),
        ],
        out_specs=seq_block,
        out_shape=jax.ShapeDtypeStruct((t, DIFF_V_WIDTH), F32),
        scratch_shapes=[
            pltpu.VMEM((hpg, 2 * tq, LANES), BF16),
            pltpu.VMEM((hpg, 1, 2 * tq), F32),
            pltpu.VMEM((hpg, 1, 2 * tq), F32),
            pltpu.VMEM((hpg, DIFF_V_DIM, 2 * tq), F32),
            pltpu.VMEM((hpg, wide * tq, 2 * tq), F32),
            pltpu.VMEM((hpg, wide * tq, 2 * tq), F32),
        ],
        compiler_params=pltpu.CompilerParams(
            dimension_semantics=("parallel", "parallel"),
            vmem_limit_bytes=VMEM_LIMIT),
        name="diff_attn",
    )(q, k, v, lq1, lk1, lq2, lk2, subln_w)


def _dot_tn(a, b):
    return lax.dot_general(a, b, (((0,), (0,)), ((), ())), preferred_element_type=F32)


def _wkv_prep(problems):
    c = CHUNK
    lane = lax.broadcasted_iota(jnp.int32, (c, LANES), 1)
    m0 = lane < HEAD_DIM
    row = lax.broadcasted_iota(jnp.int32, (LANES, LANES), 0)
    col = lax.broadcasted_iota(jnp.int32, (LANES, LANES), 1)
    same = (row // c) == (col // c)
    strict = same & ((col % c) < (row % c))
    incl = same & ((col % c) <= (row % c))
    eye = row == col

    def stack(t):
        zero = jnp.zeros_like(t)
        return jnp.concatenate([jnp.where(m0, t, zero), jnp.where(m0, zero, t)], axis=0)

    def dup(t):
        return jnp.concatenate([t, t], axis=0)

    def fold(t):
        return t[:c] + t[c:]

    n = len(problems)
    xs = [_dot_nt(jnp.concatenate([q['kkt'], q['rt']], axis=0),
                  jnp.concatenate([stack(q['bt']), stack(q['kt'])], axis=0)) for q in problems]
    l_pow = [jnp.where(strict, dup(x[:c, :LANES]), 0.0) for x in xs]
    m_ak = [jnp.where(strict, dup(x[:c, LANES:]), 0.0).astype(BF16) for x in xs]
    a_rbk = [jnp.concatenate([jnp.where(incl, dup(x[c:, :LANES]), 0.0),
                              jnp.where(incl, dup(x[c:, LANES:]), 0.0)], axis=1).astype(BF16) for x in xs]
    t_inv = [jnp.where(eye, 1.0, 0.0) - l for l in l_pow]
    for _ in range(5):
        l16 = [l.astype(BF16) for l in l_pow]
        l_pow = [_dot(l, l) for l in l16]
        upd = [_dot(t.astype(BF16), l.astype(BF16)) for t, l in zip(t_inv, l_pow)]
        t_inv = [t + u for t, u in zip(t_inv, upd)]

    vs = [stack(q['v']) for q in problems]
    mv = [_dot(m, v) for m, v in zip(m_ak, vs)]
    z = [_dot(t.astype(BF16), jnp.concatenate([stack(q['kkt']), m.astype(BF16)], axis=1))
         for t, q, m in zip(t_inv, problems, mv)]
    zero_sq = jnp.zeros((LANES, LANES), BF16)
    qy = [_dot(a, jnp.concatenate([(-zz).astype(BF16), jnp.concatenate([zero_sq, v], axis=1)], axis=0))
          for a, zz, v in zip(a_rbk, z, vs)]
    zero_c = jnp.zeros((c, LANES), BF16)
    pd = [_dot_tn(jnp.concatenate([q['bh'], q['kh']], axis=0),
                  jnp.concatenate([jnp.concatenate([fold(-zz[:, :LANES]), fold(-zz[:, LANES:])], axis=1).astype(BF16),
                                   jnp.concatenate([zero_c, q['v']], axis=1)], axis=0))
          for q, zz in zip(problems, z)]
    out = []
    for i in range(n):
        q = problems[i]
        qs = stack(q['rt']).astype(F32) + qy[i][:, :LANES]
        phi = (jnp.where(eye, jnp.broadcast_to(q['pc'], (LANES, LANES)), 0.0)
               + jnp.where(same, pd[i][:, :LANES], 0.0))
        delta = jnp.where(same, pd[i][:, LANES:], 0.0)
        qphi = jnp.concatenate([fold(qs), phi], axis=0).astype(BF16)
        out.append((qphi, fold(qy[i][:, LANES:]), delta))
    return out


def _rwkv_body(pr_ref, halo_ref, mu_ref, w0_ref, w2_ref, a0_ref, a2_ref, g2_ref, kk_ref, ka_ref,
               rk_ref, lnw_ref, lnb_ref, ones_ref, tri_ref, o_ref,
               hp_sc, rt_sc, kt_sc, bt_sc, kkt_sc, kh_sc, bh_sc, v_sc, pc_sc, qphi_sc, delta_sc, y_sc,
               *, rows):
    i = pl.program_id(1)
    n_pairs = RWKV_WIDTH // LANES
    n_chunks = rows // CHUNK

    @pl.when(i == 0)
    def _():
        hp_sc[...] = jnp.zeros(hp_sc.shape, F32)

    pr = pr_ref[0]
    first = jnp.where(i == 0, 0.0, halo_ref[0, 7:8, :])
    prev = jnp.concatenate([first, pr[:-1]], axis=0)
    xs = pr + (prev - pr) * mu_ref[...]
    w = RWKV_WIDTH
    r = xs[:, :w]
    k = xs[:, w:2 * w]
    v = xs[:, 2 * w:3 * w]
    x_lora = xs[:, 3 * w:3 * w + DECAY_LORA + AAA_LORA]
    xg = xs[:, 3 * w + DECAY_LORA + AAA_LORA:]
    wl = w0_ref[...] + _dot(jnp.tanh(x_lora).astype(BF16), w2_ref[...])
    sp = jnp.maximum(-wl, 0.0) + jnp.log(1.0 + jnp.exp(-jnp.abs(wl)))
    lw = -jnp.exp(-sp - 0.5)
    a = jax.nn.sigmoid(a0_ref[...] + _dot(x_lora.astype(BF16), a2_ref[...]))
    g = _dot(jax.nn.sigmoid(xg).astype(BF16), g2_ref[...])
    ones = ones_ref[...]
    kx = k * kk_ref[...]
    ssq = _head_sums(kx * kx, ones)
    kn = kx / jnp.maximum(jnp.sqrt(ssq), 1e-12)
    km = k * (1.0 + (a - 1.0) * ka_ref[...])
    bonus = _head_sums(r * km * rk_ref[...], ones) * v

    hi = lw.astype(BF16)
    r1 = lw - hi.astype(F32)
    mid = r1.astype(BF16)
    lo = (r1 - mid.astype(F32)).astype(BF16)
    tri = tri_ref[...]
    parts = jnp.concatenate([hi, mid, lo], axis=1)
    cum = jnp.concatenate([_dot(tri, parts[i:i + MXU_DIM]) for i in range(0, rows, MXU_DIM)], axis=0)
    cum = cum[:, :w] + cum[:, w:2 * w] + cum[:, 2 * w:]

    cum3 = cum.reshape(n_chunks, CHUNK, w)
    end3 = cum3[:, CHUNK - 1:CHUNK, :]
    p_inv = jnp.exp(-cum)
    p_end = jnp.exp(end3 - cum3).reshape(rows, w)
    b = kn * a
    rt_sc[...] = (r * jnp.exp(cum)).astype(BF16)
    kt_sc[...] = (km * p_inv).astype(BF16)
    bt_sc[...] = (b * p_inv).astype(BF16)
    kkt_sc[...] = (kn * jnp.exp(cum - lw)).astype(BF16)
    kh_sc[...] = (km * p_end).astype(BF16)
    bh_sc[...] = (b * p_end).astype(BF16)
    v_sc[...] = v.astype(BF16)
    pc_sc[...] = jnp.broadcast_to(jnp.exp(end3), pc_sc.shape)

    group = 4 if n_chunks % 4 == 0 else 1

    def prep_chunks(gi, carry):
        where = []
        for g in range(group):
            ci = gi * group + g
            rs = pl.ds(pl.multiple_of(ci * CHUNK, CHUNK), CHUNK)
            for p in range(n_pairs):
                where.append((ci, rs, p, slice(p * LANES, (p + 1) * LANES)))
        problems = [dict(rt=rt_sc[rs, sl], kt=kt_sc[rs, sl], bt=bt_sc[rs, sl], kkt=kkt_sc[rs, sl],
                         kh=kh_sc[rs, sl], bh=bh_sc[rs, sl], v=v_sc[rs, sl], pc=pc_sc[ci, 0:1, sl])
                    for ci, rs, p, sl in where]
        for (ci, rs, p, sl), (qphi, y0, delta) in zip(where, _wkv_prep(problems)):
            qphi_sc[ci, p] = qphi
            delta_sc[ci, p] = delta
            y_sc[rs, sl] = y0
        return carry

    lax.fori_loop(0, n_chunks // group, prep_chunks, 0)

    def scan_chunk(ci, carry):
        rs = pl.ds(pl.multiple_of(ci * CHUNK, CHUNK), CHUNK)
        for p in range(n_pairs):
            sl = slice(p * LANES, (p + 1) * LANES)
            res = _dot(qphi_sc[ci, p], hp_sc[p].astype(BF16))
            y_sc[rs, sl] += res[:CHUNK]
            hp_sc[p] = res[CHUNK:] + delta_sc[ci, p]
        return carry

    lax.fori_loop(0, n_chunks, scan_chunk, 0, unroll=True)

    y = y_sc[...]
    mean = _head_sums(y, ones) * (1.0 / HEAD_DIM)
    yc = y - mean
    var = _head_sums(yc * yc, ones) * (1.0 / HEAD_DIM)
    yn = yc * lax.rsqrt(var + GN_EPS) * lnw_ref[...] + lnb_ref[...]
    o_ref[0] = ((yn + bonus) * g).astype(o_ref.dtype)


def _rwkv(pr3, mu, w0, w2p, a0, a2p, g2, k_k, k_a, r_k, ln_w, ln_b, rows):
    batch, seq, _ = pr3.shape
    nb = seq // rows
    fixed = lambda b, i: (0, 0)
    vec = pl.BlockSpec((1, RWKV_WIDTH), fixed)
    lora = DECAY_LORA + AAA_LORA
    n_pairs = RWKV_WIDTH // LANES
    assert rows % MXU_DIM == 0 and MXU_DIM % CHUNK == 0
    ci = jnp.arange(MXU_DIM) // CHUNK
    ti = jnp.arange(MXU_DIM)
    tri = ((ci[:, None] == ci[None, :]) & (ti[None, :] <= ti[:, None])).astype(BF16)
    big = pltpu.VMEM((rows, RWKV_WIDTH), F32)
    n_chunks = rows // CHUNK
    return pl.pallas_call(
        functools.partial(_rwkv_body, rows=rows),
        grid=(batch, nb),
        in_specs=[
            pl.BlockSpec((1, rows, RWKV_COLS), lambda b, i: (b, i, 0)),
            pl.BlockSpec((1, 8, RWKV_COLS), lambda b, i: (b, jnp.maximum(i * (rows // 8) - 1, 0), 0)),
            pl.BlockSpec((1, RWKV_COLS), fixed),
            vec,
            pl.BlockSpec((lora, RWKV_WIDTH), fixed),
            vec,
            pl.BlockSpec((lora, RWKV_WIDTH), fixed),
            pl.BlockSpec((GATE_LORA, RWKV_WIDTH), fixed),
            vec, vec, vec, vec, vec,
            pl.BlockSpec((MXU_DIM, MXU_DIM), fixed),
            pl.BlockSpec((MXU_DIM, MXU_DIM), fixed),
        ],
        out_specs=pl.BlockSpec((1, rows, RWKV_WIDTH), lambda b, i: (b, i, 0)),
        out_shape=jax.ShapeDtypeStruct((batch, seq, RWKV_WIDTH), F32),
        scratch_shapes=(
            [pltpu.VMEM((n_pairs, LANES, LANES), F32)]
            + [pltpu.VMEM((rows, RWKV_WIDTH), BF16)] * 7
            + [pltpu.VMEM((n_chunks, 8, RWKV_WIDTH), F32),
               pltpu.VMEM((n_chunks, n_pairs, CHUNK + LANES, LANES), BF16),
               pltpu.VMEM((n_chunks, n_pairs, LANES, LANES), F32),
               big]),
        compiler_params=pltpu.CompilerParams(
            dimension_semantics=("parallel", "arbitrary"), vmem_limit_bytes=VMEM_LIMIT),
        name="rwkv7",
    )(pr3, pr3, mu, w0, w2p, a0, a2p, g2, k_k, k_a, r_k, ln_w, ln_b,
      _group_ones(MXU_DIM, HEAD_DIM), tri)


ROUTE_COLS = N_GROUPS + N_EXPERTS


def _merge_body(x_ref, gate_ref, orw_ref, odf_ref, wbr_ref, wbd_ref, wout_ref, nf_ref, wr_ref,
                br_ref, x1_ref, h2_ref, route_ref):
    gates = gate_ref[...].astype(F32)
    mixed = (gates[:, :D_MODEL] * _dot(orw_ref[...].astype(BF16), wbr_ref[...])
             + gates[:, D_MODEL:] * _dot(odf_ref[...].astype(BF16), wbd_ref[...]))
    x1 = x_ref[...] + _dot(mixed.astype(BF16), wout_ref[...])
    x1_ref[...] = x1
    ms = jnp.mean(x1 * x1, axis=-1, keepdims=True)
    h2 = x1 * lax.rsqrt(ms + RMS_EPS) * nf_ref[...]
    _store_token_tiles(h2_ref, 0, h2)

    hi = h2.astype(BF16)
    lo = (h2 - hi.astype(F32)).astype(BF16)
    acc = _dot(hi, wr_ref[...])
    logits = acc[:, :LANES] + acc[:, LANES:] + _dot(lo, wr_ref[:, :LANES]) + br_ref[...]
    lane = lax.broadcasted_iota(jnp.int32, logits.shape, 1).astype(F32)
    neg = -jnp.inf

    def top(vals):
        m = jnp.max(vals, axis=-1, keepdims=True)
        return m, jnp.min(jnp.where(vals == m, lane, float(LANES)), axis=-1, keepdims=True)

    gl = jnp.where(lane < N_GROUPS, logits, neg)
    gm, g_idx = top(gl)
    g_top = 1.0 / jnp.sum(jnp.exp(gl - gm), axis=-1, keepdims=True)
    first = N_GROUPS + EXPERTS_PER_GROUP * g_idx
    el = jnp.where((lane >= first) & (lane < first + EXPERTS_PER_GROUP), logits, neg)
    t1, i1 = top(el)
    t2, i2 = top(jnp.where(lane == i1, neg, el))
    e2 = jnp.exp(t2 - t1)
    w1 = g_top / (1.0 + e2)
    w2 = g_top * e2 / (1.0 + e2)
    route_ref[...] = jnp.where(lane == 0, i1 - N_GROUPS, jnp.where(lane == 1, i2 - N_GROUPS,
                               jnp.where(lane == 2, w1, jnp.where(lane == 3, w2, 0.0))))


def _merge(x2, gates, o_rwkv, o_diff, w_br, w_bd, w_out, norm_ffn, w_route, b_route, tm):
    t = x2.shape[0]
    row = lambda i: (i, 0)
    fixed = lambda i: (0, 0)
    return pl.pallas_call(
        _merge_body,
        grid=(t // tm,),
        in_specs=[
            pl.BlockSpec((tm, D_MODEL), row),
            pl.BlockSpec((tm, GATE_COLS), row),
            pl.BlockSpec((tm, RWKV_WIDTH), row),
            pl.BlockSpec((tm, DIFF_V_WIDTH), row),
            pl.BlockSpec((RWKV_WIDTH, D_MODEL), fixed),
            pl.BlockSpec((DIFF_V_WIDTH, D_MODEL), fixed),
            pl.BlockSpec((D_MODEL, D_MODEL), fixed),
            pl.BlockSpec((1, D_MODEL), fixed),
            pl.BlockSpec((D_MODEL, 2 * LANES), fixed),
            pl.BlockSpec((1, LANES), fixed),
        ],
        out_specs=[
            pl.BlockSpec((tm, D_MODEL), row),
            pl.BlockSpec((tm * TILE_ROWS, LANES), row),
            pl.BlockSpec((tm, LANES), row),
        ],
        out_shape=[
            jax.ShapeDtypeStruct((t, D_MODEL), F32),
            jax.ShapeDtypeStruct((t * TILE_ROWS, LANES), F32),
            jax.ShapeDtypeStruct((t, LANES), F32),
        ],
        compiler_params=pltpu.CompilerParams(
            dimension_semantics=("parallel",), vmem_limit_bytes=VMEM_LIMIT),
        name="merge_router",
    )(x2, gates, o_rwkv, o_diff, w_br, w_bd, w_out, norm_ffn, w_route, b_route)


def _expert_body(e_ref, first_ref, next_ref, wslot_ref, rows_ref, h2_hbm, wg_hbm, wu_hbm, wd_hbm, yb_hbm,
                 xbuf, obuf, wg_f32, wu_f32, wd_f32, wg_sc, wu_sc, wd_sc, gsem, wsem, osem, *, n_steps):
    step_tiles = STEP_ROWS * TILE_ROWS

    def weight_copies(e, slot):
        return [pltpu.make_async_copy(w_hbm.at[e], buf.at[slot], wsem.at[slot])
                for w_hbm, buf in ((wg_hbm, wg_f32), (wu_hbm, wu_f32), (wd_hbm, wd_f32))]

    def start_gather(block, buf):
        for r in range(ROW_BLOCK):
            tok = rows_ref[block * ROW_BLOCK + r]
            pltpu.make_async_copy(_tile(h2_hbm, tok), _tile(xbuf, buf * ROW_BLOCK + r),
                                  gsem.at[buf]).start(r % 2)

    def wait_gather(buf):
        for _ in range(ROW_BLOCK):
            pltpu.make_async_copy(_tile(h2_hbm, 0), _tile(xbuf, 0), gsem.at[buf]).wait()

    def writeback(g, parity):
        return pltpu.make_async_copy(
            obuf.at[pl.ds(parity * step_tiles, step_tiles)],
            yb_hbm.at[pl.ds(pl.multiple_of(g * step_tiles, step_tiles), step_tiles)], osem.at[parity])

    for cp in weight_copies(e_ref[0], wslot_ref[0]):
        cp.start()
    for buf in range(4):
        start_gather(buf, buf)

    def step(g, parity):
        @pl.when(first_ref[g] == 1)
        def _():
            slot = wslot_ref[g]
            for cp in weight_copies(e_ref[g], slot):
                cp.wait()
            nxt = next_ref[g]

            @pl.when(nxt >= 0)
            def _():
                for cp in weight_copies(nxt, 1 - slot):
                    cp.start()

            wg_sc[...] = wg_f32[slot].astype(BF16)
            wu_sc[...] = wu_f32[slot].astype(BF16)
            wd_sc[...] = wd_f32[slot].astype(BF16)

        @pl.when(g >= 2)
        def _():
            writeback(g - 2, parity).wait()

        xs = []
        for s in range(2):
            buf = 2 * parity + s
            wait_gather(buf)
            xs.append(_load_token_tiles(xbuf, buf * ROW_BLOCK * TILE_ROWS, ROW_BLOCK).astype(BF16))
            start_gather(2 * g + s + 4, buf)
        x = jnp.concatenate(xs, axis=0)
        gt = _dot(x, wg_sc[...])
        up = _dot(x, wu_sc[...])
        mid = (gt * jax.nn.sigmoid(gt) * up).astype(BF16)
        _store_token_tiles(obuf, parity * step_tiles, _dot(mid, wd_sc[...]))
        writeback(g, parity).start()

    def step_pair(i, carry):
        step(2 * i, 0)
        step(2 * i + 1, 1)
        return carry

    lax.fori_loop(0, n_steps // 2, step_pair, 0)
    for buf in range(4):
        wait_gather(buf)
    for parity in range(2):
        writeback(n_steps - 2 + parity, parity).wait()


def _experts(step_e, rows, h2, w_gate, w_up, w_down):
    n_steps = step_e.shape[0]
    idx = jnp.arange(n_steps, dtype=jnp.int32)
    first = jnp.concatenate([jnp.ones((1,), jnp.int32), (step_e[1:] != step_e[:-1]).astype(jnp.int32)])
    wslot = (jnp.cumsum(first) - 1) % 2
    change_at = jnp.where(first == 1, idx, n_steps)
    next_change = jnp.concatenate([lax.cummin(change_at, reverse=True)[1:], jnp.full((1,), n_steps, jnp.int32)])
    at_next = idx[None, :] == next_change[:, None]
    nxt = jnp.where(next_change < n_steps, jnp.sum(jnp.where(at_next, step_e[None, :], 0), axis=1), -1)
    assert n_steps % 2 == 0
    any_hbm = pl.BlockSpec(memory_space=pl.ANY)
    grid_spec = pltpu.PrefetchScalarGridSpec(
        num_scalar_prefetch=5,
        grid=(1,),
        in_specs=[any_hbm, any_hbm, any_hbm, any_hbm],
        out_specs=any_hbm,
        scratch_shapes=[pltpu.VMEM((2 * STEP_ROWS * TILE_ROWS, LANES), F32),
                        pltpu.VMEM((2 * STEP_ROWS * TILE_ROWS, LANES), F32),
                        pltpu.VMEM((2, D_MODEL, EXPERT_FF), F32), pltpu.VMEM((2, D_MODEL, EXPERT_FF), F32),
                        pltpu.VMEM((2, EXPERT_FF, D_MODEL), F32),
                        pltpu.VMEM((D_MODEL, EXPERT_FF), BF16), pltpu.VMEM((D_MODEL, EXPERT_FF), BF16),
                        pltpu.VMEM((EXPERT_FF, D_MODEL), BF16),
                        pltpu.SemaphoreType.DMA((4,)), pltpu.SemaphoreType.DMA((2,)),
                        pltpu.SemaphoreType.DMA((2,))],
    )
    return pl.pallas_call(
        functools.partial(_expert_body, n_steps=n_steps),
        grid_spec=grid_spec,
        out_shape=jax.ShapeDtypeStruct((n_steps * STEP_ROWS * TILE_ROWS, LANES), F32),
        compiler_params=pltpu.CompilerParams(
            dimension_semantics=("arbitrary",), vmem_limit_bytes=VMEM_LIMIT),
        name="experts",
    )(step_e, first, nxt.astype(jnp.int32), wslot.astype(jnp.int32), rows, h2, w_gate, w_up, w_down)


def _final_body(pos_ref, x1_ref, route_ref, p_ref, yb_hbm, np_ref, wpg_ref, wpp_ref, o_ref,
                ybuf, sem, *, tm, n_steps):
    i = pl.program_id(0)
    per_tile = TOP_K * tm

    def start_gather(tile_idx, s):
        for r in range(per_tile):
            row = pos_ref[tile_idx * per_tile + r]
            pltpu.make_async_copy(_tile(yb_hbm, row), _tile(ybuf, s * per_tile + r), sem.at[s]).start(r % 2)

    def wait_gather(s):
        for _ in range(per_tile):
            pltpu.make_async_copy(_tile(yb_hbm, 0), _tile(ybuf, 0), sem.at[s]).wait()

    @pl.when(i == 0)
    def _():
        start_gather(0, 0)
        start_gather(1, 1)

    for s in range(2):
        rows = pl.ds(s * tm, tm)
        wait_gather(s)
        y_first = _load_token_tiles(ybuf, s * per_tile * TILE_ROWS, tm)
        y_second = _load_token_tiles(ybuf, (s * per_tile + tm) * TILE_ROWS, tm)
        start_gather(2 * i + s + 2, s)
        route = route_ref[rows, :]
        x2 = x1_ref[rows, :] + route[:, 2:3] * y_first + route[:, 3:4] * y_second
        ms = jnp.mean(x2 * x2, axis=-1, keepdims=True)
        hn = (x2 * lax.rsqrt(ms + RMS_EPS) * np_ref[...]).astype(BF16)
        gate = jax.nn.sigmoid(_dot(hn, wpg_ref[...]))
        o_ref[rows, :] = x2 + gate * _dot(p_ref[rows, :].astype(BF16), wpp_ref[...])

    @pl.when(i == n_steps - 1)
    def _():
        for s in range(2):
            wait_gather(s)


def _final(pos, x1, route, p2, yb, norm_ple, w_pg, w_pp, tm):
    t = x1.shape[0]
    assert t % (2 * tm) == 0
    n_steps = t // (2 * tm)
    row = lambda i, pos: (i, 0)
    fixed = lambda i, pos: (0, 0)
    grid_spec = pltpu.PrefetchScalarGridSpec(
        num_scalar_prefetch=1,
        grid=(n_steps,),
        in_specs=[
            pl.BlockSpec((2 * tm, D_MODEL), row),
            pl.BlockSpec((2 * tm, LANES), row),
            pl.BlockSpec((2 * tm, PLE_DIM), row),
            pl.BlockSpec(memory_space=pl.ANY),
            pl.BlockSpec((1, D_MODEL), fixed),
            pl.BlockSpec((D_MODEL, D_MODEL), fixed),
            pl.BlockSpec((PLE_DIM, D_MODEL), fixed),
        ],
        out_specs=pl.BlockSpec((2 * tm, D_MODEL), row),
        scratch_shapes=[pltpu.VMEM((2 * TOP_K * tm * TILE_ROWS, LANES), F32), pltpu.SemaphoreType.DMA((2,))],
    )
    return pl.pallas_call(
        functools.partial(_final_body, tm=tm, n_steps=n_steps),
        grid_spec=grid_spec,
        out_shape=jax.ShapeDtypeStruct((t, D_MODEL), F32),
        compiler_params=pltpu.CompilerParams(
            dimension_semantics=("arbitrary",), vmem_limit_bytes=VMEM_LIMIT),
        name="combine_ple",
    )(pos, x1, route, p2, yb, norm_ple, w_pg, w_pp)


def _dispatch_plan(route, n_tokens, tm):
    eid = route[:, :TOP_K].astype(jnp.int32).reshape(-1)
    n_assign = n_tokens * TOP_K
    order = jnp.argsort(eid).astype(jnp.int32)
    rank = jnp.argsort(order).astype(jnp.int32)
    experts = jnp.arange(N_EXPERTS, dtype=jnp.int32)
    onehot = eid[:, None] == experts[None, :]
    counts = jnp.sum(onehot.astype(jnp.int32), axis=0)
    starts = jnp.cumsum(counts) - counts
    pcounts = ((counts + STEP_ROWS - 1) // STEP_ROWS) * STEP_ROWS
    pends = jnp.cumsum(pcounts)
    pstarts = pends - pcounts
    n_steps = (n_assign + N_EXPERTS * STEP_ROWS) // STEP_ROWS
    row0 = jnp.arange(n_steps, dtype=jnp.int32) * STEP_ROWS
    step_e = jnp.minimum(jnp.sum((pends[None, :] <= row0[:, None]).astype(jnp.int32), axis=1), N_EXPERTS - 1)
    offset = row0 - pstarts[step_e]
    valid = counts[step_e] - offset
    first = jnp.clip(starts[step_e] + offset, 0, n_assign - 1)
    r = jnp.arange(STEP_ROWS, dtype=jnp.int32)[None, :]
    idx = jnp.minimum(first[:, None] + r, n_assign - 1)
    rows = jnp.where(r < valid[:, None], order[idx] // TOP_K, 0).reshape(-1)
    rows = jnp.concatenate([rows, jnp.zeros((2 * STEP_ROWS,), jnp.int32)])
    shift = jnp.sum(jnp.where(onehot, (pstarts - starts)[None, :], 0), axis=1)
    pos = (rank + shift).reshape(n_tokens // tm, tm, TOP_K).transpose(0, 2, 1).reshape(-1)
    pos = jnp.concatenate([pos, jnp.zeros((2 * TOP_K * tm,), jnp.int32)])
    return step_e, rows, pos


def _rope_tables(seq):
    inv = ROPE_THETA ** (-jnp.arange(0, HEAD_DIM, 2, dtype=F32) / HEAD_DIM)
    ang = jnp.arange(seq, dtype=F32)[:, None] * inv[None, :]
    ang = jnp.concatenate([ang, ang], axis=-1)
    reps = DIFF_QK_WIDTH // HEAD_DIM
    return jnp.tile(jnp.cos(ang), (1, reps)), jnp.tile(jnp.sin(ang), (1, reps))


def _tile_plan(seq):
    return dict(
        proj=min(512, seq),
        combine=256,
        rwkv=min(512, seq),
        attn=min(256, seq),
    )


def _layer(x2, p2, batch, seq, lam_init, prm):
    t = x2.shape[0]
    tiles = _tile_plan(seq)
    tm = tiles['proj']
    cos, sin = _rope_tables(seq)
    reps = DIFF_QK_WIDTH // HEAD_DIM
    prw, q, k, v, gates = _inproj(
        x2, prm['norm_mix'][None], prm['w_in'].astype(BF16), cos, sin,
        jnp.tile(prm['q_norm'], reps)[None], jnp.tile(prm['k_norm'], reps)[None], seq, tm)

    zpad = jnp.zeros((DECAY_LORA, RWKV_WIDTH), F32)
    w2p = jnp.concatenate([prm['rwkv_w2'], zpad], axis=0).astype(BF16)
    a2p = jnp.concatenate([zpad, prm['rwkv_a2']], axis=0).astype(BF16)
    o_rwkv = _rwkv(
        prw.reshape(batch, seq, RWKV_COLS), prm['rwkv_mu'][None], prm['rwkv_w0'][None], w2p,
        prm['rwkv_a0'][None], a2p, prm['rwkv_g2'].astype(BF16), prm['rwkv_k_k'][None],
        prm['rwkv_k_a'][None], prm['rwkv_r_k'].reshape(1, RWKV_WIDTH), prm['rwkv_ln_w'][None],
        prm['rwkv_ln_b'][None], tiles['rwkv']).reshape(t, RWKV_WIDTH)

    o_diff = _diff_attention(
        q, k, v, prm['lambda_q1'][None], prm['lambda_k1'][None], prm['lambda_q2'][None],
        prm['lambda_k2'][None], prm['subln_w'][None], batch, seq, lam_init, tiles['attn'])

    w_r = jnp.concatenate([prm['w_group'], prm['w_expert_router']], axis=1)
    w_r = jnp.pad(w_r, ((0, 0), (0, LANES - ROUTE_COLS)))
    w_r_hi = w_r.astype(BF16)
    w_r_lo = (w_r - w_r_hi.astype(F32)).astype(BF16)
    b_r = jnp.pad(jnp.concatenate([prm['b_group'], prm['b_expert_router']]), (0, LANES - ROUTE_COLS))[None]
    x1, h2, route = _merge(
        x2, gates, o_rwkv, o_diff, prm['w_branch_rwkv'].astype(BF16), prm['w_branch_diff'].astype(BF16),
        prm['w_out'].astype(BF16), prm['norm_ffn'][None], jnp.concatenate([w_r_hi, w_r_lo], axis=1), b_r, tm)

    step_e, rows, pos = _dispatch_plan(route, t, tiles['combine'])
    yb = _experts(step_e, rows, h2, prm['w_gate'], prm['w_up'], prm['w_down'])
    return _final(pos, x1, route, p2, yb, prm['norm_ple'][None], prm['w_ple_gate'].astype(BF16),
                  prm['w_ple_proj'].astype(BF16), tiles['combine'])


_PARAM_NAMES = (
    'norm_mix', 'w_in', 'rwkv_mu', 'rwkv_w0', 'rwkv_w2', 'rwkv_a0', 'rwkv_a2', 'rwkv_g2', 'rwkv_k_k',
    'rwkv_k_a', 'rwkv_r_k', 'rwkv_ln_w', 'rwkv_ln_b', 'q_norm', 'k_norm', 'lambda_q1', 'lambda_k1',
    'lambda_q2', 'lambda_k2', 'subln_w', 'w_branch_rwkv', 'w_branch_diff', 'w_out', 'norm_ffn',
    'w_group', 'b_group', 'w_expert_router', 'b_expert_router', 'w_gate', 'w_up', 'w_down', 'norm_ple',
    'w_ple_gate', 'w_ple_proj')


def kernel(x, p, norm_mix, w_in, rwkv_mu, rwkv_w0, rwkv_w2, rwkv_a0, rwkv_a2, rwkv_g2, rwkv_k_k,
           rwkv_k_a, rwkv_r_k, rwkv_ln_w, rwkv_ln_b, q_norm, k_norm, lambda_q1, lambda_k1, lambda_q2,
           lambda_k2, subln_w, w_branch_rwkv, w_branch_diff, w_out, norm_ffn, w_group, b_group,
           w_expert_router, b_expert_router, w_gate, w_up, w_down, norm_ple, w_ple_gate, w_ple_proj):
    stacked = dict(zip(_PARAM_NAMES, (
        norm_mix, w_in, rwkv_mu, rwkv_w0, rwkv_w2, rwkv_a0, rwkv_a2, rwkv_g2, rwkv_k_k, rwkv_k_a,
        rwkv_r_k, rwkv_ln_w, rwkv_ln_b, q_norm, k_norm, lambda_q1, lambda_k1, lambda_q2, lambda_k2,
        subln_w, w_branch_rwkv, w_branch_diff, w_out, norm_ffn, w_group, b_group, w_expert_router,
        b_expert_router, w_gate, w_up, w_down, norm_ple, w_ple_gate, w_ple_proj)))
    batch, seq, _ = x.shape
    depth = p.shape[0]
    x2 = x.reshape(batch * seq, D_MODEL)
    for layer in range(depth):
        lam_init = 0.8 - 0.6 * math.exp(-0.3 * layer)
        prm = {name: val[layer] for name, val in stacked.items()}
        x2 = _layer(x2, p[layer].reshape(batch * seq, PLE_DIM), batch, seq, lam_init, prm)
    return x2.reshape(batch, seq, D_MODEL)
```

```python
import functools
import math

import jax
import jax.numpy as jnp
from jax import lax
from jax.experimental import pallas as pl
from jax.experimental.pallas import tpu as pltpu

F32 = jnp.float32
BF16 = jnp.bfloat16

D_MODEL = 1024
PLE_DIM = 256
RMS_EPS = 1e-6

RWKV_HEADS = 8
HEAD_DIM = 64
RWKV_WIDTH = RWKV_HEADS * HEAD_DIM
DECAY_LORA = 64
AAA_LORA = 64
GATE_LORA = 128
GN_EPS = 64e-5
RWKV_COLS = 3 * RWKV_WIDTH + DECAY_LORA + AAA_LORA + GATE_LORA

DIFF_HEADS = 4
DIFF_QK_WIDTH = DIFF_HEADS * 2 * HEAD_DIM
DIFF_V_DIM = 2 * HEAD_DIM
DIFF_V_WIDTH = DIFF_HEADS * DIFF_V_DIM
DIFF_COLS = 2 * DIFF_QK_WIDTH + DIFF_V_WIDTH
ROPE_THETA = 10000.0
GATE_COLS = 2 * D_MODEL

N_GROUPS = 4
EXPERTS_PER_GROUP = 8
N_EXPERTS = N_GROUPS * EXPERTS_PER_GROUP
TOP_K = 2
EXPERT_FF = 512
ROW_BLOCK = 128
STEP_ROWS = 2 * ROW_BLOCK

LANES = 128
CHUNK = 64
VMEM_LIMIT = 56 * 1024 * 1024


def _dot(a, b):
    return jnp.dot(a, b, preferred_element_type=F32)


def _dot_nt(a, b):
    return lax.dot_general(a, b, (((1,), (1,)), ((), ())), preferred_element_type=F32)


SUBLANES = 8
TILE_ROWS = D_MODEL // LANES
assert TILE_ROWS == SUBLANES


def _store_token_tiles(ref, base, val, pitch=TILE_ROWS):
    n = val.shape[0]
    for j in range(TILE_ROWS):
        ref[pl.ds(base + j, n, stride=pitch), :] = val[:, j * LANES:(j + 1) * LANES]


def _load_token_tiles(ref, base, n, pitch=TILE_ROWS):
    return jnp.concatenate([ref[pl.ds(base + j, n, stride=pitch), :] for j in range(TILE_ROWS)], axis=1)


def _tile(ref, index):
    return ref.at[pl.ds(pl.multiple_of(index * TILE_ROWS, TILE_ROWS), TILE_ROWS)]


MXU_DIM = 256
ATTN_KEY_STEP = 1024


def _group_ones(width, group):
    i = jnp.arange(width) // group
    return (i[:, None] == i[None, :]).astype(BF16)


def _head_sums(t, ones):
    t16 = t.astype(BF16)
    return jnp.concatenate([_dot(t16[:, i:i + MXU_DIM], ones) for i in range(0, t.shape[1], MXU_DIM)], axis=1)


def _rotate_half(t):
    width = t.shape[-1]
    lane = lax.broadcasted_iota(jnp.int32, t.shape, 1)
    fwd = pltpu.roll(t, width - HEAD_DIM // 2, 1)
    bwd = pltpu.roll(t, HEAD_DIM // 2, 1)
    return jnp.where(lane % HEAD_DIM < HEAD_DIM // 2, -fwd, bwd)


def _inproj_body(x_ref, g_ref, w_ref, cos_ref, sin_ref, qg_ref, kg_ref, ones_ref,
                 prw_ref, q_ref, k_ref, v_ref, gate_ref):
    x = x_ref[...]
    ms = jnp.mean(x * x, axis=-1, keepdims=True)
    h = (x * lax.rsqrt(ms + RMS_EPS) * g_ref[...]).astype(BF16)
    prw_ref[...] = _dot(h, w_ref[:, :RWKV_COLS])
    pd = _dot(h, w_ref[:, RWKV_COLS:RWKV_COLS + DIFF_COLS])
    cos = cos_ref[...]
    sin = sin_ref[...]

    def qk_prep(t, gain, scale):
        ssq = _head_sums(t * t, ones_ref[...])
        t = t * lax.rsqrt(ssq * (1.0 / HEAD_DIM) + RMS_EPS) * gain
        return ((t * cos + _rotate_half(t) * sin) * scale).astype(BF16)

    q_ref[...] = qk_prep(pd[:, :DIFF_QK_WIDTH], qg_ref[...], HEAD_DIM ** -0.5 * math.log2(math.e))
    k_ref[...] = qk_prep(pd[:, DIFF_QK_WIDTH:2 * DIFF_QK_WIDTH], kg_ref[...], 1.0)
    v_ref[...] = pd[:, 2 * DIFF_QK_WIDTH:].astype(BF16)
    gate_ref[...] = jax.nn.sigmoid(_dot(h, w_ref[:, RWKV_COLS + DIFF_COLS:])).astype(BF16)


def _inproj(x2, norm_g, w_in, cos, sin, q_gain, k_gain, seq, tm):
    t = x2.shape[0]
    n_seq_tiles = seq // tm
    row = lambda i: (i, 0)
    fixed = lambda i: (0, 0)
    rope = lambda i: (i % n_seq_tiles, 0)
    return pl.pallas_call(
        _inproj_body,
        grid=(t // tm,),
        in_specs=[
            pl.BlockSpec((tm, D_MODEL), row),
            pl.BlockSpec((1, D_MODEL), fixed),
            pl.BlockSpec((D_MODEL, RWKV_COLS + DIFF_COLS + GATE_COLS), fixed),
            pl.BlockSpec((tm, DIFF_QK_WIDTH), rope),
            pl.BlockSpec((tm, DIFF_QK_WIDTH), rope),
            pl.BlockSpec((1, DIFF_QK_WIDTH), fixed),
            pl.BlockSpec((1, DIFF_QK_WIDTH), fixed),
            pl.BlockSpec((MXU_DIM, MXU_DIM), fixed),
        ],
        out_specs=[
            pl.BlockSpec((tm, RWKV_COLS), row),
            pl.BlockSpec((tm, DIFF_QK_WIDTH), row),
            pl.BlockSpec((tm, DIFF_QK_WIDTH), row),
            pl.BlockSpec((tm, DIFF_V_WIDTH), row),
            pl.BlockSpec((tm, GATE_COLS), row),
        ],
        out_shape=[
            jax.ShapeDtypeStruct((t, RWKV_COLS), F32),
            jax.ShapeDtypeStruct((t, DIFF_QK_WIDTH), BF16),
            jax.ShapeDtypeStruct((t, DIFF_QK_WIDTH), BF16),
            jax.ShapeDtypeStruct((t, DIFF_V_WIDTH), BF16),
            jax.ShapeDtypeStruct((t, GATE_COLS), BF16),
        ],
        compiler_params=pltpu.CompilerParams(
            dimension_semantics=("parallel",), vmem_limit_bytes=VMEM_LIMIT),
        name="inproj",
    )(x2, norm_g, w_in, cos, sin, q_gain, k_gain, _group_ones(MXU_DIM, HEAD_DIM))


def _attn_body(q_ref, *refs, tq, wide, lam_init):
    def tile(qi, carry):
        _attn_tile(qi, q_ref, *refs, tq=tq, wide=wide, lam_init=lam_init)
        return carry

    lax.fori_loop(0, q_ref.shape[0] // tq, tile, 0)


def _attn_tile(qi, q_ref, k_ref, v_ref, lq1_ref, lk1_ref, lq2_ref, lk2_ref, sw_ref, o_ref,
               qs_sc, m_sc, l_sc, acc_sc, s0_sc, s1_sc, *, tq, wide, lam_init):
    q_rows = pl.ds(pl.multiple_of(qi * tq, tq), tq)
    heads = range(q_ref.shape[1] // LANES)
    cols = [slice(h * LANES, (h + 1) * LANES) for h in heads]
    for h in heads:
        q = q_ref[q_rows, cols[h]]
        lane = lax.broadcasted_iota(jnp.int32, q.shape, 1)
        qs_sc[h, :tq, :] = jnp.where(lane < HEAD_DIM, q, jnp.zeros_like(q))
        qs_sc[h, tq:, :] = jnp.where(lane >= HEAD_DIM, q, jnp.zeros_like(q))
    m_sc[...] = jnp.full(m_sc.shape, -jnp.inf, F32)
    l_sc[...] = jnp.zeros(l_sc.shape, F32)
    acc_sc[...] = jnp.zeros(acc_sc.shape, F32)

    width = wide * tq

    def scores(j, s_sc):
        for h in heads:
            kb = k_ref[pl.ds(pl.multiple_of(j * width, width), width), cols[h]]
            s_sc[h] = _dot_nt(kb, qs_sc[h])

    def update(j, s_sc, masked):
        for h in heads:
            s = s_sc[h]
            if masked:
                key = lax.broadcasted_iota(jnp.int32, s.shape, 0) + j * width
                query = lax.broadcasted_iota(jnp.int32, s.shape, 1) % tq + qi * tq
                s = jnp.where(key <= query, s, -jnp.inf)
            vb = v_ref[pl.ds(pl.multiple_of(j * width, width), width), cols[h]]
            m_prev = m_sc[h]
            m_new = jnp.maximum(m_prev, jnp.max(s, axis=0, keepdims=True))
            alpha = jnp.exp2(m_prev - m_new)
            p = jnp.exp2(s - m_new)
            l_sc[h] = alpha * l_sc[h] + jnp.sum(p, axis=0, keepdims=True)
            acc_sc[h] = alpha * acc_sc[h] + _dot_tn(vb, p.astype(BF16))
            m_sc[h] = m_new

    last = qi // wide
    pairs = last // 2
    scores(0, s0_sc)

    def body(jj, carry):
        scores(2 * jj + 1, s1_sc)
        update(2 * jj, s0_sc, False)
        scores(2 * jj + 2, s0_sc)
        update(2 * jj + 1, s1_sc, False)
        return carry

    lax.fori_loop(0, pairs, body, 0)

    @pl.when(last % 2 == 1)
    def _():
        scores(last, s1_sc)
        update(last - 1, s0_sc, False)
        update(last, s1_sc, True)

    @pl.when(last % 2 == 0)
    def _():
        update(last, s0_sc, True)

    lam = (jnp.exp(jnp.sum(lq1_ref[...] * lk1_ref[...], axis=-1, keepdims=True))
           - jnp.exp(jnp.sum(lq2_ref[...] * lk2_ref[...], axis=-1, keepdims=True)) + lam_init)
    for h in heads:
        o = acc_sc[h] / l_sc[h]
        o = o[:, :tq] - lam * o[:, tq:]
        ms = jnp.mean(o * o, axis=0, keepdims=True)
        o = (o * lax.rsqrt(ms + RMS_EPS)).T
        o_ref[q_rows, cols[h]] = o * sw_ref[...] * (1.0 - lam_init)


def _diff_attention(q, k, v, lq1, lk1, lq2, lk2, subln_w, batch, seq, lam_init, tq):
    t = q.shape[0]
    nq = seq // tq
    vec = pl.BlockSpec((1, HEAD_DIM), lambda b, h: (0, 0))
    wide = max(1, min(ATTN_KEY_STEP // tq, nq // 2))
    assert nq % wide == 0
    hpg = 2
    seq_block = pl.BlockSpec((seq, hpg * LANES), lambda b, h: (b, h))
    return pl.pallas_call(
        functools.partial(_attn_body, tq=tq, wide=wide, lam_init=lam_init),
        grid=(batch, DIFF_HEADS // hpg),
        in_specs=[
            seq_block, seq_block, seq_block,
            vec, vec, vec, vec,
            pl.BlockSpec((1, DIFF_V_DIM), lambda b, h: (0, 0)),
        ],
        out_specs=seq_block,
        out_shape=jax.ShapeDtypeStruct((t, DIFF_V_WIDTH), F32),
        scratch_shapes=[
            pltpu.VMEM((hpg, 2 * tq, LANES), BF16),
            pltpu.VMEM((hpg, 1, 2 * tq), F32),
            pltpu.VMEM((hpg, 1, 2 * tq), F32),
            pltpu.VMEM((hpg, DIFF_V_DIM, 2 * tq), F32),
            pltpu.VMEM((hpg, wide * tq, 2 * tq), F32),
            pltpu.VMEM((hpg, wide * tq, 2 * tq), F32),
        ],
        compiler_params=pltpu.CompilerParams(
            dimension_semantics=("parallel", "parallel"),
            vmem_limit_bytes=VMEM_LIMIT),
        name="diff_attn",
    )(q, k, v, lq1, lk1, lq2, lk2, subln_w)


def _dot_tn(a, b):
    return lax.dot_general(a, b, (((0,), (0,)), ((), ())), preferred_element_type=F32)


def _wkv_prep(problems):
    c = CHUNK
    lane = lax.broadcasted_iota(jnp.int32, (c, LANES), 1)
    m0 = lane < HEAD_DIM
    row = lax.broadcasted_iota(jnp.int32, (LANES, LANES), 0)
    col = lax.broadcasted_iota(jnp.int32, (LANES, LANES), 1)
    same = (row // c) == (col // c)
    strict = same & ((col % c) < (row % c))
    incl = same & ((col % c) <= (row % c))
    eye = row == col

    def stack(t):
        zero = jnp.zeros_like(t)
        return jnp.concatenate([jnp.where(m0, t, zero), jnp.where(m0, zero, t)], axis=0)

    def dup(t):
        return jnp.concatenate([t, t], axis=0)

    def fold(t):
        return t[:c] + t[c:]

    n = len(problems)
    xs = [_dot_nt(jnp.concatenate([q['kkt'], q['rt']], axis=0),
                  jnp.concatenate([stack(q['bt']), stack(q['kt'])], axis=0)) for q in problems]
    l_pow = [jnp.where(strict, dup(x[:c, :LANES]), 0.0) for x in xs]
    m_ak = [jnp.where(strict, dup(x[:c, LANES:]), 0.0).astype(BF16) for x in xs]
    a_rbk = [jnp.concatenate([jnp.where(incl, dup(x[c:, :LANES]), 0.0),
                              jnp.where(incl, dup(x[c:, LANES:]), 0.0)], axis=1).astype(BF16) for x in xs]
    t_inv = [jnp.where(eye, 1.0, 0.0) - l for l in l_pow]
    for _ in range(5):
        l16 = [l.astype(BF16) for l in l_pow]
        l_pow = [_dot(l, l) for l in l16]
        upd = [_dot(t.astype(BF16), l.astype(BF16)) for t, l in zip(t_inv, l_pow)]
        t_inv = [t + u for t, u in zip(t_inv, upd)]

    vs = [stack(q['v']) for q in problems]
    mv = [_dot(m, v) for m, v in zip(m_ak, vs)]
    z = [_dot(t.astype(BF16), jnp.concatenate([stack(q['kkt']), m.astype(BF16)], axis=1))
         for t, q, m in zip(t_inv, problems, mv)]
    zero_sq = jnp.zeros((LANES, LANES), BF16)
    qy = [_dot(a, jnp.concatenate([(-zz).astype(BF16), jnp.concatenate([zero_sq, v], axis=1)], axis=0))
          for a, zz, v in zip(a_rbk, z, vs)]
    zero_c = jnp.zeros((c, LANES), BF16)
    pd = [_dot_tn(jnp.concatenate([q['bh'], q['kh']], axis=0),
                  jnp.concatenate([jnp.concatenate([fold(-zz[:, :LANES]), fold(-zz[:, LANES:])], axis=1).astype(BF16),
                                   jnp.concatenate([zero_c, q['v']], axis=1)], axis=0))
          for q, zz in zip(problems, z)]
    out = []
    for i in range(n):
        q = problems[i]
        qs = stack(q['rt']).astype(F32) + qy[i][:, :LANES]
        phi = (jnp.where(eye, jnp.broadcast_to(q['pc'], (LANES, LANES)), 0.0)
               + jnp.where(same, pd[i][:, :LANES], 0.0))
        delta = jnp.where(same, pd[i][:, LANES:], 0.0)
        qphi = jnp.concatenate([fold(qs), phi], axis=0).astype(BF16)
        out.append((qphi, fold(qy[i][:, LANES:]), delta))
    return out


def _rwkv_body(pr_ref, halo_ref, mu_ref, w0_ref, w2_ref, a0_ref, a2_ref, g2_ref, kk_ref, ka_ref,
               rk_ref, lnw_ref, lnb_ref, ones_ref, tri_ref, o_ref,
               hp_sc, rt_sc, kt_sc, bt_sc, kkt_sc, kh_sc, bh_sc, v_sc, pc_sc, qphi_sc, delta_sc, y_sc,
               *, rows):
    i = pl.program_id(1)
    n_pairs = RWKV_WIDTH // LANES
    n_chunks = rows // CHUNK

    @pl.when(i == 0)
    def _():
        hp_sc[...] = jnp.zeros(hp_sc.shape, F32)

    pr = pr_ref[0]
    first = jnp.where(i == 0, 0.0, halo_ref[0, 7:8, :])
    prev = jnp.concatenate([first, pr[:-1]], axis=0)
    xs = pr + (prev - pr) * mu_ref[...]
    w = RWKV_WIDTH
    r = xs[:, :w]
    k = xs[:, w:2 * w]
    v = xs[:, 2 * w:3 * w]
    x_lora = xs[:, 3 * w:3 * w + DECAY_LORA + AAA_LORA]
    xg = xs[:, 3 * w + DECAY_LORA + AAA_LORA:]
    wl = w0_ref[...] + _dot(jnp.tanh(x_lora).astype(BF16), w2_ref[...])
    sp = jnp.maximum(-wl, 0.0) + jnp.log(1.0 + jnp.exp(-jnp.abs(wl)))
    lw = -jnp.exp(-sp - 0.5)
    a = jax.nn.sigmoid(a0_ref[...] + _dot(x_lora.astype(BF16), a2_ref[...]))
    g = _dot(jax.nn.sigmoid(xg).astype(BF16), g2_ref[...])
    ones = ones_ref[...]
    kx = k * kk_ref[...]
    ssq = _head_sums(kx * kx, ones)
    kn = kx / jnp.maximum(jnp.sqrt(ssq), 1e-12)
    km = k * (1.0 + (a - 1.0) * ka_ref[...])
    bonus = _head_sums(r * km * rk_ref[...], ones) * v

    hi = lw.astype(BF16)
    r1 = lw - hi.astype(F32)
    mid = r1.astype(BF16)
    lo = (r1 - mid.astype(F32)).astype(BF16)
    tri = tri_ref[...]
    parts = jnp.concatenate([hi, mid, lo], axis=1)
    cum = jnp.concatenate([_dot(tri, parts[i:i + MXU_DIM]) for i in range(0, rows, MXU_DIM)], axis=0)
    cum = cum[:, :w] + cum[:, w:2 * w] + cum[:, 2 * w:]

    cum3 = cum.reshape(n_chunks, CHUNK, w)
    end3 = cum3[:, CHUNK - 1:CHUNK, :]
    p_inv = jnp.exp(-cum)
    p_end = jnp.exp(end3 - cum3).reshape(rows, w)
    b = kn * a
    rt_sc[...] = (r * jnp.exp(cum)).astype(BF16)
    kt_sc[...] = (km * p_inv).astype(BF16)
    bt_sc[...] = (b * p_inv).astype(BF16)
    kkt_sc[...] = (kn * jnp.exp(cum - lw)).astype(BF16)
    kh_sc[...] = (km * p_end).astype(BF16)
    bh_sc[...] = (b * p_end).astype(BF16)
    v_sc[...] = v.astype(BF16)
    pc_sc[...] = jnp.broadcast_to(jnp.exp(end3), pc_sc.shape)

    group = 4 if n_chunks % 4 == 0 else 1

    def prep_chunks(gi, carry):
        where = []
        for g in range(group):
            ci = gi * group + g
            rs = pl.ds(pl.multiple_of(ci * CHUNK, CHUNK), CHUNK)
            for p in range(n_pairs):
                where.append((ci, rs, p, slice(p * LANES, (p + 1) * LANES)))
        problems = [dict(rt=rt_sc[rs, sl], kt=kt_sc[rs, sl], bt=bt_sc[rs, sl], kkt=kkt_sc[rs, sl],
                         kh=kh_sc[rs, sl], bh=bh_sc[rs, sl], v=v_sc[rs, sl], pc=pc_sc[ci, 0:1, sl])
                    for ci, rs, p, sl in where]
        for (ci, rs, p, sl), (qphi, y0, delta) in zip(where, _wkv_prep(problems)):
            qphi_sc[ci, p] = qphi
            delta_sc[ci, p] = delta
            y_sc[rs, sl] = y0
        return carry

    lax.fori_loop(0, n_chunks // group, prep_chunks, 0)

    def scan_chunk(ci, carry):
        rs = pl.ds(pl.multiple_of(ci * CHUNK, CHUNK), CHUNK)
        for p in range(n_pairs):
            sl = slice(p * LANES, (p + 1) * LANES)
            res = _dot(qphi_sc[ci, p], hp_sc[p].astype(BF16))
            y_sc[rs, sl] += res[:CHUNK]
            hp_sc[p] = res[CHUNK:] + delta_sc[ci, p]
        return carry

    lax.fori_loop(0, n_chunks, scan_chunk, 0, unroll=True)

    y = y_sc[...]
    mean = _head_sums(y, ones) * (1.0 / HEAD_DIM)
    yc = y - mean
    var = _head_sums(yc * yc, ones) * (1.0 / HEAD_DIM)
    yn = yc * lax.rsqrt(var + GN_EPS) * lnw_ref[...] + lnb_ref[...]
    o_ref[0] = ((yn + bonus) * g).astype(o_ref.dtype)


def _rwkv(pr3, mu, w0, w2p, a0, a2p, g2, k_k, k_a, r_k, ln_w, ln_b, rows):
    batch, seq, _ = pr3.shape
    nb = seq // rows
    fixed = lambda b, i: (0, 0)
    vec = pl.BlockSpec((1, RWKV_WIDTH), fixed)
    lora = DECAY_LORA + AAA_LORA
    n_pairs = RWKV_WIDTH // LANES
    assert rows % MXU_DIM == 0 and MXU_DIM % CHUNK == 0
    ci = jnp.arange(MXU_DIM) // CHUNK
    ti = jnp.arange(MXU_DIM)
    tri = ((ci[:, None] == ci[None, :]) & (ti[None, :] <= ti[:, None])).astype(BF16)
    big = pltpu.VMEM((rows, RWKV_WIDTH), F32)
    n_chunks = rows // CHUNK
    return pl.pallas_call(
        functools.partial(_rwkv_body, rows=rows),
        grid=(batch, nb),
        in_specs=[
            pl.BlockSpec((1, rows, RWKV_COLS), lambda b, i: (b, i, 0)),
            pl.BlockSpec((1, 8, RWKV_COLS), lambda b, i: (b, jnp.maximum(i * (rows // 8) - 1, 0), 0)),
            pl.BlockSpec((1, RWKV_COLS), fixed),
            vec,
            pl.BlockSpec((lora, RWKV_WIDTH), fixed),
            vec,
            pl.BlockSpec((lora, RWKV_WIDTH), fixed),
            pl.BlockSpec((GATE_LORA, RWKV_WIDTH), fixed),
            vec, vec, vec, vec, vec,
            pl.BlockSpec((MXU_DIM, MXU_DIM), fixed),
            pl.BlockSpec((MXU_DIM, MXU_DIM), fixed),
        ],
        out_specs=pl.BlockSpec((1, rows, RWKV_WIDTH), lambda b, i: (b, i, 0)),
        out_shape=jax.ShapeDtypeStruct((batch, seq, RWKV_WIDTH), F32),
        scratch_shapes=(
            [pltpu.VMEM((n_pairs, LANES, LANES), F32)]
            + [pltpu.VMEM((rows, RWKV_WIDTH), BF16)] * 7
            + [pltpu.VMEM((n_chunks, 8, RWKV_WIDTH), F32),
               pltpu.VMEM((n_chunks, n_pairs, CHUNK + LANES, LANES), BF16),
               pltpu.VMEM((n_chunks, n_pairs, LANES, LANES), F32),
               big]),
        compiler_params=pltpu.CompilerParams(
            dimension_semantics=("parallel", "arbitrary"), vmem_limit_bytes=VMEM_LIMIT),
        name="rwkv7",
    )(pr3, pr3, mu, w0, w2p, a0, a2p, g2, k_k, k_a, r_k, ln_w, ln_b,
      _group_ones(MXU_DIM, HEAD_DIM), tri)


ROUTE_COLS = N_GROUPS + N_EXPERTS


def _merge_body(x_ref, gate_ref, orw_ref, odf_ref, wbr_ref, wbd_ref, wout_ref, nf_ref, wr_ref,
                br_ref, x1_ref, h2_ref, route_ref):
    gates = gate_ref[...].astype(F32)
    mixed = (gates[:, :D_MODEL] * _dot(orw_ref[...].astype(BF16), wbr_ref[...])
             + gates[:, D_MODEL:] * _dot(odf_ref[...].astype(BF16), wbd_ref[...]))
    x1 = x_ref[...] + _dot(mixed.astype(BF16), wout_ref[...])
    x1_ref[...] = x1
    ms = jnp.mean(x1 * x1, axis=-1, keepdims=True)
    h2 = x1 * lax.rsqrt(ms + RMS_EPS) * nf_ref[...]
    _store_token_tiles(h2_ref, 0, h2)

    hi = h2.astype(BF16)
    lo = (h2 - hi.astype(F32)).astype(BF16)
    acc = _dot(hi, wr_ref[...])
    logits = acc[:, :LANES] + acc[:, LANES:] + _dot(lo, wr_ref[:, :LANES]) + br_ref[...]
    lane = lax.broadcasted_iota(jnp.int32, logits.shape, 1).astype(F32)
    neg = -jnp.inf

    def top(vals):
        m = jnp.max(vals, axis=-1, keepdims=True)
        return m, jnp.min(jnp.where(vals == m, lane, float(LANES)), axis=-1, keepdims=True)

    gl = jnp.where(lane < N_GROUPS, logits, neg)
    gm, g_idx = top(gl)
    g_top = 1.0 / jnp.sum(jnp.exp(gl - gm), axis=-1, keepdims=True)
    first = N_GROUPS + EXPERTS_PER_GROUP * g_idx
    el = jnp.where((lane >= first) & (lane < first + EXPERTS_PER_GROUP), logits, neg)
    t1, i1 = top(el)
    t2, i2 = top(jnp.where(lane == i1, neg, el))
    e2 = jnp.exp(t2 - t1)
    w1 = g_top / (1.0 + e2)
    w2 = g_top * e2 / (1.0 + e2)
    route_ref[...] = jnp.where(lane == 0, i1 - N_GROUPS, jnp.where(lane == 1, i2 - N_GROUPS,
                               jnp.where(lane == 2, w1, jnp.where(lane == 3, w2, 0.0))))


def _merge(x2, gates, o_rwkv, o_diff, w_br, w_bd, w_out, norm_ffn, w_route, b_route, tm):
    t = x2.shape[0]
    row = lambda i: (i, 0)
    fixed = lambda i: (0, 0)
    return pl.pallas_call(
        _merge_body,
        grid=(t // tm,),
        in_specs=[
            pl.BlockSpec((tm, D_MODEL), row),
            pl.BlockSpec((tm, GATE_COLS), row),
            pl.BlockSpec((tm, RWKV_WIDTH), row),
            pl.BlockSpec((tm, DIFF_V_WIDTH), row),
            pl.BlockSpec((RWKV_WIDTH, D_MODEL), fixed),
            pl.BlockSpec((DIFF_V_WIDTH, D_MODEL), fixed),
            pl.BlockSpec((D_MODEL, D_MODEL), fixed),
            pl.BlockSpec((1, D_MODEL), fixed),
            pl.BlockSpec((D_MODEL, 2 * LANES), fixed),
            pl.BlockSpec((1, LANES), fixed),
        ],
        out_specs=[
            pl.BlockSpec((tm, D_MODEL), row),
            pl.BlockSpec((tm * TILE_ROWS, LANES), row),
            pl.BlockSpec((tm, LANES), row),
        ],
        out_shape=[
            jax.ShapeDtypeStruct((t, D_MODEL), F32),
            jax.ShapeDtypeStruct((t * TILE_ROWS, LANES), F32),
            jax.ShapeDtypeStruct((t, LANES), F32),
        ],
        compiler_params=pltpu.CompilerParams(
            dimension_semantics=("parallel",), vmem_limit_bytes=VMEM_LIMIT),
        name="merge_router",
    )(x2, gates, o_rwkv, o_diff, w_br, w_bd, w_out, norm_ffn, w_route, b_route)


def _expert_body(e_ref, first_ref, next_ref, wslot_ref, rows_ref, h2_hbm, wg_hbm, wu_hbm, wd_hbm, yb_hbm,
                 xbuf, obuf, wg_f32, wu_f32, wd_f32, wg_sc, wu_sc, wd_sc, gsem, wsem, osem, *, n_steps):
    step_tiles = STEP_ROWS * TILE_ROWS

    def weight_copies(e, slot):
        return [pltpu.make_async_copy(w_hbm.at[e], buf.at[slot], wsem.at[slot])
                for w_hbm, buf in ((wg_hbm, wg_f32), (wu_hbm, wu_f32), (wd_hbm, wd_f32))]

    def start_gather(block, buf):
        for r in range(ROW_BLOCK):
            tok = rows_ref[block * ROW_BLOCK + r]
            pltpu.make_async_copy(_tile(h2_hbm, tok), _tile(xbuf, buf * ROW_BLOCK + r),
                                  gsem.at[buf]).start(r % 2)

    def wait_gather(buf):
        for _ in range(ROW_BLOCK):
            pltpu.make_async_copy(_tile(h2_hbm, 0), _tile(xbuf, 0), gsem.at[buf]).wait()

    def writeback(g, parity):
        return pltpu.make_async_copy(
            obuf.at[pl.ds(parity * step_tiles, step_tiles)],
            yb_hbm.at[pl.ds(pl.multiple_of(g * step_tiles, step_tiles), step_tiles)], osem.at[parity])

    for cp in weight_copies(e_ref[0], wslot_ref[0]):
        cp.start()
    for buf in range(4):
        start_gather(buf, buf)

    def step(g, parity):
        @pl.when(first_ref[g] == 1)
        def _():
            slot = wslot_ref[g]
            for cp in weight_copies(e_ref[g], slot):
                cp.wait()
            nxt = next_ref[g]

            @pl.when(nxt >= 0)
            def _():
                for cp in weight_copies(nxt, 1 - slot):
                    cp.start()

            wg_sc[...] = wg_f32[slot].astype(BF16)
            wu_sc[...] = wu_f32[slot].astype(BF16)
            wd_sc[...] = wd_f32[slot].astype(BF16)

        @pl.when(g >= 2)
        def _():
            writeback(g - 2, parity).wait()

        xs = []
        for s in range(2):
            buf = 2 * parity + s
            wait_gather(buf)
            xs.append(_load_token_tiles(xbuf, buf * ROW_BLOCK * TILE_ROWS, ROW_BLOCK).astype(BF16))
            start_gather(2 * g + s + 4, buf)
        x = jnp.concatenate(xs, axis=0)
        gt = _dot(x, wg_sc[...])
        up = _dot(x, wu_sc[...])
        mid = (gt * jax.nn.sigmoid(gt) * up).astype(BF16)
        _store_token_tiles(obuf, parity * step_tiles, _dot(mid, wd_sc[...]))
        writeback(g, parity).start()

    def step_pair(i, carry):
        step(2 * i, 0)
        step(2 * i + 1, 1)
        return carry

    lax.fori_loop(0, n_steps // 2, step_pair, 0)
    for buf in range(4):
        wait_gather(buf)
    for parity in range(2):
        writeback(n_steps - 2 + parity, parity).wait()


def _experts(step_e, rows, h2, w_gate, w_up, w_down):
    n_steps = step_e.shape[0]
    idx = jnp.arange(n_steps, dtype=jnp.int32)
    first = jnp.concatenate([jnp.ones((1,), jnp.int32), (step_e[1:] != step_e[:-1]).astype(jnp.int32)])
    wslot = (jnp.cumsum(first) - 1) % 2
    change_at = jnp.where(first == 1, idx, n_steps)
    next_change = jnp.concatenate([lax.cummin(change_at, reverse=True)[1:], jnp.full((1,), n_steps, jnp.int32)])
    at_next = idx[None, :] == next_change[:, None]
    nxt = jnp.where(next_change < n_steps, jnp.sum(jnp.where(at_next, step_e[None, :], 0), axis=1), -1)
    assert n_steps % 2 == 0
    any_hbm = pl.BlockSpec(memory_space=pl.ANY)
    grid_spec = pltpu.PrefetchScalarGridSpec(
        num_scalar_prefetch=5,
        grid=(1,),
        in_specs=[any_hbm, any_hbm, any_hbm, any_hbm],
        out_specs=any_hbm,
        scratch_shapes=[pltpu.VMEM((2 * STEP_ROWS * TILE_ROWS, LANES), F32),
                        pltpu.VMEM((2 * STEP_ROWS * TILE_ROWS, LANES), F32),
                        pltpu.VMEM((2, D_MODEL, EXPERT_FF), F32), pltpu.VMEM((2, D_MODEL, EXPERT_FF), F32),
                        pltpu.VMEM((2, EXPERT_FF, D_MODEL), F32),
                        pltpu.VMEM((D_MODEL, EXPERT_FF), BF16), pltpu.VMEM((D_MODEL, EXPERT_FF), BF16),
                        pltpu.VMEM((EXPERT_FF, D_MODEL), BF16),
                        pltpu.SemaphoreType.DMA((4,)), pltpu.SemaphoreType.DMA((2,)),
                        pltpu.SemaphoreType.DMA((2,))],
    )
    return pl.pallas_call(
        functools.partial(_expert_body, n_steps=n_steps),
        grid_spec=grid_spec,
        out_shape=jax.ShapeDtypeStruct((n_steps * STEP_ROWS * TILE_ROWS, LANES), F32),
        compiler_params=pltpu.CompilerParams(
            dimension_semantics=("arbitrary",), vmem_limit_bytes=VMEM_LIMIT),
        name="experts",
    )(step_e, first, nxt.astype(jnp.int32), wslot.astype(jnp.int32), rows, h2, w_gate, w_up, w_down)


def _final_body(pos_ref, x1_ref, route_ref, p_ref, yb_hbm, np_ref, wpg_ref, wpp_ref, o_ref,
                ybuf, sem, *, tm, n_steps):
    i = pl.program_id(0)
    per_tile = TOP_K * tm

    def start_gather(tile_idx, s):
        for r in range(per_tile):
            row = pos_ref[tile_idx * per_tile + r]
            pltpu.make_async_copy(_tile(yb_hbm, row), _tile(ybuf, s * per_tile + r), sem.at[s]).start(r % 2)

    def wait_gather(s):
        for _ in range(per_tile):
            pltpu.make_async_copy(_tile(yb_hbm, 0), _tile(ybuf, 0), sem.at[s]).wait()

    @pl.when(i == 0)
    def _():
        start_gather(0, 0)
        start_gather(1, 1)

    for s in range(2):
        rows = pl.ds(s * tm, tm)
        wait_gather(s)
        y_first = _load_token_tiles(ybuf, s * per_tile * TILE_ROWS, tm)
        y_second = _load_token_tiles(ybuf, (s * per_tile + tm) * TILE_ROWS, tm)
        start_gather(2 * i + s + 2, s)
        route = route_ref[rows, :]
        x2 = x1_ref[rows, :] + route[:, 2:3] * y_first + route[:, 3:4] * y_second
        ms = jnp.mean(x2 * x2, axis=-1, keepdims=True)
        hn = (x2 * lax.rsqrt(ms + RMS_EPS) * np_ref[...]).astype(BF16)
        gate = jax.nn.sigmoid(_dot(hn, wpg_ref[...]))
        o_ref[rows, :] = x2 + gate * _dot(p_ref[rows, :].astype(BF16), wpp_ref[...])

    @pl.when(i == n_steps - 1)
    def _():
        for s in range(2):
            wait_gather(s)


def _final(pos, x1, route, p2, yb, norm_ple, w_pg, w_pp, tm):
    t = x1.shape[0]
    assert t % (2 * tm) == 0
    n_steps = t // (2 * tm)
    row = lambda i, pos: (i, 0)
    fixed = lambda i, pos: (0, 0)
    grid_spec = pltpu.PrefetchScalarGridSpec(
        num_scalar_prefetch=1,
        grid=(n_steps,),
        in_specs=[
            pl.BlockSpec((2 * tm, D_MODEL), row),
            pl.BlockSpec((2 * tm, LANES), row),
            pl.BlockSpec((2 * tm, PLE_DIM), row),
            pl.BlockSpec(memory_space=pl.ANY),
            pl.BlockSpec((1, D_MODEL), fixed),
            pl.BlockSpec((D_MODEL, D_MODEL), fixed),
            pl.BlockSpec((PLE_DIM, D_MODEL), fixed),
        ],
        out_specs=pl.BlockSpec((2 * tm, D_MODEL), row),
        scratch_shapes=[pltpu.VMEM((2 * TOP_K * tm * TILE_ROWS, LANES), F32), pltpu.SemaphoreType.DMA((2,))],
    )
    return pl.pallas_call(
        functools.partial(_final_body, tm=tm, n_steps=n_steps),
        grid_spec=grid_spec,
        out_shape=jax.ShapeDtypeStruct((t, D_MODEL), F32),
        compiler_params=pltpu.CompilerParams(
            dimension_semantics=("arbitrary",), vmem_limit_bytes=VMEM_LIMIT),
        name="combine_ple",
    )(pos, x1, route, p2, yb, norm_ple, w_pg, w_pp)


def _dispatch_plan(route, n_tokens, tm):
    eid = route[:, :TOP_K].astype(jnp.int32).reshape(-1)
    n_assign = n_tokens * TOP_K
    order = jnp.argsort(eid).astype(jnp.int32)
    rank = jnp.argsort(order).astype(jnp.int32)
    experts = jnp.arange(N_EXPERTS, dtype=jnp.int32)
    onehot = eid[:, None] == experts[None, :]
    counts = jnp.sum(onehot.astype(jnp.int32), axis=0)
    starts = jnp.cumsum(counts) - counts
    pcounts = ((counts + STEP_ROWS - 1) // STEP_ROWS) * STEP_ROWS
    pends = jnp.cumsum(pcounts)
    pstarts = pends - pcounts
    n_steps = (n_assign + N_EXPERTS * STEP_ROWS) // STEP_ROWS
    row0 = jnp.arange(n_steps, dtype=jnp.int32) * STEP_ROWS
    step_e = jnp.minimum(jnp.sum((pends[None, :] <= row0[:, None]).astype(jnp.int32), axis=1), N_EXPERTS - 1)
    offset = row0 - pstarts[step_e]
    valid = counts[step_e] - offset
    first = jnp.clip(starts[step_e] + offset, 0, n_assign - 1)
    r = jnp.arange(STEP_ROWS, dtype=jnp.int32)[None, :]
    idx = jnp.minimum(first[:, None] + r, n_assign - 1)
    rows = jnp.where(r < valid[:, None], order[idx] // TOP_K, 0).reshape(-1)
    rows = jnp.concatenate([rows, jnp.zeros((2 * STEP_ROWS,), jnp.int32)])
    shift = jnp.sum(jnp.where(onehot, (pstarts - starts)[None, :], 0), axis=1)
    pos = (rank + shift).reshape(n_tokens // tm, tm, TOP_K).transpose(0, 2, 1).reshape(-1)
    pos = jnp.concatenate([pos, jnp.zeros((2 * TOP_K * tm,), jnp.int32)])
    return step_e, rows, pos


def _rope_tables(seq):
    inv = ROPE_THETA ** (-jnp.arange(0, HEAD_DIM, 2, dtype=F32) / HEAD_DIM)
    ang = jnp.arange(seq, dtype=F32)[:, None] * inv[None, :]
    ang = jnp.concatenate([ang, ang], axis=-1)
    reps = DIFF_QK_WIDTH // HEAD_DIM
    return jnp.tile(jnp.cos(ang), (1, reps)), jnp.tile(jnp.sin(ang), (1, reps))


def _tile_plan(seq):
    return dict(
        proj=min(512, seq),
        combine=256,
        rwkv=min(512, seq),
        attn=min(512, seq),
    )


def _layer(x2, p2, batch, seq, lam_init, prm):
    t = x2.shape[0]
    tiles = _tile_plan(seq)
    tm = tiles['proj']
    cos, sin = _rope_tables(seq)
    reps = DIFF_QK_WIDTH // HEAD_DIM
    prw, q, k, v, gates = _inproj(
        x2, prm['norm_mix'][None], prm['w_in'].astype(BF16), cos, sin,
        jnp.tile(prm['q_norm'], reps)[None], jnp.tile(prm['k_norm'], reps)[None], seq, tm)

    zpad = jnp.zeros((DECAY_LORA, RWKV_WIDTH), F32)
    w2p = jnp.concatenate([prm['rwkv_w2'], zpad], axis=0).astype(BF16)
    a2p = jnp.concatenate([zpad, prm['rwkv_a2']], axis=0).astype(BF16)
    o_rwkv = _rwkv(
        prw.reshape(batch, seq, RWKV_COLS), prm['rwkv_mu'][None], prm['rwkv_w0'][None], w2p,
        prm['rwkv_a0'][None], a2p, prm['rwkv_g2'].astype(BF16), prm['rwkv_k_k'][None],
        prm['rwkv_k_a'][None], prm['rwkv_r_k'].reshape(1, RWKV_WIDTH), prm['rwkv_ln_w'][None],
        prm['rwkv_ln_b'][None], tiles['rwkv']).reshape(t, RWKV_WIDTH)

    o_diff = _diff_attention(
        q, k, v, prm['lambda_q1'][None], prm['lambda_k1'][None], prm['lambda_q2'][None],
        prm['lambda_k2'][None], prm['subln_w'][None], batch, seq, lam_init, tiles['attn'])

    w_r = jnp.concatenate([prm['w_group'], prm['w_expert_router']], axis=1)
    w_r = jnp.pad(w_r, ((0, 0), (0, LANES - ROUTE_COLS)))
    w_r_hi = w_r.astype(BF16)
    w_r_lo = (w_r - w_r_hi.astype(F32)).astype(BF16)
    b_r = jnp.pad(jnp.concatenate([prm['b_group'], prm['b_expert_router']]), (0, LANES - ROUTE_COLS))[None]
    x1, h2, route = _merge(
        x2, gates, o_rwkv, o_diff, prm['w_branch_rwkv'].astype(BF16), prm['w_branch_diff'].astype(BF16),
        prm['w_out'].astype(BF16), prm['norm_ffn'][None], jnp.concatenate([w_r_hi, w_r_lo], axis=1), b_r, tm)

    step_e, rows, pos = _dispatch_plan(route, t, tiles['combine'])
    yb = _experts(step_e, rows, h2, prm['w_gate'], prm['w_up'], prm['w_down'])
    return _final(pos, x1, route, p2, yb, prm['norm_ple'][None], prm['w_ple_gate'].astype(BF16),
                  prm['w_ple_proj'].astype(BF16), tiles['combine'])


_PARAM_NAMES = (
    'norm_mix', 'w_in', 'rwkv_mu', 'rwkv_w0', 'rwkv_w2', 'rwkv_a0', 'rwkv_a2', 'rwkv_g2', 'rwkv_k_k',
    'rwkv_k_a', 'rwkv_r_k', 'rwkv_ln_w', 'rwkv_ln_b', 'q_norm', 'k_norm', 'lambda_q1', 'lambda_k1',
    'lambda_q2', 'lambda_k2', 'subln_w', 'w_branch_rwkv', 'w_branch_diff', 'w_out', 'norm_ffn',
    'w_group', 'b_group', 'w_expert_router', 'b_expert_router', 'w_gate', 'w_up', 'w_down', 'norm_ple',
    'w_ple_gate', 'w_ple_proj')


def kernel(x, p, norm_mix, w_in, rwkv_mu, rwkv_w0, rwkv_w2, rwkv_a0, rwkv_a2, rwkv_g2, rwkv_k_k,
           rwkv_k_a, rwkv_r_k, rwkv_ln_w, rwkv_ln_b, q_norm, k_norm, lambda_q1, lambda_k1, lambda_q2,
           lambda_k2, subln_w, w_branch_rwkv, w_branch_diff, w_out, norm_ffn, w_group, b_group,
           w_expert_router, b_expert_router, w_gate, w_up, w_down, norm_ple, w_ple_gate, w_ple_proj):
    stacked = dict(zip(_PARAM_NAMES, (
        norm_mix, w_in, rwkv_mu, rwkv_w0, rwkv_w2, rwkv_a0, rwkv_a2, rwkv_g2, rwkv_k_k, rwkv_k_a,
        rwkv_r_k, rwkv_ln_w, rwkv_ln_b, q_norm, k_norm, lambda_q1, lambda_k1, lambda_q2, lambda_k2,
        subln_w, w_branch_rwkv, w_branch_diff, w_out, norm_ffn, w_group, b_group, w_expert_router,
        b_expert_router, w_gate, w_up, w_down, norm_ple, w_ple_gate, w_ple_proj)))
    batch, seq, _ = x.shape
    depth = p.shape[0]
    x2 = x.reshape(batch * seq, D_MODEL)
    for layer in range(depth):
        lam_init = 0.8 - 0.6 * math.exp(-0.3 * layer)
        prm = {name: val[layer] for name, val in stacked.items()}
        x2 = _layer(x2, p[layer].reshape(batch * seq, PLE_DIM), batch, seq, lam_init, prm)
    return x2.reshape(batch, seq, D_MODEL)
```
